```python
import functools
import jax, jax.numpy as jnp
from jax import lax
import numpy as np

D_MODEL = 1024
BATCH = 2
SEQ = 8192
DEPTH = 1
DEC_BATCH = 128
DEC_SEQ = 1
PAST_LEN = 16384
PAGE_SIZE = 128

RET_HEADS = 4
RET_DK = 128
RET_DV = 256
RET_CHUNK = 128
RET_QK_W = RET_HEADS * RET_DK
RET_V_W = RET_HEADS * RET_DV
MLA_HEADS = 8
MLA_Q_RANK = 256
MLA_KV_RANK = 256
MLA_NOPE = 64
MLA_ROPE = 32
MLA_DV = 64
MLA_V_W = MLA_HEADS * MLA_DV
Q_BLOCK = 128
N_GROUPS = 4
EXPERTS_PER_GROUP = 8
N_EXPERTS = N_GROUPS * EXPERTS_PER_GROUP
TOP_K_IN_GROUP = 2
D_EXPERT = 256
ROPE_BASE = 10000.0
EPS = 1e-6
IN_SIZES = (RET_QK_W, RET_QK_W, RET_V_W, RET_V_W, MLA_Q_RANK, MLA_KV_RANK, MLA_ROPE, D_MODEL, D_MODEL)
IN_W = sum(IN_SIZES)

kernel_name = 'hybrid_retention_mla_hmoe_step'


def rmsnorm(x, g):
    xf = x.astype(jnp.float32)
    xf = xf * lax.rsqrt(jnp.mean(xf * xf, axis=-1, keepdims=True) + EPS)
    return xf.astype(x.dtype) * g


def rope(x, pos):
    half = x.shape[-1] // 2
    inv = ROPE_BASE ** (-jnp.arange(half, dtype=jnp.float32) / half)
    ang = pos[:, None] * inv[None, :]
    cos = jnp.cos(ang)[None, :, None, :]
    sin = jnp.sin(ang)[None, :, None, :]
    x1 = x[..., :half].astype(jnp.float32)
    x2 = x[..., half:].astype(jnp.float32)
    return jnp.concatenate([x1 * cos - x2 * sin, x2 * cos + x1 * sin], axis=-1).astype(x.dtype)


def split_in(z):
    points, acc = [], 0
    for s in IN_SIZES[:-1]:
        acc += s
        points.append(acc)
    return jnp.split(z, points, axis=-1)


def retention_chunkwise(q, k, v, state0, chunk):
    B, T, H, dk = q.shape
    dv = v.shape[-1]
    n = T // chunk
    log_g = jnp.log1p(-jnp.exp2(-5.0 - jnp.arange(H, dtype=jnp.float32)))
    idx = jnp.arange(chunk, dtype=jnp.float32)
    diff = idx[:, None] - idx[None, :]
    decay_mask = jnp.where(diff >= 0, jnp.exp(log_g[:, None, None] * jnp.maximum(diff, 0.0)), 0.0)
    q_decay = jnp.exp(log_g[:, None] * (idx + 1.0)).T[None, :, :, None]
    k_decay = jnp.exp(log_g[:, None] * (chunk - 1.0 - idx)).T[None, :, :, None]
    chunk_decay = jnp.exp(log_g * chunk)[None, :, None, None]

    def to_chunks(a):
        return a.reshape(B, n, chunk, H, a.shape[-1]).transpose(1, 0, 2, 3, 4)

    def step(S, inp):
        qc, kc, vc = inp
        s = jnp.einsum('bihd,bjhd->bhij', qc, kc) * decay_mask
        o = jnp.einsum('bhij,bjhv->bihv', s, vc) + jnp.einsum('bihd,bhdv->bihv', qc, S) * q_decay
        S = S * chunk_decay + jnp.einsum('bjhd,bjhv->bhdv', kc * k_decay, vc)
        return S, o.astype(jnp.float32)

    S, o = lax.scan(step, state0.astype(jnp.float32), (to_chunks(q), to_chunks(k), to_chunks(v)))
    o = o.transpose(1, 0, 2, 3, 4).reshape(B, T, H, dv).astype(v.dtype)
    return o, S


def mla_prompt_attend(q_lat, q_pe, ckv, kr):
    B, T, H, C = q_lat.shape
    scale = (MLA_NOPE + MLA_ROPE) ** -0.5
    key_pos = jnp.arange(T)

    def block(i):
        start = i * Q_BLOCK
        ql = lax.dynamic_slice_in_dim(q_lat, start, Q_BLOCK, axis=1)
        qp = lax.dynamic_slice_in_dim(q_pe, start, Q_BLOCK, axis=1)
        s = (jnp.einsum('bqhc,bkc->bhqk', ql, ckv) + jnp.einsum('bqhr,bkr->bhqk', qp, kr)).astype(jnp.float32) * scale
        qpos = start + jnp.arange(Q_BLOCK)
        s = jnp.where(key_pos[None, :] <= qpos[:, None], s, -jnp.inf)
        p = jax.nn.softmax(s, axis=-1).astype(ckv.dtype)
        return jnp.einsum('bhqk,bkc->bqhc', p, ckv)

    o = lax.map(block, jnp.arange(T // Q_BLOCK))
    return o.transpose(1, 0, 2, 3, 4).reshape(B, T, H, C)


def mla_sample_attend(q_lat, q_pe, ckv, kr, c_past, r_past):
    Tn = q_lat.shape[1]
    P = c_past.shape[1]
    scale = (MLA_NOPE + MLA_ROPE) ** -0.5
    s_past = jnp.einsum('bqhc,bkc->bhqk', q_lat, c_past) + jnp.einsum('bqhr,bkr->bhqk', q_pe, r_past)
    s_new = jnp.einsum('bqhc,bkc->bhqk', q_lat, ckv) + jnp.einsum('bqhr,bkr->bhqk', q_pe, kr)
    causal = jnp.arange(Tn)[None, :] <= jnp.arange(Tn)[:, None]
    s_new = jnp.where(causal, s_new.astype(jnp.float32) * scale, -jnp.inf)
    s = jnp.concatenate([s_past.astype(jnp.float32) * scale, s_new], axis=-1)
    p = jax.nn.softmax(s, axis=-1).astype(ckv.dtype)
    return jnp.einsum('bhqk,bkc->bqhc', p[..., :P], c_past) + jnp.einsum('bhqk,bkc->bqhc', p[..., P:], ckv)


def hier_moe(h, w_router_group, w_router_expert, w_gate, w_up, w_down):
    shp = h.shape
    t = h.reshape(-1, shp[-1])
    N = t.shape[0]
    g_logits = (t @ w_router_group).astype(jnp.float32)
    g_prob = jax.nn.softmax(g_logits, axis=-1)
    g_idx = jnp.argmax(g_logits, axis=-1)
    g_w = jnp.take_along_axis(g_prob, g_idx[:, None], axis=1)[:, 0]
    e_logits = (t @ w_router_expert).astype(jnp.float32).reshape(N, N_GROUPS, EXPERTS_PER_GROUP)
    e_in = jnp.take_along_axis(e_logits, g_idx[:, None, None], axis=1)[:, 0]
    top_v, top_i = lax.top_k(e_in, TOP_K_IN_GROUP)
    top_w = jax.nn.softmax(top_v, axis=-1) * g_w[:, None]
    expert_id = g_idx[:, None] * EXPERTS_PER_GROUP + top_i
    combine = jnp.sum(jax.nn.one_hot(expert_id, N_EXPERTS, dtype=jnp.float32) * top_w[..., None], axis=1).astype(t.dtype)
    hg = jnp.einsum('nd,edf->nef', t, w_gate)
    hu = jnp.einsum('nd,edf->nef', t, w_up)
    a = jax.nn.silu(hg) * hu * combine[:, :, None]
    return jnp.einsum('nef,efd->nd', a, w_down).reshape(shp)


def decoder_layer(x, pos, ret_state0, ret_chunk, mla_attend,
                  norm_mix_g, w_in, q_norm_g, kv_norm_g, w_uq, w_ukv, ret_gn_g,
                  w_ret_o, w_mla_o, w_out, norm_ffn_g, w_router_group, w_router_expert,
                  w_gate, w_up, w_down):
    B, T, _ = x.shape
    h = rmsnorm(x, norm_mix_g)
    rq, rk, rv, rg, cq, ckv, kr, g_ret, g_mla = split_in(h @ w_in)
    rq = rope(rq.reshape(B, T, RET_HEADS, RET_DK), pos)
    rk = rope(rk.reshape(B, T, RET_HEADS, RET_DK), pos) * (RET_DK ** -0.5)
    rv = rv.reshape(B, T, RET_HEADS, RET_DV)
    ret_o, ret_state = retention_chunkwise(rq, rk, rv, ret_state0, ret_chunk)
    ret_o = rmsnorm(ret_o, ret_gn_g).reshape(B, T, RET_V_W) * jax.nn.silu(rg)
    q = (rmsnorm(cq, q_norm_g) @ w_uq).reshape(B, T, MLA_HEADS, MLA_NOPE + MLA_ROPE)
    q_nope = q[..., :MLA_NOPE]
    q_pe = rope(q[..., MLA_NOPE:], pos)
    ckv = rmsnorm(ckv, kv_norm_g)
    kr = rope(kr[:, :, None, :], pos)[:, :, 0, :]
    q_lat = jnp.einsum('bthn,chn->bthc', q_nope, w_ukv[..., :MLA_NOPE])
    o_lat = mla_attend(q_lat, q_pe, ckv, kr)
    mla_o = jnp.einsum('bthc,chv->bthv', o_lat, w_ukv[..., MLA_NOPE:]).reshape(B, T, MLA_V_W)
    mixed = jax.nn.sigmoid(g_ret) * (ret_o @ w_ret_o) + jax.nn.sigmoid(g_mla) * (mla_o @ w_mla_o)
    x = x + mixed @ w_out
    x = x + hier_moe(rmsnorm(x, norm_ffn_g), w_router_group, w_router_expert, w_gate, w_up, w_down)
    return x, ckv, kr, ret_state


def setup_inputs(seed: int = 0) -> dict:
    key = jax.random.key(seed)
    ks = jax.random.split(key, 24)
    f32 = jnp.float32
    n_pages = PAST_LEN // PAGE_SIZE
    n_used = DEC_BATCH * n_pages
    n_phys = (n_used * 5) // 4

    def nrm(k, shape, scale):
        return jax.random.normal(k, shape, f32) * scale

    def gain(k, shape):
        return 1.0 + 0.02 * jax.random.normal(k, shape, f32)

    page_table = jax.random.permutation(ks[5], n_phys)[:n_used].reshape(DEC_BATCH, n_pages).astype(jnp.int32)
    return {
        'x_prompt': nrm(ks[0], (BATCH, SEQ, D_MODEL), 1.0),
        'x_sample': nrm(ks[1], (DEC_BATCH, DEC_SEQ, D_MODEL), 1.0),
        'cache_kv_latent': nrm(ks[2], (DEPTH, n_phys, PAGE_SIZE, MLA_KV_RANK), 1.0),
        'cache_k_rope': nrm(ks[3], (DEPTH, n_phys, PAGE_SIZE, MLA_ROPE), 1.0),
        'state_retention': nrm(ks[4], (DEPTH, DEC_BATCH, RET_HEADS, RET_DK, RET_DV), 0.5),
        'page_table': page_table,
        'norm_mix_g': gain(ks[6], (DEPTH, D_MODEL)),
        'w_in': nrm(ks[7], (DEPTH, D_MODEL, IN_W), D_MODEL ** -0.5),
        'q_norm_g': gain(ks[8], (DEPTH, MLA_Q_RANK)),
        'kv_norm_g': gain(ks[9], (DEPTH, MLA_KV_RANK)),
        'w_uq': nrm(ks[10], (DEPTH, MLA_Q_RANK, MLA_HEADS * (MLA_NOPE + MLA_ROPE)), MLA_Q_RANK ** -0.5),
        'w_ukv': nrm(ks[11], (DEPTH, MLA_KV_RANK, MLA_HEADS, MLA_NOPE + MLA_DV), MLA_KV_RANK ** -0.5),
        'ret_gn_g': gain(ks[12], (DEPTH, RET_HEADS, RET_DV)),
        'w_ret_o': nrm(ks[13], (DEPTH, RET_V_W, D_MODEL), RET_V_W ** -0.5),
        'w_mla_o': nrm(ks[14], (DEPTH, MLA_V_W, D_MODEL), MLA_V_W ** -0.5),
        'w_out': nrm(ks[15], (DEPTH, D_MODEL, D_MODEL), D_MODEL ** -0.5),
        'norm_ffn_g': gain(ks[16], (DEPTH, D_MODEL)),
        'w_router_group': nrm(ks[17], (DEPTH, D_MODEL, N_GROUPS), D_MODEL ** -0.5),
        'w_router_expert': nrm(ks[18], (DEPTH, D_MODEL, N_EXPERTS), D_MODEL ** -0.5),
        'w_gate': nrm(ks[19], (DEPTH, N_EXPERTS, D_MODEL, D_EXPERT), D_MODEL ** -0.5),
        'w_up': nrm(ks[20], (DEPTH, N_EXPERTS, D_MODEL, D_EXPERT), D_MODEL ** -0.5),
        'w_down': nrm(ks[21], (DEPTH, N_EXPERTS, D_EXPERT, D_MODEL), D_EXPERT ** -0.5),
        'norm_final_g': gain(ks[22], (D_MODEL,)),
    }


def reference(x_prompt, x_sample, cache_kv_latent, cache_k_rope, state_retention, page_table,
              norm_mix_g, w_in, q_norm_g, kv_norm_g, w_uq, w_ukv, ret_gn_g, w_ret_o, w_mla_o,
              w_out, norm_ffn_g, w_router_group, w_router_expert, w_gate, w_up, w_down, norm_final_g):
    B, T, _ = x_prompt.shape
    Bd, Tn, _ = x_sample.shape
    past_len = page_table.shape[1] * PAGE_SIZE
    pos_p = jnp.arange(T, dtype=jnp.float32)
    pos_s = past_len + jnp.arange(Tn, dtype=jnp.float32)
    xp, xs = x_prompt, x_sample
    lat_p, rope_p, ret_p, lat_s, rope_s, ret_s = [], [], [], [], [], []
    for l in range(DEPTH):
        lw = (norm_mix_g[l], w_in[l], q_norm_g[l], kv_norm_g[l], w_uq[l], w_ukv[l], ret_gn_g[l],
              w_ret_o[l], w_mla_o[l], w_out[l], norm_ffn_g[l], w_router_group[l], w_router_expert[l],
              w_gate[l], w_up[l], w_down[l])
        ret0 = jnp.zeros((B, RET_HEADS, RET_DK, RET_DV), jnp.float32)
        xp, ckv_p, kr_p, st_p = decoder_layer(xp, pos_p, ret0, RET_CHUNK, mla_prompt_attend, *lw)
        c_past = cache_kv_latent[l][page_table].reshape(Bd, past_len, MLA_KV_RANK)
        r_past = cache_k_rope[l][page_table].reshape(Bd, past_len, MLA_ROPE)
        attend_s = functools.partial(mla_sample_attend, c_past=c_past, r_past=r_past)
        xs, ckv_s, kr_s, st_s = decoder_layer(xs, pos_s, state_retention[l], Tn, attend_s, *lw)
        lat_p.append(ckv_p.reshape(B, T // PAGE_SIZE, PAGE_SIZE, MLA_KV_RANK))
        rope_p.append(kr_p.reshape(B, T // PAGE_SIZE, PAGE_SIZE, MLA_ROPE))
        ret_p.append(st_p)
        lat_s.append(ckv_s)
        rope_s.append(kr_s)
        ret_s.append(st_s)
    y_prompt = rmsnorm(xp, norm_final_g)
    y_sample = rmsnorm(xs, norm_final_g)
    return (y_prompt, y_sample, jnp.stack(lat_p), jnp.stack(rope_p), jnp.stack(ret_p),
            jnp.stack(lat_s), jnp.stack(rope_s), jnp.stack(ret_s))
```

```python
import functools
import math

import numpy as np
import jax
import jax.numpy as jnp
from jax import lax
from jax.experimental import pallas as pl
from jax.experimental.pallas import tpu as pltpu

F32 = jnp.float32
BF16 = jnp.bfloat16

ROPE_BASE = 10000.0
EPS = 1e-6
RET_CHUNK = 128
HEAD_PAD = 128
VMEM_LIMIT = 48 * 1024 * 1024


def _cparams(n_axes):
    return pltpu.CompilerParams(dimension_semantics=("arbitrary",) * n_axes,
                                vmem_limit_bytes=VMEM_LIMIT)


def _const_spec(shape):
    nd = len(shape)
    return pl.BlockSpec(shape, lambda *_: (0,) * nd, pipeline_mode=pl.Buffered(1))


def _rms(x, g):
    return x * lax.rsqrt(jnp.mean(x * x, axis=-1, keepdims=True) + EPS) * g


def _sigmoid(x):
    return 1.0 / (1.0 + jnp.exp(-x))


def _dot(a, b):
    return jnp.dot(a, b, preferred_element_type=F32)


def _dot_nt(a, b):
    return lax.dot_general(a, b, (((1,), (1,)), ((), ())), preferred_element_type=F32)


def _inproj_kernel(x_ref, gmix_ref, wmain_ref, wsmall_ref, gq_ref, gkv_ref, wq_ref, wqsw_ref,
                   wk_ref, wv_ref, cosr_ref, sinr_ref, cp_ref, sp_ref,
                   rq_o, rk_o, rv_o, rgs_o, gret_o, gmla_o, lat_o, rope_o, q_o, k_o, v_o,
                   *, n_rh, dk, dv, q_rank, kv_rank, rope_dim, n_mh, nope, k_scale, q_scale):
    x = x_ref[...]
    h = _rms(x, gmix_ref[...]).astype(BF16)
    cosr = cosr_ref[...]
    sinr = sinr_ref[...]
    qk_w = n_rh * dk
    v_w = n_rh * dv

    def rope_heads(z, scale):
        outs = []
        for i in range(n_rh):
            seg = z[:, i * dk:(i + 1) * dk]
            rot = seg * cosr + pltpu.roll(seg, dk // 2, 1) * sinr
            outs.append(rot if scale is None else rot * scale)
        return jnp.concatenate(outs, axis=1)

    off = 0
    rq_o[...] = rope_heads(_dot(h, wmain_ref[:, off:off + qk_w]), None).astype(BF16)
    off += qk_w
    rk_o[...] = rope_heads(_dot(h, wmain_ref[:, off:off + qk_w]), k_scale).astype(BF16)
    off += qk_w
    rv_o[...] = _dot(h, wmain_ref[:, off:off + v_w]).astype(BF16)
    off += v_w
    rg = _dot(h, wmain_ref[:, off:off + v_w])
    rgs_o[...] = (rg * _sigmoid(rg)).astype(BF16)
    off += v_w
    d_model = x.shape[1]
    gret_o[...] = _sigmoid(_dot(h, wmain_ref[:, off:off + d_model])).astype(BF16)
    off += d_model
    gmla_o[...] = _sigmoid(_dot(h, wmain_ref[:, off:off + d_model])).astype(BF16)

    zs = _dot(h, wsmall_ref[...])
    cq = zs[:, :q_rank]
    ckv = zs[:, q_rank:q_rank + kv_rank]
    krp = zs[:, q_rank + kv_rank:q_rank + kv_rank + HEAD_PAD]
    krsw = zs[:, q_rank + kv_rank + HEAD_PAD:]
    cp = cp_ref[...]
    sp = sp_ref[...]
    cqn = _rms(cq, gq_ref[...]).astype(BF16)
    ckvn = _rms(ckv, gkv_ref[...])
    lat_o[...] = ckvn
    ckvb = ckvn.astype(BF16)
    kr_rot = krp * cp + krsw * sp
    rope_o[...] = pltpu.roll(kr_rot, HEAD_PAD - nope, 1)[:, :rope_dim]

    qh = _dot(cqn, wq_ref[...])
    qs = _dot(cqn, wqsw_ref[...])
    kh = _dot(ckvb, wk_ref[...])
    vh = _dot(ckvb, wv_ref[...])
    for i in range(n_mh):
        sl = slice(i * HEAD_PAD, (i + 1) * HEAD_PAD)
        q_o[i] = ((qh[:, sl] * cp + qs[:, sl] * sp) * q_scale).astype(BF16)
        k_o[i] = (kh[:, sl] + kr_rot).astype(BF16)
        v_o[i] = vh[:, sl].astype(BF16)


def _inproj(x, tables, wts, dims, tm):
    n, d_model = x.shape
    cosr, sinr, cp, sp = tables
    tab_blocks = cosr.shape[0] // tm
    n_rh, dk, dv, q_rank, kv_rank, rope_dim, n_mh, nope = dims
    grid = (n // tm,)
    tok = lambda w: pl.BlockSpec((tm, w), lambda i: (i, 0))
    tab = lambda w: pl.BlockSpec((tm, w), lambda i: (i % tab_blocks, 0))
    head = pl.BlockSpec((n_mh, tm, HEAD_PAD), lambda i: (0, i, 0))
    w_names = ("gmix", "wmain", "wsmall", "gq", "gkv", "wq", "wqsw", "wk", "wv")
    w_args = [wts[k] for k in w_names]
    in_specs = ([tok(d_model)] + [_const_spec(a.shape) for a in w_args]
                + [tab(dk), tab(dk), tab(HEAD_PAD), tab(HEAD_PAD)])
    qk_w, v_w = n_rh * dk, n_rh * dv
    out_shape = (
        jax.ShapeDtypeStruct((n, qk_w), BF16), jax.ShapeDtypeStruct((n, qk_w), BF16),
        jax.ShapeDtypeStruct((n, v_w), BF16), jax.ShapeDtypeStruct((n, v_w), BF16),
        jax.ShapeDtypeStruct((n, d_model), BF16), jax.ShapeDtypeStruct((n, d_model), BF16),
        jax.ShapeDtypeStruct((n, kv_rank), F32), jax.ShapeDtypeStruct((n, rope_dim), F32),
        jax.ShapeDtypeStruct((n_mh, n, HEAD_PAD), BF16), jax.ShapeDtypeStruct((n_mh, n, HEAD_PAD), BF16),
        jax.ShapeDtypeStruct((n_mh, n, HEAD_PAD), BF16),
    )
    out_specs = (tok(qk_w), tok(qk_w), tok(v_w), tok(v_w), tok(d_model), tok(d_model),
                 tok(kv_rank), tok(rope_dim), head, head, head)
    kern = functools.partial(_inproj_kernel, n_rh=n_rh, dk=dk, dv=dv, q_rank=q_rank, kv_rank=kv_rank,
                             rope_dim=rope_dim, n_mh=n_mh, nope=nope, k_scale=dk ** -0.5,
                             q_scale=(nope + rope_dim) ** -0.5)
    return pl.pallas_call(kern, out_shape=out_shape, grid=grid, in_specs=in_specs, out_specs=out_specs,
                          compiler_params=_cparams(1), name="inproj")(x, *w_args, cosr, sinr, cp, sp)


def _ret_prompt_kernel(q_ref, k_ref, v_ref, gate_ref, mask_ref, qd_ref, kd_ref, gn_ref,
                       o_ref, st_ref, s_scr, *, n_rh, dk, dv, chunk_decay):
    c = pl.program_id(1)

    @pl.when(c == 0)
    def _():
        s_scr[...] = jnp.zeros_like(s_scr)

    for i in range(n_rh):
        q = q_ref[0, :, i * dk:(i + 1) * dk]
        k = k_ref[0, :, i * dk:(i + 1) * dk]
        v = v_ref[0, :, i * dv:(i + 1) * dv]
        s = _dot_nt(q, k) * mask_ref[i]
        state = s_scr[i]
        o = _dot(s.astype(BF16), v) + _dot(q, state.astype(BF16)) * qd_ref[i]
        kd = (k.astype(F32) * kd_ref[i]).T.astype(BF16)
        s_scr[i] = state * chunk_decay[i] + _dot(kd, v)
        o = _rms(o, gn_ref[i]) * gate_ref[0, :, i * dv:(i + 1) * dv].astype(F32)
        o_ref[0, :, i * dv:(i + 1) * dv] = o.astype(BF16)

    @pl.when(c == pl.num_programs(1) - 1)
    def _():
        st_ref[0] = s_scr[...]


def _ret_tables(n_rh, chunk, dk, dv):
    log_g = np.log1p(-np.exp2(-5.0 - np.arange(n_rh, dtype=np.float64)))
    idx = np.arange(chunk, dtype=np.float64)
    diff = idx[:, None] - idx[None, :]
    mask = np.where(diff >= 0, np.exp(log_g[:, None, None] * np.maximum(diff, 0.0)), 0.0)
    qd = np.exp(log_g[:, None] * (idx + 1.0))
    kd = np.exp(log_g[:, None] * (chunk - 1.0 - idx))
    cd = np.exp(log_g * chunk)
    qd = np.broadcast_to(qd[:, :, None], (n_rh, chunk, dv))
    kd = np.broadcast_to(kd[:, :, None], (n_rh, chunk, dk))
    return (jnp.asarray(mask, F32), jnp.asarray(qd, F32), jnp.asarray(kd, F32),
            tuple(float(np.float32(v)) for v in cd), tuple(float(np.float32(v)) for v in np.exp(log_g)))


def _ret_prompt(rq, rk, rv, rgs, gn, n_rh, dk, dv):
    b, t, _ = rq.shape
    chunk = RET_CHUNK
    mask, qd, kd, cd, _ = _ret_tables(n_rh, chunk, dk, dv)
    tokb = lambda w: pl.BlockSpec((1, chunk, w), lambda bi, c: (bi, c, 0))
    in_specs = [tokb(n_rh * dk), tokb(n_rh * dk), tokb(n_rh * dv), tokb(n_rh * dv),
                _const_spec(mask.shape), _const_spec(qd.shape), _const_spec(kd.shape),
                _const_spec((n_rh, 1, dv))]
    out_shape = (jax.ShapeDtypeStruct((b, t, n_rh * dv), BF16),
                 jax.ShapeDtypeStruct((b, n_rh, dk, dv), F32))
    out_specs = (tokb(n_rh * dv), pl.BlockSpec((1, n_rh, dk, dv), lambda bi, c: (bi, 0, 0, 0)))
    kern = functools.partial(_ret_prompt_kernel, n_rh=n_rh, dk=dk, dv=dv, chunk_decay=cd)
    return pl.pallas_call(kern, out_shape=out_shape, grid=(b, t // chunk), in_specs=in_specs,
                          out_specs=out_specs, scratch_shapes=[pltpu.VMEM((n_rh, dk, dv), F32)],
                          compiler_params=_cparams(2), name="retention_prompt")(
        rq, rk, rv, rgs, mask, qd, kd, gn.reshape(n_rh, 1, dv))


def _ret_sample_kernel(qt_ref, kt_ref, v_ref, gate_ref, st_ref, gn_ref, o_ref, ns_ref,
                       *, n_rh, dk, dv, gamma):
    b = pl.program_id(0)
    lane = lax.broadcasted_iota(jnp.int32, qt_ref.shape, 1)
    sel = lane == b
    qcol = jnp.sum(jnp.where(sel, qt_ref[...], 0.0), axis=1, keepdims=True)
    kcol = jnp.sum(jnp.where(sel, kt_ref[...], 0.0), axis=1, keepdims=True)
    v = v_ref[0]
    gate = gate_ref[0]
    for i in range(n_rh):
        vi = v[:, i * dv:(i + 1) * dv]
        new = st_ref[0, i] * gamma[i] + kcol[i * dk:(i + 1) * dk] * vi
        ns_ref[0, i] = new
        o = jnp.sum(qcol[i * dk:(i + 1) * dk] * new, axis=0, keepdims=True)
        o_ref[0, :, i * dv:(i + 1) * dv] = _rms(o, gn_ref[i]) * gate[:, i * dv:(i + 1) * dv]


def _ret_sample(rq, rk, rv, rgs, state, gn, n_rh, dk, dv):
    nb = rq.shape[0]
    _, _, _, _, gamma = _ret_tables(n_rh, 1, dk, dv)
    qt = rq.astype(F32).T
    kt = rk.astype(F32).T
    row = lambda w: pl.BlockSpec((1, 1, w), lambda b: (b, 0, 0))
    st_spec = pl.BlockSpec((1, n_rh, dk, dv), lambda b: (b, 0, 0, 0))
    in_specs = [_const_spec(qt.shape), _const_spec(kt.shape), row(n_rh * dv), row(n_rh * dv), st_spec,
                _const_spec((n_rh, 1, dv))]
    out_shape = (jax.ShapeDtypeStruct((nb, 1, n_rh * dv), F32),
                 jax.ShapeDtypeStruct((nb, n_rh, dk, dv), F32))
    kern = functools.partial(_ret_sample_kernel, n_rh=n_rh, dk=dk, dv=dv, gamma=gamma)
    o, ns = pl.pallas_call(kern, out_shape=out_shape, grid=(nb,), in_specs=in_specs,
                           out_specs=(row(n_rh * dv), st_spec), compiler_params=_cparams(1),
                           name="retention_sample")(
        qt, kt, rv.astype(F32).reshape(nb, 1, -1), rgs.astype(F32).reshape(nb, 1, -1), state,
        gn.reshape(n_rh, 1, dv))
    return o.reshape(nb, n_rh * dv).astype(BF16), ns


def _attn_prompt_kernel(q_ref, k_ref, v_ref, o_ref, *, tq):
    i = pl.program_id(2)
    q = q_ref[0]

    def update(carry, k, v, masked):
        m, l, acc = carry
        s = _dot_nt(q, k)
        if masked:
            row = lax.broadcasted_iota(jnp.int32, s.shape, 0)
            col = lax.broadcasted_iota(jnp.int32, s.shape, 1)
            s = jnp.where(col <= row, s, -jnp.inf)
        m_new = jnp.maximum(m, jnp.max(s, axis=1, keepdims=True))
        alpha = jnp.exp(m - m_new)
        p = jnp.exp(s - m_new)
        l = alpha * l + jnp.sum(p, axis=1, keepdims=True)
        acc = alpha * acc + _dot(p.astype(BF16), v)
        return m_new, l, acc

    def body(j, carry):
        start = pl.multiple_of(j * tq, tq)
        return update(carry, k_ref[0, pl.ds(start, tq), :], v_ref[0, pl.ds(start, tq), :], False)

    init = (jnp.full((tq, 1), -jnp.inf, F32), jnp.zeros((tq, 1), F32), jnp.zeros((tq, HEAD_PAD), F32))
    carry = lax.fori_loop(0, i, body, init)
    start = pl.multiple_of(i * tq, tq)
    _, l, acc = update(carry, k_ref[0, pl.ds(start, tq), :], v_ref[0, pl.ds(start, tq), :], True)
    o_ref[...] = (acc / l).astype(BF16)


def _attn_prompt(q, k, v, b, t, tq):
    n_mh, n, _ = q.shape
    nq = t // tq
    in_specs = [pl.BlockSpec((1, tq, HEAD_PAD), lambda bi, h, i: (h, bi * nq + i, 0)),
                pl.BlockSpec((1, t, HEAD_PAD), lambda bi, h, i: (h, bi, 0)),
                pl.BlockSpec((1, t, HEAD_PAD), lambda bi, h, i: (h, bi, 0))]
    out_specs = pl.BlockSpec((tq, HEAD_PAD), lambda bi, h, i: (bi * nq + i, h))
    return pl.pallas_call(functools.partial(_attn_prompt_kernel, tq=tq),
                          out_shape=jax.ShapeDtypeStruct((n, n_mh * HEAD_PAD), BF16),
                          grid=(b, n_mh, nq), in_specs=in_specs, out_specs=out_specs,
                          compiler_params=_cparams(3), name="attn_prompt")(q, k, v)


def _attn_sample_kernel(pt_ref, ql_ref, qp_ref, cn_ref, rn_ref, *refs, npg):
    lat_refs = refs[:npg]
    rope_refs = refs[npg:2 * npg]
    o_ref = refs[2 * npg]
    m_scr, l_scr, acc_scr = refs[2 * npg + 1:]
    j = pl.program_id(1)
    ql = ql_ref[0]
    qp = qp_ref[0]

    @pl.when(j == 0)
    def _():
        cn = cn_ref[0]
        s0 = (jnp.sum(ql * cn, axis=1, keepdims=True)
              + jnp.sum(qp * rn_ref[0], axis=1, keepdims=True))
        m_scr[...] = s0
        l_scr[...] = jnp.ones_like(l_scr)
        acc_scr[...] = jnp.broadcast_to(cn, acc_scr.shape)

    s = jnp.concatenate([_dot_nt(ql, lat_refs[i][0]) + _dot_nt(qp, rope_refs[i][0])
                         for i in range(npg)], axis=1)
    m_prev = m_scr[...]
    m_new = jnp.maximum(m_prev, jnp.max(s, axis=1, keepdims=True))
    alpha = jnp.exp(m_prev - m_new)
    p = jnp.exp(s - m_new)
    page = lat_refs[0].shape[1]
    pv = _dot(p[:, :page], lat_refs[0][0])
    for i in range(1, npg):
        pv = pv + _dot(p[:, i * page:(i + 1) * page], lat_refs[i][0])
    m_scr[...] = m_new
    l_scr[...] = alpha * l_scr[...] + jnp.sum(p, axis=1, keepdims=True)
    acc_scr[...] = alpha * acc_scr[...] + pv

    @pl.when(j == pl.num_programs(1) - 1)
    def _():
        o_ref[0] = acc_scr[...] / l_scr[...]


def _attn_sample(page_table, q_lat, q_pe, c_new, r_new, cache_lat, cache_rope, npg):
    nb, n_mh, kv_rank = q_lat.shape
    rope_dim = q_pe.shape[-1]
    n_pages = page_table.shape[1]
    page = cache_lat.shape[1]
    per_seq = lambda shp: pl.BlockSpec((1,) + shp, lambda b, j, pt: (b, 0, 0))
    lat_spec = lambda i: pl.BlockSpec((1, page, kv_rank), lambda b, j, pt: (pt[b, j * npg + i], 0, 0))
    rope_spec = lambda i: pl.BlockSpec((1, page, rope_dim), lambda b, j, pt: (pt[b, j * npg + i], 0, 0))
    in_specs = ([per_seq((n_mh, kv_rank)), per_seq((n_mh, rope_dim)), per_seq((1, kv_rank)),
                 per_seq((1, rope_dim))]
                + [lat_spec(i) for i in range(npg)] + [rope_spec(i) for i in range(npg)])
    grid_spec = pltpu.PrefetchScalarGridSpec(
        num_scalar_prefetch=1, grid=(nb, n_pages // npg), in_specs=in_specs,
        out_specs=per_seq((n_mh, kv_rank)),
        scratch_shapes=[pltpu.VMEM((n_mh, 1), F32), pltpu.VMEM((n_mh, 1), F32),
                        pltpu.VMEM((n_mh, kv_rank), F32)])
    return pl.pallas_call(functools.partial(_attn_sample_kernel, npg=npg),
                          out_shape=jax.ShapeDtypeStruct((nb, n_mh, kv_rank), F32),
                          grid_spec=grid_spec, compiler_params=_cparams(2), name="attn_sample")(
        page_table, q_lat, q_pe, c_new, r_new, *([cache_lat] * npg), *([cache_rope] * npg))


def _headmm_kernel(a_ref, w_ref, o_ref):
    o_ref[0] = _dot(a_ref[0].astype(BF16), w_ref[0]).astype(o_ref.dtype)


def _headmm(a, w, out_dtype):
    nh, m, kk = a.shape
    nn = w.shape[-1]
    spec = lambda r, c: pl.BlockSpec((1, r, c), lambda h: (h, 0, 0))
    return pl.pallas_call(_headmm_kernel, out_shape=jax.ShapeDtypeStruct((nh, m, nn), out_dtype),
                          grid=(nh,), in_specs=[spec(m, kk), spec(kk, nn)], out_specs=spec(m, nn),
                          compiler_params=_cparams(1), name="head_matmul")(a, w)


def _merge_kernel(x_ref, ro_ref, gr_ref, mo_ref, gm_ref, wro_ref, wmo_ref, wout_ref, gffn_ref,
                  wrg_ref, wre_ref, x1_o, h2_o, comb_o, *, group_size):
    mixed = (_dot(ro_ref[...], wro_ref[...]) * gr_ref[...].astype(F32)
             + _dot(mo_ref[...], wmo_ref[...]) * gm_ref[...].astype(F32))
    x1 = x_ref[...] + _dot(mixed.astype(BF16), wout_ref[...])
    x1_o[...] = x1
    h2 = _rms(x1, gffn_ref[...])
    h2_o[...] = h2.astype(BF16)
    ge = jnp.dot(h2, wrg_ref[...], preferred_element_type=F32, precision=lax.Precision.HIGHEST)
    el = jnp.dot(h2, wre_ref[...], preferred_element_type=F32, precision=lax.Precision.HIGHEST)
    n_exp = float(ge.shape[1])
    lane_i = lax.broadcasted_iota(jnp.int32, ge.shape, 1)
    lane = lane_i.astype(F32)
    grp = (lane_i // group_size).astype(F32)
    gmax = jnp.max(ge, axis=1, keepdims=True)
    g_idx = jnp.min(jnp.where(ge == gmax, grp, n_exp), axis=1, keepdims=True)
    g_w = group_size / jnp.sum(jnp.exp(ge - gmax), axis=1, keepdims=True)
    e_in = jnp.where(grp == g_idx, el, -jnp.inf)
    top1 = jnp.max(e_in, axis=1, keepdims=True)
    idx1 = jnp.min(jnp.where(e_in == top1, lane, n_exp), axis=1, keepdims=True)
    e_rest = jnp.where(lane == idx1, -jnp.inf, e_in)
    top2 = jnp.max(e_rest, axis=1, keepdims=True)
    idx2 = jnp.min(jnp.where(e_rest == top2, lane, n_exp), axis=1, keepdims=True)
    e2 = jnp.exp(top2 - top1)
    w1 = g_w / (1.0 + e2)
    w2 = g_w * e2 / (1.0 + e2)
    comb_o[...] = jnp.where(lane == idx1, w1, 0.0) + jnp.where(lane == idx2, w2, 0.0)


def _merge(x, ro, gr, mo, gm, wts, group_size, tm):
    n, d_model = x.shape
    n_exp = wts["wre"].shape[1]
    tok = lambda w: pl.BlockSpec((tm, w), lambda i: (i, 0))
    w_args = [wts[k] for k in ("wro", "wmo", "wout", "gffn", "wrg", "wre")]
    in_specs = [tok(d_model), tok(ro.shape[1]), tok(d_model), tok(mo.shape[1]), tok(d_model)] + [
        _const_spec(a.shape) for a in w_args]
    out_shape = (jax.ShapeDtypeStruct((n, d_model), F32), jax.ShapeDtypeStruct((n, d_model), BF16),
                 jax.ShapeDtypeStruct((n, n_exp), F32))
    return pl.pallas_call(functools.partial(_merge_kernel, group_size=group_size), out_shape=out_shape,
                          grid=(n // tm,), in_specs=in_specs,
                          out_specs=(tok(d_model), tok(d_model), tok(n_exp)),
                          compiler_params=_cparams(1), name="merge_router")(x, ro, gr, mo, gm, *w_args)


def _moe_kernel(h2_ref, comb_ref, x1_ref, wg_ref, wu_ref, wd_ref, gfin_ref, y_ref, acc_scr):
    e = pl.program_id(1)

    @pl.when(e == 0)
    def _():
        acc_scr[...] = x1_ref[...]

    h = h2_ref[...]
    hg = _dot(h, wg_ref[0])
    hu = _dot(h, wu_ref[0])
    comb = comb_ref[...]
    lane = lax.broadcasted_iota(jnp.int32, comb.shape, 1)
    ce = jnp.sum(jnp.where(lane == e, comb, 0.0), axis=1, keepdims=True)
    a = hg * _sigmoid(hg) * hu * ce
    acc_scr[...] += _dot(a.astype(BF16), wd_ref[0])

    @pl.when(e == pl.num_programs(1) - 1)
    def _():
        y_ref[...] = _rms(acc_scr[...], gfin_ref[...])


def _moe(h2, comb, x1, wg, wu, wd, gfin, tm):
    n, d_model = x1.shape
    n_exp, _, d_exp = wg.shape
    tok = lambda w: pl.BlockSpec((tm, w), lambda i, e: (i, 0))
    in_specs = [tok(d_model), tok(n_exp), tok(d_model),
                pl.BlockSpec((1, d_model, d_exp), lambda i, e: (e, 0, 0)),
                pl.BlockSpec((1, d_model, d_exp), lambda i, e: (e, 0, 0)),
                pl.BlockSpec((1, d_exp, d_model), lambda i, e: (e, 0, 0)),
                pl.BlockSpec((1, d_model), lambda i, e: (0, 0))]
    return pl.pallas_call(_moe_kernel, out_shape=jax.ShapeDtypeStruct((n, d_model), F32),
                          grid=(n // tm, n_exp), in_specs=in_specs, out_specs=tok(d_model),
                          scratch_shapes=[pltpu.VMEM((tm, d_model), F32)],
                          compiler_params=_cparams(2), name="moe")(h2, comb, x1, wg, wu, wd, gfin)


def _rope_tables(pos, dk, rope_dim, nope):
    def cs(half):
        inv = ROPE_BASE ** (-jnp.arange(half, dtype=F32) / half)
        ang = pos[:, None] * inv[None, :]
        return jnp.cos(ang), jnp.sin(ang)

    c, s = cs(dk // 2)
    cosr = jnp.concatenate([c, c], axis=1)
    sinr = jnp.concatenate([-s, s], axis=1)
    c, s = cs(rope_dim // 2)
    n = pos.shape[0]
    tail = jnp.zeros((n, HEAD_PAD - nope - rope_dim), F32)
    cp = jnp.concatenate([jnp.ones((n, nope), F32), c, c, tail], axis=1)
    sp = jnp.concatenate([jnp.zeros((n, nope), F32), -s, s, tail], axis=1)
    return cosr, sinr, cp, sp


def _swap_halves(w):
    half = w.shape[-1] // 2
    return jnp.concatenate([w[..., half:], w[..., :half]], axis=-1)


def _layout_weights(norm_mix_g, w_in, q_norm_g, kv_norm_g, w_uq, w_ukv, w_ret_o, w_mla_o, w_out,
                    norm_ffn_g, w_router_group, w_router_expert, dims):
    n_rh, dk, dv, q_rank, kv_rank, rope_dim, n_mh, nope = dims
    d_model = w_in.shape[0]
    dv_m = w_ukv.shape[-1] - nope
    qk_w, v_w = n_rh * dk, n_rh * dv
    o_cq = 2 * qk_w + 2 * v_w
    o_ckv = o_cq + q_rank
    o_kr = o_ckv + kv_rank
    o_gate = o_kr + rope_dim
    pad_tail = HEAD_PAD - nope - rope_dim
    zeros = lambda *shape: jnp.zeros(shape, F32)
    w_kr = w_in[:, o_kr:o_gate]
    place = lambda w: jnp.concatenate([zeros(d_model, nope), w, zeros(d_model, pad_tail)], axis=1)
    wsmall = jnp.concatenate([w_in[:, o_cq:o_kr], place(w_kr), place(_swap_halves(w_kr))], axis=1)
    wmain = jnp.concatenate([w_in[:, :o_cq], w_in[:, o_gate:]], axis=1)

    uq = w_uq.reshape(q_rank, n_mh, nope + rope_dim)
    uq_rope = uq[..., nope:]
    wq = jnp.concatenate([uq, zeros(q_rank, n_mh, pad_tail)], axis=-1)
    wqsw = jnp.concatenate([zeros(q_rank, n_mh, nope), _swap_halves(uq_rope),
                            zeros(q_rank, n_mh, pad_tail)], axis=-1)
    uk = w_ukv[..., :nope]
    uv = w_ukv[..., nope:]
    wk = jnp.concatenate([uk, zeros(kv_rank, n_mh, HEAD_PAD - nope)], axis=-1)
    wv = jnp.concatenate([uv, zeros(kv_rank, n_mh, HEAD_PAD - dv_m)], axis=-1)
    flat = lambda w: w.reshape(w.shape[0], n_mh * HEAD_PAD).astype(BF16)
    wmo = jnp.concatenate([w_mla_o.reshape(n_mh, dv_m, d_model),
                           zeros(n_mh, HEAD_PAD - dv_m, d_model)], axis=1).reshape(n_mh * HEAD_PAD, d_model)
    wabs = jnp.concatenate([jnp.transpose(uk, (1, 2, 0)), zeros(n_mh, HEAD_PAD - nope, kv_rank)], axis=1)
    wvh = jnp.transpose(wv, (1, 0, 2))
    n_grp = w_router_group.shape[1]
    group_size = w_router_expert.shape[1] // n_grp
    return dict(
        gmix=norm_mix_g.reshape(1, -1), wmain=wmain.astype(BF16), wsmall=wsmall.astype(BF16),
        gq=q_norm_g.reshape(1, -1), gkv=kv_norm_g.reshape(1, -1),
        wq=flat(wq), wqsw=flat(wqsw), wk=flat(wk), wv=flat(wv),
        wro=w_ret_o.astype(BF16), wmo=wmo.astype(BF16), wout=w_out.astype(BF16),
        gffn=norm_ffn_g.reshape(1, -1), wrg=jnp.repeat(w_router_group, group_size, axis=1),
        wre=w_router_expert, wabs=wabs.astype(BF16), wvh=wvh.astype(BF16)), group_size


def kernel(x_prompt, x_sample, cache_kv_latent, cache_k_rope, state_retention, page_table, norm_mix_g,
           w_in, q_norm_g, kv_norm_g, w_uq, w_ukv, ret_gn_g, w_ret_o, w_mla_o, w_out, norm_ffn_g,
           w_router_group, w_router_expert, w_gate, w_up, w_down, norm_final_g):
    depth = w_in.shape[0]
    assert depth == 1, "single-layer step only"
    b, t, d_model = x_prompt.shape
    bd, tn, _ = x_sample.shape
    assert tn == 1, "one new token per sample sequence"
    _, _, n_rh, dk, dv = state_retention.shape
    kv_rank = cache_kv_latent.shape[-1]
    rope_dim = cache_k_rope.shape[-1]
    page = cache_kv_latent.shape[2]
    n_mh = w_ukv.shape[2]
    q_rank = w_uq.shape[1]
    nope = w_uq.shape[2] // n_mh - rope_dim
    dims = (n_rh, dk, dv, q_rank, kv_rank, rope_dim, n_mh, nope)
    past_len = page_table.shape[1] * page

    wts, group_size = _layout_weights(norm_mix_g[0], w_in[0], q_norm_g[0], kv_norm_g[0], w_uq[0], w_ukv[0],
                                      w_ret_o[0], w_mla_o[0], w_out[0], norm_ffn_g[0],
                                      w_router_group[0], w_router_expert[0], dims)
    gn = ret_gn_g[0]
    wg, wu, wd = w_gate[0].astype(BF16), w_up[0].astype(BF16), w_down[0].astype(BF16)
    gfin = norm_final_g.reshape(1, -1)

    tm_p = min(256, t)
    xp = x_prompt.reshape(b * t, d_model)
    tabs_p = _rope_tables(jnp.arange(t, dtype=F32), dk, rope_dim, nope)
    (rq, rk, rv, rgs, gret, gmla, lat_p, rope_p, q_p, k_p, v_p) = _inproj(xp, tabs_p, wts, dims, tm_p)
    ret_o, st_p = _ret_prompt(rq.reshape(b, t, -1), rk.reshape(b, t, -1), rv.reshape(b, t, -1),
                              rgs.reshape(b, t, -1), gn, n_rh, dk, dv)
    mla_o = _attn_prompt(q_p, k_p, v_p, b, t, min(256, t))
    x1, h2, comb = _merge(xp, ret_o.reshape(b * t, -1), gret, mla_o, gmla, wts, group_size, tm_p)
    y_prompt = _moe(h2, comb, x1, wg, wu, wd, gfin, min(1024, b * t)).reshape(b, t, d_model)

    xs = x_sample.reshape(bd, d_model)
    tabs_s = _rope_tables(jnp.full((bd,), float(past_len), F32), dk, rope_dim, nope)
    (rq, rk, rv, rgs, gret, gmla, lat_s, rope_s, q_s, _, _) = _inproj(xs, tabs_s, wts, dims, bd)
    ret_o_s, st_s = _ret_sample(rq, rk, rv, rgs, state_retention[0], gn, n_rh, dk, dv)
    q_lat = jnp.transpose(_headmm(q_s, wts["wabs"], F32), (1, 0, 2))
    q_pe = jnp.transpose(q_s[:, :, nope:nope + rope_dim].astype(F32), (1, 0, 2))
    o_lat = _attn_sample(page_table, q_lat, q_pe, lat_s.reshape(bd, 1, kv_rank),
                         rope_s.reshape(bd, 1, rope_dim), cache_kv_latent.reshape(-1, page, kv_rank),
                         cache_k_rope.reshape(-1, page, rope_dim),
                         min(32, page_table.shape[1]))
    mla_o_s = _headmm(jnp.transpose(o_lat, (1, 0, 2)), wts["wvh"], BF16)
    mla_o_s = jnp.transpose(mla_o_s, (1, 0, 2)).reshape(bd, n_mh * HEAD_PAD)
    x1, h2, comb = _merge(xs, ret_o_s, gret, mla_o_s, gmla, wts, group_size, bd)
    y_sample = _moe(h2, comb, x1, wg, wu, wd, gfin, bd).reshape(bd, tn, d_model)

    return (y_prompt, y_sample,
            lat_p.reshape(depth, b, t // page, page, kv_rank),
            rope_p.reshape(depth, b, t // page, page, rope_dim),
            st_p.reshape(depth, b, n_rh, dk, dv),
            lat_s.reshape(depth, bd, tn, kv_rank),
            rope_s.reshape(depth, bd, tn, rope_dim),
            st_s.reshape(depth, bd, n_rh, dk, dv))
```

```python
import functools
import math

import numpy as np
import jax
import jax.numpy as jnp
from jax import lax
from jax.experimental import pallas as pl
from jax.experimental.pallas import tpu as pltpu

F32 = jnp.float32
BF16 = jnp.bfloat16

ROPE_BASE = 10000.0
EPS = 1e-6
RET_CHUNK = 128
HEAD_PAD = 128
ATTN_TQ = 512
ATTN_HEADS_PER_STEP = 4
VMEM_LIMIT = 48 * 1024 * 1024


def _cparams(n_axes):
    return pltpu.CompilerParams(dimension_semantics=("arbitrary",) * n_axes,
                                vmem_limit_bytes=VMEM_LIMIT)


def _const_spec(shape):
    nd = len(shape)
    return pl.BlockSpec(shape, lambda *_: (0,) * nd, pipeline_mode=pl.Buffered(1))


def _rms(x, g):
    return x * lax.rsqrt(jnp.mean(x * x, axis=-1, keepdims=True) + EPS) * g


def _sigmoid(x):
    return 1.0 / (1.0 + jnp.exp(-x))


def _dot(a, b):
    return jnp.dot(a, b, preferred_element_type=F32)


def _dot_nt(a, b):
    return lax.dot_general(a, b, (((1,), (1,)), ((), ())), preferred_element_type=F32)


def _inproj_kernel(x_ref, gmix_ref, wmain_ref, wsmall_ref, gq_ref, gkv_ref, wq_ref, wqsw_ref,
                   wk_ref, wv_ref, cosr_ref, sinr_ref, cp_ref, sp_ref,
                   rq_o, rk_o, rv_o, rgs_o, gret_o, gmla_o, lat_o, rope_o, q_o, k_o, v_o,
                   *, n_rh, dk, dv, q_rank, kv_rank, rope_dim, n_mh, nope, dv_m, k_scale, q_scale):
    x = x_ref[...]
    h = _rms(x, gmix_ref[...]).astype(BF16)
    cosr = cosr_ref[...]
    sinr = sinr_ref[...]
    qk_w = n_rh * dk
    v_w = n_rh * dv

    def rope_heads(z, scale):
        outs = []
        for i in range(n_rh):
            seg = z[:, i * dk:(i + 1) * dk]
            rot = seg * cosr + pltpu.roll(seg, dk // 2, 1) * sinr
            outs.append(rot if scale is None else rot * scale)
        return jnp.concatenate(outs, axis=1)

    off = 0
    rq_o[...] = rope_heads(_dot(h, wmain_ref[:, off:off + qk_w]), None).astype(BF16)
    off += qk_w
    rk_o[...] = rope_heads(_dot(h, wmain_ref[:, off:off + qk_w]), k_scale).astype(BF16)
    off += qk_w
    rv_o[...] = _dot(h, wmain_ref[:, off:off + v_w]).astype(BF16)
    off += v_w
    rg = _dot(h, wmain_ref[:, off:off + v_w])
    rgs_o[...] = (rg * _sigmoid(rg)).astype(BF16)
    off += v_w
    d_model = x.shape[1]
    gret_o[...] = _sigmoid(_dot(h, wmain_ref[:, off:off + d_model])).astype(BF16)
    off += d_model
    gmla_o[...] = _sigmoid(_dot(h, wmain_ref[:, off:off + d_model])).astype(BF16)

    zs = _dot(h, wsmall_ref[...])
    cq = zs[:, :q_rank]
    ckv = zs[:, q_rank:q_rank + kv_rank]
    krp = zs[:, q_rank + kv_rank:q_rank + kv_rank + HEAD_PAD]
    krsw = zs[:, q_rank + kv_rank + HEAD_PAD:]
    cp = cp_ref[...]
    sp = sp_ref[...]
    cqn = _rms(cq, gq_ref[...]).astype(BF16)
    ckvn = _rms(ckv, gkv_ref[...])
    lat_o[...] = ckvn
    ckvb = ckvn.astype(BF16)
    kr_rot = krp * cp + krsw * sp
    rope_o[...] = pltpu.roll(kr_rot, HEAD_PAD - nope, 1)[:, :rope_dim]

    qh = _dot(cqn, wq_ref[...])
    qs = _dot(cqn, wqsw_ref[...])
    kh = _dot(ckvb, wk_ref[...])
    vh = _dot(ckvb, wv_ref[...])
    sum_lane = lax.broadcasted_iota(jnp.int32, cp.shape, 1) == dv_m
    for i in range(n_mh):
        sl = slice(i * HEAD_PAD, (i + 1) * HEAD_PAD)
        q_o[i] = ((qh[:, sl] * cp + qs[:, sl] * sp) * q_scale).astype(BF16)
        k_o[i] = (kh[:, sl] + kr_rot).astype(BF16)
        v_o[i] = jnp.where(sum_lane, 1.0, vh[:, sl]).astype(BF16)


def _inproj(x, tables, wts, dims, tm):
    n, d_model = x.shape
    cosr, sinr, cp, sp = tables
    tab_blocks = cosr.shape[0] // tm
    n_rh, dk, dv, q_rank, kv_rank, rope_dim, n_mh, nope, dv_m = dims
    grid = (n // tm,)
    tok = lambda w: pl.BlockSpec((tm, w), lambda i: (i, 0))
    tab = lambda w: pl.BlockSpec((tm, w), lambda i: (i % tab_blocks, 0))
    head = pl.BlockSpec((n_mh, tm, HEAD_PAD), lambda i: (0, i, 0))
    w_names = ("gmix", "wmain", "wsmall", "gq", "gkv", "wq", "wqsw", "wk", "wv")
    w_args = [wts[k] for k in w_names]
    in_specs = ([tok(d_model)] + [_const_spec(a.shape) for a in w_args]
                + [tab(dk), tab(dk), tab(HEAD_PAD), tab(HEAD_PAD)])
    qk_w, v_w = n_rh * dk, n_rh * dv
    out_shape = (
        jax.ShapeDtypeStruct((n, qk_w), BF16), jax.ShapeDtypeStruct((n, qk_w), BF16),
        jax.ShapeDtypeStruct((n, v_w), BF16), jax.ShapeDtypeStruct((n, v_w), BF16),
        jax.ShapeDtypeStruct((n, d_model), BF16), jax.ShapeDtypeStruct((n, d_model), BF16),
        jax.ShapeDtypeStruct((n, kv_rank), F32), jax.ShapeDtypeStruct((n, rope_dim), F32),
        jax.ShapeDtypeStruct((n_mh, n, HEAD_PAD), BF16), jax.ShapeDtypeStruct((n_mh, n, HEAD_PAD), BF16),
        jax.ShapeDtypeStruct((n_mh, n, HEAD_PAD), BF16),
    )
    out_specs = (tok(qk_w), tok(qk_w), tok(v_w), tok(v_w), tok(d_model), tok(d_model),
                 tok(kv_rank), tok(rope_dim), head, head, head)
    kern = functools.partial(_inproj_kernel, n_rh=n_rh, dk=dk, dv=dv, q_rank=q_rank, kv_rank=kv_rank,
                             rope_dim=rope_dim, n_mh=n_mh, nope=nope, dv_m=dv_m, k_scale=dk ** -0.5,
                             q_scale=(nope + rope_dim) ** -0.5 * math.log2(math.e))
    return pl.pallas_call(kern, out_shape=out_shape, grid=grid, in_specs=in_specs, out_specs=out_specs,
                          compiler_params=_cparams(1), name="inproj")(x, *w_args, cosr, sinr, cp, sp)


def _ret_prompt_kernel(q_ref, k_ref, v_ref, gate_ref, mask_ref, qd_ref, kd_ref, gn_ref,
                       o_ref, st_ref, s_scr, *, n_rh, dk, dv, chunk_decay):
    c = pl.program_id(1)

    @pl.when(c == 0)
    def _():
        s_scr[...] = jnp.zeros_like(s_scr)

    for i in range(n_rh):
        q = q_ref[0, :, i * dk:(i + 1) * dk]
        k = k_ref[0, :, i * dk:(i + 1) * dk]
        v = v_ref[0, :, i * dv:(i + 1) * dv]
        s = _dot_nt(q, k) * mask_ref[i]
        state = s_scr[i]
        o = _dot(s.astype(BF16), v) + _dot(q, state.astype(BF16)) * qd_ref[i]
        kd = (k.astype(F32) * kd_ref[i]).T.astype(BF16)
        s_scr[i] = state * chunk_decay[i] + _dot(kd, v)
        o = _rms(o, gn_ref[i]) * gate_ref[0, :, i * dv:(i + 1) * dv].astype(F32)
        o_ref[0, :, i * dv:(i + 1) * dv] = o.astype(BF16)

    @pl.when(c == pl.num_programs(1) - 1)
    def _():
        st_ref[0] = s_scr[...]


def _ret_tables(n_rh, chunk, dk, dv):
    log_g = np.log1p(-np.exp2(-5.0 - np.arange(n_rh, dtype=np.float64)))
    idx = np.arange(chunk, dtype=np.float64)
    diff = idx[:, None] - idx[None, :]
    mask = np.where(diff >= 0, np.exp(log_g[:, None, None] * np.maximum(diff, 0.0)), 0.0)
    qd = np.exp(log_g[:, None] * (idx + 1.0))
    kd = np.exp(log_g[:, None] * (chunk - 1.0 - idx))
    cd = np.exp(log_g * chunk)
    qd = np.broadcast_to(qd[:, :, None], (n_rh, chunk, dv))
    kd = np.broadcast_to(kd[:, :, None], (n_rh, chunk, dk))
    return (jnp.asarray(mask, F32), jnp.asarray(qd, F32), jnp.asarray(kd, F32),
            tuple(float(np.float32(v)) for v in cd), tuple(float(np.float32(v)) for v in np.exp(log_g)))


def _ret_prompt(rq, rk, rv, rgs, gn, n_rh, dk, dv):
    b, t, _ = rq.shape
    chunk = RET_CHUNK
    mask, qd, kd, cd, _ = _ret_tables(n_rh, chunk, dk, dv)
    tokb = lambda w: pl.BlockSpec((1, chunk, w), lambda bi, c: (bi, c, 0))
    in_specs = [tokb(n_rh * dk), tokb(n_rh * dk), tokb(n_rh * dv), tokb(n_rh * dv),
                _const_spec(mask.shape), _const_spec(qd.shape), _const_spec(kd.shape),
                _const_spec((n_rh, 1, dv))]
    out_shape = (jax.ShapeDtypeStruct((b, t, n_rh * dv), BF16),
                 jax.ShapeDtypeStruct((b, n_rh, dk, dv), F32))
    out_specs = (tokb(n_rh * dv), pl.BlockSpec((1, n_rh, dk, dv), lambda bi, c: (bi, 0, 0, 0)))
    kern = functools.partial(_ret_prompt_kernel, n_rh=n_rh, dk=dk, dv=dv, chunk_decay=cd)
    return pl.pallas_call(kern, out_shape=out_shape, grid=(b, t // chunk), in_specs=in_specs,
                          out_specs=out_specs, scratch_shapes=[pltpu.VMEM((n_rh, dk, dv), F32)],
                          compiler_params=_cparams(2), name="retention_prompt")(
        rq, rk, rv, rgs, mask, qd, kd, gn.reshape(n_rh, 1, dv))


def _ret_sample_kernel(qt_ref, kt_ref, v_ref, gate_ref, st_ref, gn_ref, o_ref, ns_ref,
                       *, n_rh, dk, dv, gamma):
    b = pl.program_id(0)
    lane = lax.broadcasted_iota(jnp.int32, qt_ref.shape, 1)
    sel = lane == b
    qcol = jnp.sum(jnp.where(sel, qt_ref[...], 0.0), axis=1, keepdims=True)
    kcol = jnp.sum(jnp.where(sel, kt_ref[...], 0.0), axis=1, keepdims=True)
    v = v_ref[0]
    gate = gate_ref[0]
    for i in range(n_rh):
        vi = v[:, i * dv:(i + 1) * dv]
        new = st_ref[0, i] * gamma[i] + kcol[i * dk:(i + 1) * dk] * vi
        ns_ref[0, i] = new
        o = jnp.sum(qcol[i * dk:(i + 1) * dk] * new, axis=0, keepdims=True)
        o_ref[0, :, i * dv:(i + 1) * dv] = _rms(o, gn_ref[i]) * gate[:, i * dv:(i + 1) * dv]


def _ret_sample(rq, rk, rv, rgs, state, gn, n_rh, dk, dv):
    nb = rq.shape[0]
    _, _, _, _, gamma = _ret_tables(n_rh, 1, dk, dv)
    qt = rq.astype(F32).T
    kt = rk.astype(F32).T
    row = lambda w: pl.BlockSpec((1, 1, w), lambda b: (b, 0, 0))
    st_spec = pl.BlockSpec((1, n_rh, dk, dv), lambda b: (b, 0, 0, 0))
    in_specs = [_const_spec(qt.shape), _const_spec(kt.shape), row(n_rh * dv), row(n_rh * dv), st_spec,
                _const_spec((n_rh, 1, dv))]
    out_shape = (jax.ShapeDtypeStruct((nb, 1, n_rh * dv), F32),
                 jax.ShapeDtypeStruct((nb, n_rh, dk, dv), F32))
    kern = functools.partial(_ret_sample_kernel, n_rh=n_rh, dk=dk, dv=dv, gamma=gamma)
    o, ns = pl.pallas_call(kern, out_shape=out_shape, grid=(nb,), in_specs=in_specs,
                           out_specs=(row(n_rh * dv), st_spec), compiler_params=_cparams(1),
                           name="retention_sample")(
        qt, kt, rv.astype(F32).reshape(nb, 1, -1), rgs.astype(F32).reshape(nb, 1, -1), state,
        gn.reshape(n_rh, 1, dv))
    return o.reshape(nb, n_rh * dv).astype(BF16), ns


def _attn_prompt_kernel(q_ref, k_ref, v_ref, o_ref, m_scr, acc_scr, *, tq, hpg, sum_lane):
    i = pl.program_id(2)
    m_scr[...] = jnp.full(m_scr.shape, -jnp.inf, F32)
    acc_scr[...] = jnp.zeros(acc_scr.shape, F32)

    def tile(j, masked):
        start = pl.multiple_of(j * tq, tq)
        for h in range(hpg):
            s = _dot_nt(q_ref[h], k_ref[h, pl.ds(start, tq), :])
            if masked:
                row = lax.broadcasted_iota(jnp.int32, s.shape, 0)
                col = lax.broadcasted_iota(jnp.int32, s.shape, 1)
                s = jnp.where(col <= row, s, -jnp.inf)
            m = m_scr[h]
            m_new = jnp.maximum(m, jnp.max(s, axis=1, keepdims=True))
            m_wide = jnp.concatenate([m_new] * (tq // HEAD_PAD), axis=1)
            p = jnp.exp2((s - m_wide).astype(BF16))
            acc_scr[h] = jnp.exp2(m - m_new) * acc_scr[h] + _dot(p, v_ref[h, pl.ds(start, tq), :])
            m_scr[h] = m_new

    def body(j, carry):
        tile(j, False)
        return carry

    lax.fori_loop(0, i, body, 0)
    tile(i, True)
    for h in range(hpg):
        acc = acc_scr[h]
        o_ref[:, h * HEAD_PAD:(h + 1) * HEAD_PAD] = (acc / acc[:, sum_lane:sum_lane + 1]).astype(BF16)


def _attn_prompt(q, k, v, b, t, tq, hpg, sum_lane):
    n_mh, n, _ = q.shape
    nq = t // tq
    kv_spec = pl.BlockSpec((hpg, t, HEAD_PAD), lambda bi, g, i: (g, bi, 0), pipeline_mode=pl.Buffered(1))
    in_specs = [pl.BlockSpec((hpg, tq, HEAD_PAD), lambda bi, g, i: (g, bi * nq + i, 0)), kv_spec, kv_spec]
    out_specs = pl.BlockSpec((tq, hpg * HEAD_PAD), lambda bi, g, i: (bi * nq + i, g))
    return pl.pallas_call(functools.partial(_attn_prompt_kernel, tq=tq, hpg=hpg, sum_lane=sum_lane),
                          out_shape=jax.ShapeDtypeStruct((n, n_mh * HEAD_PAD), BF16),
                          grid=(b, n_mh // hpg, nq), in_specs=in_specs, out_specs=out_specs,
                          scratch_shapes=[pltpu.VMEM((hpg, tq, HEAD_PAD), F32),
                                          pltpu.VMEM((hpg, tq, HEAD_PAD), F32)],
                          compiler_params=_cparams(3), name="attn_prompt")(q, k, v)


def _attn_sample_kernel(pt_ref, ql_ref, qp_ref, cn_ref, rn_ref, lat_hbm, rope_hbm, o_ref,
                        lat_slab, rope_slab, sems, m_scr, l_scr, acc_scr, *, npg, n_chunks, page):
    step = pl.program_id(0)
    slot = step % 2
    chunk = step % n_chunks

    def page_copies(page_ids, slot_):
        out = []
        for i in range(npg):
            rows = pl.ds(i * page, page)
            out.append(pltpu.make_async_copy(lat_hbm.at[page_ids(i)], lat_slab.at[slot_, rows, :],
                                             sems.at[0, slot_]))
            out.append(pltpu.make_async_copy(rope_hbm.at[page_ids(i)], rope_slab.at[slot_, :, rows],
                                             sems.at[1, slot_]))
        return out

    def start_step(step_, slot_):
        seq = step_ // n_chunks
        first = (step_ % n_chunks) * npg
        for cp in page_copies(lambda i: pt_ref[seq, first + i], slot_):
            cp.start()

    @pl.when(step == 0)
    def _():
        start_step(step, slot)

    @pl.when(step + 1 < pl.num_programs(0))
    def _():
        start_step(step + 1, 1 - slot)

    for cp in page_copies(lambda i: 0, slot):
        cp.wait()

    ql = ql_ref[0]
    qp = qp_ref[0]

    @pl.when(chunk == 0)
    def _():
        cn = cn_ref[0]
        s0 = (jnp.sum(ql * cn, axis=1, keepdims=True)
              + jnp.sum(qp * rn_ref[0], axis=1, keepdims=True))
        m_scr[...] = s0
        l_scr[...] = jnp.ones_like(l_scr)
        acc_scr[...] = jnp.broadcast_to(cn, acc_scr.shape)

    lat = lat_slab[slot].astype(BF16)
    s = _dot_nt(ql.astype(BF16), lat) + _dot(qp, rope_slab[slot])
    m_prev = m_scr[...]
    m_new = jnp.maximum(m_prev, jnp.max(s, axis=1, keepdims=True))
    alpha = jnp.exp2(m_prev - m_new)
    p = jnp.exp2(s - m_new)
    m_scr[...] = m_new
    l_scr[...] = alpha * l_scr[...] + jnp.sum(p, axis=1, keepdims=True)
    acc_scr[...] = alpha * acc_scr[...] + _dot(p.astype(BF16), lat)

    @pl.when(chunk == n_chunks - 1)
    def _():
        o_ref[0] = acc_scr[...] / l_scr[...]


def _attn_sample(page_table, q_lat, q_pe, c_new, r_new, cache_lat, cache_rope_t, npg):
    nb, n_mh, kv_rank = q_lat.shape
    rope_dim = q_pe.shape[-1]
    n_pages = page_table.shape[1]
    page = cache_lat.shape[1]
    n_chunks = n_pages // npg
    per_seq = lambda shp: pl.BlockSpec((1,) + shp, lambda s, pt: (s // n_chunks, 0, 0))
    hbm = pl.BlockSpec(memory_space=pl.ANY)
    in_specs = [per_seq((n_mh, kv_rank)), per_seq((n_mh, rope_dim)), per_seq((1, kv_rank)),
                per_seq((1, rope_dim)), hbm, hbm]
    grid_spec = pltpu.PrefetchScalarGridSpec(
        num_scalar_prefetch=1, grid=(nb * n_chunks,), in_specs=in_specs,
        out_specs=per_seq((n_mh, kv_rank)),
        scratch_shapes=[pltpu.VMEM((2, npg * page, kv_rank), F32),
                        pltpu.VMEM((2, rope_dim, npg * page), F32),
                        pltpu.SemaphoreType.DMA((2, 2)),
                        pltpu.VMEM((n_mh, 1), F32), pltpu.VMEM((n_mh, 1), F32),
                        pltpu.VMEM((n_mh, kv_rank), F32)])
    kern = functools.partial(_attn_sample_kernel, npg=npg, n_chunks=n_chunks, page=page)
    return pl.pallas_call(kern, out_shape=jax.ShapeDtypeStruct((nb, n_mh, kv_rank), F32),
                          grid_spec=grid_spec, compiler_params=_cparams(1), name="attn_sample")(
        page_table, q_lat, q_pe, c_new, r_new, cache_lat, cache_rope_t)


def _headmm_kernel(a_ref, w_ref, o_ref):
    o_ref[0] = _dot(a_ref[0].astype(BF16), w_ref[0]).astype(o_ref.dtype)


def _headmm(a, w, out_dtype):
    nh, m, kk = a.shape
    nn = w.shape[-1]
    spec = lambda r, c: pl.BlockSpec((1, r, c), lambda h: (h, 0, 0))
    return pl.pallas_call(_headmm_kernel, out_shape=jax.ShapeDtypeStruct((nh, m, nn), out_dtype),
                          grid=(nh,), in_specs=[spec(m, kk), spec(kk, nn)], out_specs=spec(m, nn),
                          compiler_params=_cparams(1), name="head_matmul")(a, w)


def _merge_kernel(x_ref, ro_ref, gr_ref, mo_ref, gm_ref, wro_ref, wmo_ref, wout_ref, gffn_ref,
                  wrg_ref, wre_ref, x1_o, h2_o, comb_o, *, group_size):
    mixed = (_dot(ro_ref[...], wro_ref[...]) * gr_ref[...].astype(F32)
             + _dot(mo_ref[...], wmo_ref[...]) * gm_ref[...].astype(F32))
    x1 = x_ref[...] + _dot(mixed.astype(BF16), wout_ref[...])
    x1_o[...] = x1
    h2 = _rms(x1, gffn_ref[...])
    h2_o[...] = h2.astype(BF16)
    ge = jnp.dot(h2, wrg_ref[...], preferred_element_type=F32, precision=lax.Precision.HIGHEST)
    el = jnp.dot(h2, wre_ref[...], preferred_element_type=F32, precision=lax.Precision.HIGHEST)
    n_exp = float(ge.shape[1])
    lane_i = lax.broadcasted_iota(jnp.int32, ge.shape, 1)
    lane = lane_i.astype(F32)
    grp = (lane_i // group_size).astype(F32)
    gmax = jnp.max(ge, axis=1, keepdims=True)
    g_idx = jnp.min(jnp.where(ge == gmax, grp, n_exp), axis=1, keepdims=True)
    g_w = group_size / jnp.sum(jnp.exp(ge - gmax), axis=1, keepdims=True)
    e_in = jnp.where(grp == g_idx, el, -jnp.inf)
    top1 = jnp.max(e_in, axis=1, keepdims=True)
    idx1 = jnp.min(jnp.where(e_in == top1, lane, n_exp), axis=1, keepdims=True)
    e_rest = jnp.where(lane == idx1, -jnp.inf, e_in)
    top2 = jnp.max(e_rest, axis=1, keepdims=True)
    idx2 = jnp.min(jnp.where(e_rest == top2, lane, n_exp), axis=1, keepdims=True)
    e2 = jnp.exp(top2 - top1)
    w1 = g_w / (1.0 + e2)
    w2 = g_w * e2 / (1.0 + e2)
    comb_o[...] = jnp.where(lane == idx1, w1, 0.0) + jnp.where(lane == idx2, w2, 0.0)


def _merge(x, ro, gr, mo, gm, wts, group_size, tm):
    n, d_model = x.shape
    n_exp = wts["wre"].shape[1]
    tok = lambda w: pl.BlockSpec((tm, w), lambda i: (i, 0))
    w_args = [wts[k] for k in ("wro", "wmo", "wout", "gffn", "wrg", "wre")]
    in_specs = [tok(d_model), tok(ro.shape[1]), tok(d_model), tok(mo.shape[1]), tok(d_model)] + [
        _const_spec(a.shape) for a in w_args]
    out_shape = (jax.ShapeDtypeStruct((n, d_model), F32), jax.ShapeDtypeStruct((n, d_model), BF16),
                 jax.ShapeDtypeStruct((n, n_exp), F32))
    return pl.pallas_call(functools.partial(_merge_kernel, group_size=group_size), out_shape=out_shape,
                          grid=(n // tm,), in_specs=in_specs,
                          out_specs=(tok(d_model), tok(d_model), tok(n_exp)),
                          compiler_params=_cparams(1), name="merge_router")(x, ro, gr, mo, gm, *w_args)


def _moe_kernel(h2_ref, comb_ref, x1_ref, wg_ref, wu_ref, wd_ref, gfin_ref, y_ref, acc_scr):
    e = pl.program_id(1)

    @pl.when(e == 0)
    def _():
        acc_scr[...] = x1_ref[...]

    h = h2_ref[...]
    hg = _dot(h, wg_ref[0])
    hu = _dot(h, wu_ref[0])
    comb = comb_ref[...]
    lane = lax.broadcasted_iota(jnp.int32, comb.shape, 1)
    ce = jnp.sum(jnp.where(lane == e, comb, 0.0), axis=1, keepdims=True)
    a = hg * _sigmoid(hg) * hu * ce
    acc_scr[...] += _dot(a.astype(BF16), wd_ref[0])

    @pl.when(e == pl.num_programs(1) - 1)
    def _():
        y_ref[...] = _rms(acc_scr[...], gfin_ref[...])


def _moe(h2, comb, x1, wg, wu, wd, gfin, tm):
    n, d_model = x1.shape
    n_exp, _, d_exp = wg.shape
    tok = lambda w: pl.BlockSpec((tm, w), lambda i, e: (i, 0))
    in_specs = [tok(d_model), tok(n_exp), tok(d_model),
                pl.BlockSpec((1, d_model, d_exp), lambda i, e: (e, 0, 0)),
                pl.BlockSpec((1, d_model, d_exp), lambda i, e: (e, 0, 0)),
                pl.BlockSpec((1, d_exp, d_model), lambda i, e: (e, 0, 0)),
                pl.BlockSpec((1, d_model), lambda i, e: (0, 0))]
    return pl.pallas_call(_moe_kernel, out_shape=jax.ShapeDtypeStruct((n, d_model), F32),
                          grid=(n // tm, n_exp), in_specs=in_specs, out_specs=tok(d_model),
                          scratch_shapes=[pltpu.VMEM((tm, d_model), F32)],
                          compiler_params=_cparams(2), name="moe")(h2, comb, x1, wg, wu, wd, gfin)


def _rope_tables(pos, dk, rope_dim, nope):
    def cs(half):
        inv = ROPE_BASE ** (-jnp.arange(half, dtype=F32) / half)
        ang = pos[:, None] * inv[None, :]
        return jnp.cos(ang), jnp.sin(ang)

    c, s = cs(dk // 2)
    cosr = jnp.concatenate([c, c], axis=1)
    sinr = jnp.concatenate([-s, s], axis=1)
    c, s = cs(rope_dim // 2)
    n = pos.shape[0]
    tail = jnp.zeros((n, HEAD_PAD - nope - rope_dim), F32)
    cp = jnp.concatenate([jnp.ones((n, nope), F32), c, c, tail], axis=1)
    sp = jnp.concatenate([jnp.zeros((n, nope), F32), -s, s, tail], axis=1)
    return cosr, sinr, cp, sp


def _swap_halves(w):
    half = w.shape[-1] // 2
    return jnp.concatenate([w[..., half:], w[..., :half]], axis=-1)


def _layout_weights(norm_mix_g, w_in, q_norm_g, kv_norm_g, w_uq, w_ukv, w_ret_o, w_mla_o, w_out,
                    norm_ffn_g, w_router_group, w_router_expert, dims):
    n_rh, dk, dv, q_rank, kv_rank, rope_dim, n_mh, nope, dv_m = dims
    d_model = w_in.shape[0]
    qk_w, v_w = n_rh * dk, n_rh * dv
    o_cq = 2 * qk_w + 2 * v_w
    o_ckv = o_cq + q_rank
    o_kr = o_ckv + kv_rank
    o_gate = o_kr + rope_dim
    pad_tail = HEAD_PAD - nope - rope_dim
    zeros = lambda *shape: jnp.zeros(shape, F32)
    w_kr = w_in[:, o_kr:o_gate]
    place = lambda w: jnp.concatenate([zeros(d_model, nope), w, zeros(d_model, pad_tail)], axis=1)
    wsmall = jnp.concatenate([w_in[:, o_cq:o_kr], place(w_kr), place(_swap_halves(w_kr))], axis=1)
    wmain = jnp.concatenate([w_in[:, :o_cq], w_in[:, o_gate:]], axis=1)

    uq = w_uq.reshape(q_rank, n_mh, nope + rope_dim)
    uq_rope = uq[..., nope:]
    wq = jnp.concatenate([uq, zeros(q_rank, n_mh, pad_tail)], axis=-1)
    wqsw = jnp.concatenate([zeros(q_rank, n_mh, nope), _swap_halves(uq_rope),
                            zeros(q_rank, n_mh, pad_tail)], axis=-1)
    uk = w_ukv[..., :nope]
    uv = w_ukv[..., nope:]
    wk = jnp.concatenate([uk, zeros(kv_rank, n_mh, HEAD_PAD - nope)], axis=-1)
    wv = jnp.concatenate([uv, zeros(kv_rank, n_mh, HEAD_PAD - dv_m)], axis=-1)
    flat = lambda w: w.reshape(w.shape[0], n_mh * HEAD_PAD).astype(BF16)
    wmo = jnp.concatenate([w_mla_o.reshape(n_mh, dv_m, d_model),
                           zeros(n_mh, HEAD_PAD - dv_m, d_model)], axis=1).reshape(n_mh * HEAD_PAD, d_model)
    wabs = jnp.concatenate([jnp.transpose(uk, (1, 2, 0)), zeros(n_mh, HEAD_PAD - nope, kv_rank)], axis=1)
    wvh = jnp.transpose(wv, (1, 0, 2))
    n_grp = w_router_group.shape[1]
    group_size = w_router_expert.shape[1] // n_grp
    return dict(
        gmix=norm_mix_g.reshape(1, -1), wmain=wmain.astype(BF16), wsmall=wsmall.astype(BF16),
        gq=q_norm_g.reshape(1, -1), gkv=kv_norm_g.reshape(1, -1),
        wq=flat(wq), wqsw=flat(wqsw), wk=flat(wk), wv=flat(wv),
        wro=w_ret_o.astype(BF16), wmo=wmo.astype(BF16), wout=w_out.astype(BF16),
        gffn=norm_ffn_g.reshape(1, -1), wrg=jnp.repeat(w_router_group, group_size, axis=1),
        wre=w_router_expert, wabs=wabs.astype(BF16), wvh=wvh.astype(BF16)), group_size


def kernel(x_prompt, x_sample, cache_kv_latent, cache_k_rope, state_retention, page_table, norm_mix_g,
           w_in, q_norm_g, kv_norm_g, w_uq, w_ukv, ret_gn_g, w_ret_o, w_mla_o, w_out, norm_ffn_g,
           w_router_group, w_router_expert, w_gate, w_up, w_down, norm_final_g):
    depth = w_in.shape[0]
    assert depth == 1, "single-layer step only"
    b, t, d_model = x_prompt.shape
    bd, tn, _ = x_sample.shape
    assert tn == 1, "one new token per sample sequence"
    _, _, n_rh, dk, dv = state_retention.shape
    kv_rank = cache_kv_latent.shape[-1]
    rope_dim = cache_k_rope.shape[-1]
    page = cache_kv_latent.shape[2]
    n_mh = w_ukv.shape[2]
    q_rank = w_uq.shape[1]
    nope = w_uq.shape[2] // n_mh - rope_dim
    dv_m = w_ukv.shape[3] - nope
    assert nope + rope_dim <= HEAD_PAD and dv_m < HEAD_PAD
    dims = (n_rh, dk, dv, q_rank, kv_rank, rope_dim, n_mh, nope, dv_m)
    past_len = page_table.shape[1] * page

    wts, group_size = _layout_weights(norm_mix_g[0], w_in[0], q_norm_g[0], kv_norm_g[0], w_uq[0], w_ukv[0],
                                      w_ret_o[0], w_mla_o[0], w_out[0], norm_ffn_g[0],
                                      w_router_group[0], w_router_expert[0], dims)
    gn = ret_gn_g[0]
    wg, wu, wd = w_gate[0].astype(BF16), w_up[0].astype(BF16), w_down[0].astype(BF16)
    gfin = norm_final_g.reshape(1, -1)

    tm_p = min(256, t)
    xp = x_prompt.reshape(b * t, d_model)
    tabs_p = _rope_tables(jnp.arange(t, dtype=F32), dk, rope_dim, nope)
    (rq, rk, rv, rgs, gret, gmla, lat_p, rope_p, q_p, k_p, v_p) = _inproj(xp, tabs_p, wts, dims, tm_p)
    ret_o, st_p = _ret_prompt(rq.reshape(b, t, -1), rk.reshape(b, t, -1), rv.reshape(b, t, -1),
                              rgs.reshape(b, t, -1), gn, n_rh, dk, dv)
    mla_o = _attn_prompt(q_p, k_p, v_p, b, t, min(ATTN_TQ, t), min(ATTN_HEADS_PER_STEP, n_mh), dv_m)
    x1, h2, comb = _merge(xp, ret_o.reshape(b * t, -1), gret, mla_o, gmla, wts, group_size, tm_p)
    y_prompt = _moe(h2, comb, x1, wg, wu, wd, gfin, min(1024, b * t)).reshape(b, t, d_model)

    xs = x_sample.reshape(bd, d_model)
    tabs_s = _rope_tables(jnp.full((bd,), float(past_len), F32), dk, rope_dim, nope)
    (rq, rk, rv, rgs, gret, gmla, lat_s, rope_s, q_s, _, _) = _inproj(xs, tabs_s, wts, dims, bd)
    ret_o_s, st_s = _ret_sample(rq, rk, rv, rgs, state_retention[0], gn, n_rh, dk, dv)
    q_lat = jnp.transpose(_headmm(q_s, wts["wabs"], F32), (1, 0, 2))
    q_pe = jnp.transpose(q_s[:, :, nope:nope + rope_dim].astype(F32), (1, 0, 2))
    o_lat = _attn_sample(page_table, q_lat, q_pe, lat_s.reshape(bd, 1, kv_rank),
                         rope_s.reshape(bd, 1, rope_dim), cache_kv_latent.reshape(-1, page, kv_rank),
                         jnp.transpose(cache_k_rope.reshape(-1, page, rope_dim), (0, 2, 1)),
                         min(32, page_table.shape[1]))
    mla_o_s = _headmm(jnp.transpose(o_lat, (1, 0, 2)), wts["wvh"], BF16)
    mla_o_s = jnp.transpose(mla_o_s, (1, 0, 2)).reshape(bd, n_mh * HEAD_PAD)
    x1, h2, comb = _merge(xs, ret_o_s, gret, mla_o_s, gmla, wts, group_size, bd)
    y_sample = _moe(h2, comb, x1, wg, wu, wd, gfin, bd).reshape(bd, tn, d_model)

    return (y_prompt, y_sample,
            lat_p.reshape(depth, b, t // page, page, kv_rank),
            rope_p.reshape(depth, b, t // page, page, rope_dim),
            st_p.reshape(depth, b, n_rh, dk, dv),
            lat_s.reshape(depth, bd, tn, kv_rank),
            rope_s.reshape(depth, bd, tn, rope_dim),
            st_s.reshape(depth, bd, n_rh, dk, dv))
```

```python
import functools
import math

import numpy as np
import jax
import jax.numpy as jnp
from jax import lax
from jax.experimental import pallas as pl
from jax.experimental.pallas import tpu as pltpu

F32 = jnp.float32
BF16 = jnp.bfloat16

ROPE_BASE = 10000.0
EPS = 1e-6
RET_CHUNK = 128
LANES = 128
HEAD_PAD = LANES
MOE_ROW_TILE = 256
DECODE_SLOTS = 3
ATTN_TQ = 512
ATTN_HEADS_PER_STEP = 4
VMEM_LIMIT = 48 * 1024 * 1024


def _cparams(n_axes):
    return pltpu.CompilerParams(dimension_semantics=("arbitrary",) * n_axes,
                                vmem_limit_bytes=VMEM_LIMIT)


def _const_spec(shape):
    nd = len(shape)
    return pl.BlockSpec(shape, lambda *_: (0,) * nd, pipeline_mode=pl.Buffered(1))


def _rms(x, g):
    return x * lax.rsqrt(jnp.mean(x * x, axis=-1, keepdims=True) + EPS) * g


def _sigmoid(x):
    return 1.0 / (1.0 + jnp.exp(-x))


def _dot(a, b):
    return jnp.dot(a, b, preferred_element_type=F32)


def _dot_nt(a, b):
    return lax.dot_general(a, b, (((1,), (1,)), ((), ())), preferred_element_type=F32)


def _inproj_kernel(x_ref, gmix_ref, wmain_ref, wsmall_ref, gq_ref, gkv_ref, wq_ref, wqsw_ref,
                   wk_ref, wv_ref, cosr_ref, sinr_ref, cp_ref, sp_ref,
                   rq_o, rk_o, rv_o, rgs_o, gret_o, gmla_o, lat_o, rope_o, q_o, k_o, v_o,
                   *, n_rh, dk, dv, q_rank, kv_rank, rope_dim, n_mh, nope, dv_m, k_scale, q_scale):
    x = x_ref[...]
    h = _rms(x, gmix_ref[...]).astype(BF16)
    cosr = cosr_ref[...]
    sinr = sinr_ref[...]
    qk_w = n_rh * dk
    v_w = n_rh * dv

    def rope_heads(z, scale):
        outs = []
        for i in range(n_rh):
            seg = z[:, i * dk:(i + 1) * dk]
            rot = seg * cosr + pltpu.roll(seg, dk // 2, 1) * sinr
            outs.append(rot if scale is None else rot * scale)
        return jnp.concatenate(outs, axis=1)

    off = 0
    rq_o[...] = rope_heads(_dot(h, wmain_ref[:, off:off + qk_w]), None).astype(BF16)
    off += qk_w
    rk_o[...] = rope_heads(_dot(h, wmain_ref[:, off:off + qk_w]), k_scale).astype(BF16)
    off += qk_w
    rv_o[...] = _dot(h, wmain_ref[:, off:off + v_w]).astype(BF16)
    off += v_w
    rg = _dot(h, wmain_ref[:, off:off + v_w])
    rgs_o[...] = (rg * _sigmoid(rg)).astype(BF16)
    off += v_w
    d_model = x.shape[1]
    gret_o[...] = _sigmoid(_dot(h, wmain_ref[:, off:off + d_model])).astype(BF16)
    off += d_model
    gmla_o[...] = _sigmoid(_dot(h, wmain_ref[:, off:off + d_model])).astype(BF16)

    zs = _dot(h, wsmall_ref[...])
    cq = zs[:, :q_rank]
    ckv = zs[:, q_rank:q_rank + kv_rank]
    krp = zs[:, q_rank + kv_rank:q_rank + kv_rank + HEAD_PAD]
    krsw = zs[:, q_rank + kv_rank + HEAD_PAD:]
    cp = cp_ref[...]
    sp = sp_ref[...]
    cqn = _rms(cq, gq_ref[...]).astype(BF16)
    ckvn = _rms(ckv, gkv_ref[...])
    lat_o[...] = ckvn
    ckvb = ckvn.astype(BF16)
    kr_rot = krp * cp + krsw * sp
    rope_o[...] = pltpu.roll(kr_rot, HEAD_PAD - nope, 1)[:, :rope_dim]

    qh = _dot(cqn, wq_ref[...])
    qs = _dot(cqn, wqsw_ref[...])
    kh = _dot(ckvb, wk_ref[...])
    vh = _dot(ckvb, wv_ref[...])
    sum_lane = lax.broadcasted_iota(jnp.int32, cp.shape, 1) == dv_m
    for i in range(n_mh):
        sl = slice(i * HEAD_PAD, (i + 1) * HEAD_PAD)
        q_o[i] = ((qh[:, sl] * cp + qs[:, sl] * sp) * q_scale).astype(BF16)
        k_o[i] = (kh[:, sl] + kr_rot).astype(BF16)
        v_o[i] = jnp.where(sum_lane, 1.0, vh[:, sl]).astype(BF16)


def _inproj(x, tables, wts, dims, tm):
    n, d_model = x.shape
    cosr, sinr, cp, sp = tables
    tab_blocks = cosr.shape[0] // tm
    n_rh, dk, dv, q_rank, kv_rank, rope_dim, n_mh, nope, dv_m = dims
    grid = (n // tm,)
    tok = lambda w: pl.BlockSpec((tm, w), lambda i: (i, 0))
    tab = lambda w: pl.BlockSpec((tm, w), lambda i: (i % tab_blocks, 0))
    head = pl.BlockSpec((n_mh, tm, HEAD_PAD), lambda i: (0, i, 0))
    w_names = ("gmix", "wmain", "wsmall", "gq", "gkv", "wq", "wqsw", "wk", "wv")
    w_args = [wts[k] for k in w_names]
    in_specs = ([tok(d_model)] + [_const_spec(a.shape) for a in w_args]
                + [tab(dk), tab(dk), tab(HEAD_PAD), tab(HEAD_PAD)])
    qk_w, v_w = n_rh * dk, n_rh * dv
    out_shape = (
        jax.ShapeDtypeStruct((n, qk_w), BF16), jax.ShapeDtypeStruct((n, qk_w), BF16),
        jax.ShapeDtypeStruct((n, v_w), BF16), jax.ShapeDtypeStruct((n, v_w), BF16),
        jax.ShapeDtypeStruct((n, d_model), BF16), jax.ShapeDtypeStruct((n, d_model), BF16),
        jax.ShapeDtypeStruct((n, kv_rank), F32), jax.ShapeDtypeStruct((n, rope_dim), F32),
        jax.ShapeDtypeStruct((n_mh, n, HEAD_PAD), BF16), jax.ShapeDtypeStruct((n_mh, n, HEAD_PAD), BF16),
        jax.ShapeDtypeStruct((n_mh, n, HEAD_PAD), BF16),
    )
    out_specs = (tok(qk_w), tok(qk_w), tok(v_w), tok(v_w), tok(d_model), tok(d_model),
                 tok(kv_rank), tok(rope_dim), head, head, head)
    kern = functools.partial(_inproj_kernel, n_rh=n_rh, dk=dk, dv=dv, q_rank=q_rank, kv_rank=kv_rank,
                             rope_dim=rope_dim, n_mh=n_mh, nope=nope, dv_m=dv_m, k_scale=dk ** -0.5,
                             q_scale=(nope + rope_dim) ** -0.5 * math.log2(math.e))
    return pl.pallas_call(kern, out_shape=out_shape, grid=grid, in_specs=in_specs, out_specs=out_specs,
                          compiler_params=_cparams(1), name="inproj")(x, *w_args, cosr, sinr, cp, sp)


def _ret_prompt_kernel(q_ref, k_ref, v_ref, gate_ref, mask_ref, qd_ref, kd_ref, gn_ref,
                       o_ref, st_ref, s_scr, *, n_rh, dk, dv, chunk_decay):
    c = pl.program_id(1)

    @pl.when(c == 0)
    def _():
        s_scr[...] = jnp.zeros_like(s_scr)

    for i in range(n_rh):
        q = q_ref[0, :, i * dk:(i + 1) * dk]
        k = k_ref[0, :, i * dk:(i + 1) * dk]
        v = v_ref[0, :, i * dv:(i + 1) * dv]
        s = _dot_nt(q, k) * mask_ref[i]
        state = s_scr[i]
        o = _dot(s.astype(BF16), v) + _dot(q, state.astype(BF16)) * qd_ref[i]
        kd = (k.astype(F32) * kd_ref[i]).T.astype(BF16)
        s_scr[i] = state * chunk_decay[i] + _dot(kd, v)
        o = _rms(o, gn_ref[i]) * gate_ref[0, :, i * dv:(i + 1) * dv].astype(F32)
        o_ref[0, :, i * dv:(i + 1) * dv] = o.astype(BF16)

    @pl.when(c == pl.num_programs(1) - 1)
    def _():
        st_ref[0] = s_scr[...]


def _ret_tables(n_rh, chunk, dk, dv):
    log_g = np.log1p(-np.exp2(-5.0 - np.arange(n_rh, dtype=np.float64)))
    idx = np.arange(chunk, dtype=np.float64)
    diff = idx[:, None] - idx[None, :]
    mask = np.where(diff >= 0, np.exp(log_g[:, None, None] * np.maximum(diff, 0.0)), 0.0)
    qd = np.exp(log_g[:, None] * (idx + 1.0))
    kd = np.exp(log_g[:, None] * (chunk - 1.0 - idx))
    cd = np.exp(log_g * chunk)
    qd = np.broadcast_to(qd[:, :, None], (n_rh, chunk, dv))
    kd = np.broadcast_to(kd[:, :, None], (n_rh, chunk, dk))
    return (jnp.asarray(mask, F32), jnp.asarray(qd, F32), jnp.asarray(kd, F32),
            tuple(float(np.float32(v)) for v in cd), tuple(float(np.float32(v)) for v in np.exp(log_g)))


def _ret_prompt(rq, rk, rv, rgs, gn, n_rh, dk, dv):
    b, t, _ = rq.shape
    chunk = RET_CHUNK
    mask, qd, kd, cd, _ = _ret_tables(n_rh, chunk, dk, dv)
    tokb = lambda w: pl.BlockSpec((1, chunk, w), lambda bi, c: (bi, c, 0))
    in_specs = [tokb(n_rh * dk), tokb(n_rh * dk), tokb(n_rh * dv), tokb(n_rh * dv),
                _const_spec(mask.shape), _const_spec(qd.shape), _const_spec(kd.shape),
                _const_spec((n_rh, 1, dv))]
    out_shape = (jax.ShapeDtypeStruct((b, t, n_rh * dv), BF16),
                 jax.ShapeDtypeStruct((b, n_rh, dk, dv), F32))
    out_specs = (tokb(n_rh * dv), pl.BlockSpec((1, n_rh, dk, dv), lambda bi, c: (bi, 0, 0, 0)))
    kern = functools.partial(_ret_prompt_kernel, n_rh=n_rh, dk=dk, dv=dv, chunk_decay=cd)
    return pl.pallas_call(kern, out_shape=out_shape, grid=(b, t // chunk), in_specs=in_specs,
                          out_specs=out_specs, scratch_shapes=[pltpu.VMEM((n_rh, dk, dv), F32)],
                          compiler_params=_cparams(2), name="retention_prompt")(
        rq, rk, rv, rgs, mask, qd, kd, gn.reshape(n_rh, 1, dv))


def _ret_sample_kernel(qt_ref, kt_ref, v_ref, gate_ref, st_ref, gn_ref, o_ref, ns_ref,
                       *, n_rh, dk, dv, gamma):
    b = pl.program_id(0)
    lane = lax.broadcasted_iota(jnp.int32, qt_ref.shape, 1)
    sel = lane == b
    qcol = jnp.sum(jnp.where(sel, qt_ref[...], 0.0), axis=1, keepdims=True)
    kcol = jnp.sum(jnp.where(sel, kt_ref[...], 0.0), axis=1, keepdims=True)
    v = v_ref[0]
    gate = gate_ref[0]
    for i in range(n_rh):
        vi = v[:, i * dv:(i + 1) * dv]
        new = st_ref[0, i] * gamma[i] + kcol[i * dk:(i + 1) * dk] * vi
        ns_ref[0, i] = new
        o = jnp.sum(qcol[i * dk:(i + 1) * dk] * new, axis=0, keepdims=True)
        o_ref[0, :, i * dv:(i + 1) * dv] = _rms(o, gn_ref[i]) * gate[:, i * dv:(i + 1) * dv]


def _ret_sample(rq, rk, rv, rgs, state, gn, n_rh, dk, dv):
    nb = rq.shape[0]
    _, _, _, _, gamma = _ret_tables(n_rh, 1, dk, dv)
    qt = rq.astype(F32).T
    kt = rk.astype(F32).T
    row = lambda w: pl.BlockSpec((1, 1, w), lambda b: (b, 0, 0))
    st_spec = pl.BlockSpec((1, n_rh, dk, dv), lambda b: (b, 0, 0, 0))
    in_specs = [_const_spec(qt.shape), _const_spec(kt.shape), row(n_rh * dv), row(n_rh * dv), st_spec,
                _const_spec((n_rh, 1, dv))]
    out_shape = (jax.ShapeDtypeStruct((nb, 1, n_rh * dv), F32),
                 jax.ShapeDtypeStruct((nb, n_rh, dk, dv), F32))
    kern = functools.partial(_ret_sample_kernel, n_rh=n_rh, dk=dk, dv=dv, gamma=gamma)
    o, ns = pl.pallas_call(kern, out_shape=out_shape, grid=(nb,), in_specs=in_specs,
                           out_specs=(row(n_rh * dv), st_spec), compiler_params=_cparams(1),
                           name="retention_sample")(
        qt, kt, rv.astype(F32).reshape(nb, 1, -1), rgs.astype(F32).reshape(nb, 1, -1), state,
        gn.reshape(n_rh, 1, dv))
    return o.reshape(nb, n_rh * dv).astype(BF16), ns


def _attn_prompt_kernel(q_ref, k_ref, v_ref, o_ref, m_scr, acc_scr, *, tq, hpg, sum_lane):
    i = pl.program_id(2)
    m_scr[...] = jnp.full(m_scr.shape, -jnp.inf, F32)
    acc_scr[...] = jnp.zeros(acc_scr.shape, F32)

    def tile(j, masked):
        start = pl.multiple_of(j * tq, tq)
        for h in range(hpg):
            s = _dot_nt(q_ref[h], k_ref[h, pl.ds(start, tq), :])
            if masked:
                row = lax.broadcasted_iota(jnp.int32, s.shape, 0)
                col = lax.broadcasted_iota(jnp.int32, s.shape, 1)
                s = jnp.where(col <= row, s, -jnp.inf)
            m = m_scr[h]
            m_new = jnp.maximum(m, jnp.max(s, axis=1, keepdims=True))
            m_wide = jnp.concatenate([m_new] * (tq // HEAD_PAD), axis=1)
            p = jnp.exp2((s - m_wide).astype(BF16))
            acc_scr[h] = jnp.exp2(m - m_new) * acc_scr[h] + _dot(p, v_ref[h, pl.ds(start, tq), :])
            m_scr[h] = m_new

    def body(j, carry):
        tile(j, False)
        return carry

    lax.fori_loop(0, i, body, 0)
    tile(i, True)
    for h in range(hpg):
        acc = acc_scr[h]
        o_ref[:, h * HEAD_PAD:(h + 1) * HEAD_PAD] = (acc / acc[:, sum_lane:sum_lane + 1]).astype(BF16)


def _attn_prompt(q, k, v, b, t, tq, hpg, sum_lane):
    n_mh, n, _ = q.shape
    nq = t // tq
    kv_spec = pl.BlockSpec((hpg, t, HEAD_PAD), lambda bi, g, i: (g, bi, 0), pipeline_mode=pl.Buffered(1))
    in_specs = [pl.BlockSpec((hpg, tq, HEAD_PAD), lambda bi, g, i: (g, bi * nq + i, 0)), kv_spec, kv_spec]
    out_specs = pl.BlockSpec((tq, hpg * HEAD_PAD), lambda bi, g, i: (bi * nq + i, g))
    return pl.pallas_call(functools.partial(_attn_prompt_kernel, tq=tq, hpg=hpg, sum_lane=sum_lane),
                          out_shape=jax.ShapeDtypeStruct((n, n_mh * HEAD_PAD), BF16),
                          grid=(b, n_mh // hpg, nq), in_specs=in_specs, out_specs=out_specs,
                          scratch_shapes=[pltpu.VMEM((hpg, tq, HEAD_PAD), F32),
                                          pltpu.VMEM((hpg, tq, HEAD_PAD), F32)],
                          compiler_params=_cparams(3), name="attn_prompt")(q, k, v)


def _attn_sample_kernel(pt_ref, ql_ref, qp_ref, cn_ref, rn_ref, lat_hbm, rope_hbm, o_ref,
                        lat_slab, rope_slab, sems, m_scr, l_scr, acc_scr, *, npg, n_chunks, page):
    step = pl.program_id(0)
    n_steps = pl.num_programs(0)
    n_slots = lat_slab.shape[0]
    ahead = n_slots - 1
    slot = step % n_slots
    chunk = step % n_chunks

    def page_copies(page_ids, slot_):
        out = []
        for i in range(npg):
            rows = pl.ds(i * page, page)
            out.append(pltpu.make_async_copy(lat_hbm.at[page_ids(i)], lat_slab.at[slot_, rows, :],
                                             sems.at[0, slot_]))
            out.append(pltpu.make_async_copy(rope_hbm.at[page_ids(i)], rope_slab.at[slot_, :, rows],
                                             sems.at[1, slot_]))
        return out

    def start_step(step_, slot_):
        seq = step_ // n_chunks
        first = (step_ % n_chunks) * npg
        for cp in page_copies(lambda i: pt_ref[seq, first + i], slot_):
            cp.start()

    for k in range(ahead):
        @pl.when(jnp.logical_and(step == 0, k < n_steps))
        def _():
            start_step(k, k)

    @pl.when(step + ahead < n_steps)
    def _():
        start_step(step + ahead, (step + ahead) % n_slots)

    for cp in page_copies(lambda i: 0, slot):
        cp.wait()

    ql = ql_ref[0]
    qp = qp_ref[0]

    @pl.when(chunk == 0)
    def _():
        cn = cn_ref[0]
        s0 = (jnp.sum(ql * cn, axis=1, keepdims=True)
              + jnp.sum(qp * rn_ref[0], axis=1, keepdims=True))
        m_scr[...] = s0
        l_scr[...] = jnp.ones_like(l_scr)
        acc_scr[...] = jnp.broadcast_to(cn, acc_scr.shape)

    lat = lat_slab[slot].astype(BF16)
    s = _dot_nt(ql.astype(BF16), lat) + _dot(qp, rope_slab[slot])
    m_prev = m_scr[...]
    m_new = jnp.maximum(m_prev, jnp.max(s, axis=1, keepdims=True))
    alpha = jnp.exp2(m_prev - m_new)
    p = jnp.exp2(s - m_new)
    m_scr[...] = m_new
    l_scr[...] = alpha * l_scr[...] + jnp.sum(p, axis=1, keepdims=True)
    acc_scr[...] = alpha * acc_scr[...] + _dot(p.astype(BF16), lat)

    @pl.when(chunk == n_chunks - 1)
    def _():
        o_ref[0] = acc_scr[...] / l_scr[...]


def _attn_sample(page_table, q_lat, q_pe, c_new, r_new, cache_lat, cache_rope_t, npg):
    nb, n_mh, kv_rank = q_lat.shape
    rope_dim = q_pe.shape[-1]
    n_pages = page_table.shape[1]
    page = cache_lat.shape[1]
    n_chunks = n_pages // npg
    per_seq = lambda shp: pl.BlockSpec((1,) + shp, lambda s, pt: (s // n_chunks, 0, 0))
    hbm = pl.BlockSpec(memory_space=pl.ANY)
    in_specs = [per_seq((n_mh, kv_rank)), per_seq((n_mh, rope_dim)), per_seq((1, kv_rank)),
                per_seq((1, rope_dim)), hbm, hbm]
    grid_spec = pltpu.PrefetchScalarGridSpec(
        num_scalar_prefetch=1, grid=(nb * n_chunks,), in_specs=in_specs,
        out_specs=per_seq((n_mh, kv_rank)),
        scratch_shapes=[pltpu.VMEM((DECODE_SLOTS, npg * page, kv_rank), F32),
                        pltpu.VMEM((DECODE_SLOTS, rope_dim, npg * page), F32),
                        pltpu.SemaphoreType.DMA((2, DECODE_SLOTS)),
                        pltpu.VMEM((n_mh, 1), F32), pltpu.VMEM((n_mh, 1), F32),
                        pltpu.VMEM((n_mh, kv_rank), F32)])
    kern = functools.partial(_attn_sample_kernel, npg=npg, n_chunks=n_chunks, page=page)
    return pl.pallas_call(kern, out_shape=jax.ShapeDtypeStruct((nb, n_mh, kv_rank), F32),
                          grid_spec=grid_spec, compiler_params=_cparams(1), name="attn_sample")(
        page_table, q_lat, q_pe, c_new, r_new, cache_lat, cache_rope_t)


def _headmm_kernel(a_ref, w_ref, o_ref):
    o_ref[0] = _dot(a_ref[0].astype(BF16), w_ref[0]).astype(o_ref.dtype)


def _headmm(a, w, out_dtype):
    nh, m, kk = a.shape
    nn = w.shape[-1]
    spec = lambda r, c: pl.BlockSpec((1, r, c), lambda h: (h, 0, 0))
    return pl.pallas_call(_headmm_kernel, out_shape=jax.ShapeDtypeStruct((nh, m, nn), out_dtype),
                          grid=(nh,), in_specs=[spec(m, kk), spec(kk, nn)], out_specs=spec(m, nn),
                          compiler_params=_cparams(1), name="head_matmul")(a, w)


def _router_logits(h2, whi_ref, wlo_ref):
    hi = h2.astype(BF16)
    lo = (h2 - hi.astype(F32)).astype(BF16)
    lg = _dot(hi, whi_ref[...]) + (_dot(hi, wlo_ref[...]) + _dot(lo, whi_ref[...]))
    return hi, lg[:, :LANES], lg[:, LANES:]


def _merge_kernel(x_ref, ro_ref, gr_ref, mo_ref, gm_ref, wro_ref, wmo_ref, wout_ref, gffn_ref,
                  whi_ref, wlo_ref, x1_o, gid_o, *, group_size, n_exp):
    mixed = (_dot(ro_ref[...], wro_ref[...]) * gr_ref[...].astype(F32)
             + _dot(mo_ref[...], wmo_ref[...]) * gm_ref[...].astype(F32))
    x1 = x_ref[...] + _dot(mixed.astype(BF16), wout_ref[...])
    x1_o[...] = x1
    _, ge, _ = _router_logits(_rms(x1, gffn_ref[...]), whi_ref, wlo_ref)
    lane = lax.broadcasted_iota(jnp.int32, ge.shape, 1)
    ge = jnp.where(lane < n_exp, ge, -jnp.inf)
    grp = (lane // group_size).astype(F32)
    gmax = jnp.max(ge, axis=1, keepdims=True)
    g_idx = jnp.min(jnp.where(ge == gmax, grp, float(n_exp)), axis=1, keepdims=True)
    gid_o[...] = jnp.broadcast_to(g_idx, ge.shape).astype(jnp.int32)


def _merge(x, ro, gr, mo, gm, wts, group_size, n_exp, tm):
    n, d_model = x.shape
    tok = lambda w: pl.BlockSpec((tm, w), lambda i: (i, 0))
    w_args = [wts[k] for k in ("wro", "wmo", "wout", "gffn", "wr_hi", "wr_lo")]
    in_specs = [tok(d_model), tok(ro.shape[1]), tok(d_model), tok(mo.shape[1]), tok(d_model)] + [
        _const_spec(a.shape) for a in w_args]
    out_shape = (jax.ShapeDtypeStruct((n, d_model), F32), jax.ShapeDtypeStruct((n, LANES), jnp.int32))
    kern = functools.partial(_merge_kernel, group_size=group_size, n_exp=n_exp)
    return pl.pallas_call(kern, out_shape=out_shape, grid=(n // tm,), in_specs=in_specs,
                          out_specs=(tok(d_model), tok(LANES)),
                          compiler_params=_cparams(1), name="merge_router")(x, ro, gr, mo, gm, *w_args)


def _group_tiles(gid, n_grp, tmr):
    n = gid.shape[0]
    onehot = (gid[:, None] == jnp.arange(n_grp, dtype=jnp.int32)[None, :]).astype(jnp.int32)
    incl = jnp.cumsum(onehot, axis=0)
    rank = jnp.sum((incl - onehot) * onehot, axis=1)
    counts = incl[-1]
    padded = ((counts + tmr - 1) // tmr) * tmr
    ends = jnp.cumsum(padded)
    offs = ends - padded
    pos = jnp.sum(onehot * offs[None, :], axis=1) + rank
    n_tiles = n // tmr + n_grp
    row_token = jnp.zeros((n_tiles * tmr,), jnp.int32).at[pos].set(jnp.arange(n, dtype=jnp.int32))
    tile_start = jnp.arange(n_tiles, dtype=jnp.int32) * tmr
    tile_group = jnp.minimum(jnp.sum((tile_start[:, None] >= ends[None, :]).astype(jnp.int32), axis=1),
                             n_grp - 1)
    tile_valid = jnp.clip(counts[tile_group] - (tile_start - offs[tile_group]), 0, tmr)
    return row_token, tile_group, tile_valid.astype(jnp.int32)


def _moe_kernel(rt_ref, tg_ref, tv_ref, x1_hbm, gffn_ref, gfin_ref, whi_ref, wlo_ref, wg_ref, wu_ref, wd_ref,
                y_hbm, xbuf, obuf, trash, gsem, ssem, *, tmr, group_size, n_exp, d_exp):
    t = pl.program_id(0)
    last = pl.num_programs(0) - 1
    slot = t % 2

    def gather_copy(tok, r, slot_):
        return pltpu.make_async_copy(x1_hbm.at[pl.ds(tok, 1)], xbuf.at[slot_, pl.ds(r, 1)], gsem.at[slot_])

    def start_gather(tile, slot_):
        for r in range(tmr):
            gather_copy(rt_ref[tile * tmr + r], r, slot_).start()

    def wait_gather(slot_):
        for r in range(tmr):
            gather_copy(0, r, slot_).wait()

    def scatter_copy(r, dst_row_ref, slot_):
        return pltpu.make_async_copy(obuf.at[slot_, pl.ds(r, 1)], dst_row_ref, ssem.at[slot_])

    def wait_scatter(slot_):
        for r in range(tmr):
            scatter_copy(r, trash.at[slot_, pl.ds(r, 1)], slot_).wait()

    @pl.when(t == 0)
    def _():
        start_gather(t, slot)

    wait_gather(slot)

    @pl.when(t >= 2)
    def _():
        wait_scatter(slot)

    start_gather(jnp.minimum(t + 1, last), 1 - slot)

    g = tg_ref[t]
    x = xbuf[slot]
    hb, ge, el = _router_logits(_rms(x, gffn_ref[...]), whi_ref, wlo_ref)
    lane_i = lax.broadcasted_iota(jnp.int32, ge.shape, 1)
    lane = lane_i.astype(F32)
    is_exp = lane_i < n_exp
    in_grp = (lane_i // group_size) == g
    gmax = jnp.max(jnp.where(is_exp, ge, -jnp.inf), axis=1, keepdims=True)
    gsum = jnp.sum(jnp.where(is_exp, jnp.exp(ge - gmax), 0.0), axis=1, keepdims=True) / group_size
    ge_g = jnp.max(jnp.where(in_grp, ge, -jnp.inf), axis=1, keepdims=True)
    g_w = jnp.exp(ge_g - gmax) / gsum
    e_in = jnp.where(in_grp, el, -jnp.inf)
    top1 = jnp.max(e_in, axis=1, keepdims=True)
    idx1 = jnp.min(jnp.where(e_in == top1, lane, float(LANES)), axis=1, keepdims=True)
    e_rest = jnp.where(lane == idx1, -jnp.inf, e_in)
    top2 = jnp.max(e_rest, axis=1, keepdims=True)
    idx2 = jnp.min(jnp.where(e_rest == top2, lane, float(LANES)), axis=1, keepdims=True)
    e2 = jnp.exp(top2 - top1)
    comb = (jnp.where(lane == idx1, g_w / (1.0 + e2), 0.0)
            + jnp.where(lane == idx2, g_w * e2 / (1.0 + e2), 0.0))

    hg = _dot(hb, wg_ref[0])
    hu = _dot(hb, wu_ref[0])
    first = (g * group_size).astype(F32)
    parts = []
    for e in range(group_size):
        ce = jnp.sum(jnp.where(lane == first + e, comb, 0.0), axis=1, keepdims=True)
        sl = slice(e * d_exp, (e + 1) * d_exp)
        parts.append((hg[:, sl] * _sigmoid(hg[:, sl]) * hu[:, sl] * ce).astype(BF16))
    y = x + _dot(jnp.concatenate(parts, axis=1), wd_ref[0])
    obuf[slot] = _rms(y, gfin_ref[...])

    n_real = tv_ref[t]

    def scatter_real(r, carry):
        tok = rt_ref[t * tmr + r]
        scatter_copy(r, y_hbm.at[pl.ds(tok, 1)], slot).start()
        return carry

    def scatter_pad(r, carry):
        scatter_copy(r, trash.at[slot, pl.ds(r, 1)], slot).start()
        return carry

    @pl.when(n_real == tmr)
    def _():
        for r in range(tmr):
            scatter_real(r, 0)

    @pl.when(n_real < tmr)
    def _():
        lax.fori_loop(0, n_real, scatter_real, 0)
        lax.fori_loop(n_real, tmr, scatter_pad, 0)

    @pl.when(t == last)
    def _():
        wait_gather(1 - slot)
        wait_scatter(slot)

        @pl.when(t >= 1)
        def _():
            wait_scatter(1 - slot)


def _moe(x1, gid, wts, wg, wu, wd, gfin, group_size, n_exp, tmr):
    n, d_model = x1.shape
    n_grp = wg.shape[0]
    d_exp = wg.shape[2] // group_size
    row_token, tile_group, tile_valid = _group_tiles(gid, n_grp, tmr)
    n_tiles = tile_group.shape[0]
    hbm = pl.BlockSpec(memory_space=pl.ANY)
    const = lambda a: pl.BlockSpec(a.shape, lambda t, rt, tg, tv: (0,) * a.ndim, pipeline_mode=pl.Buffered(1))
    grp_w = lambda a: pl.BlockSpec((1,) + a.shape[1:], lambda t, rt, tg, tv: (tg[t], 0, 0))
    w_args = [wts["gffn"], gfin, wts["wr_hi"], wts["wr_lo"]]
    grid_spec = pltpu.PrefetchScalarGridSpec(
        num_scalar_prefetch=3, grid=(n_tiles,),
        in_specs=[hbm] + [const(a) for a in w_args] + [grp_w(wg), grp_w(wu), grp_w(wd)],
        out_specs=hbm,
        scratch_shapes=[pltpu.VMEM((2, tmr, d_model), F32), pltpu.VMEM((2, tmr, d_model), F32),
                        pltpu.VMEM((2, tmr, d_model), F32),
                        pltpu.SemaphoreType.DMA((2,)), pltpu.SemaphoreType.DMA((2,))])
    kern = functools.partial(_moe_kernel, tmr=tmr, group_size=group_size, n_exp=n_exp, d_exp=d_exp)
    return pl.pallas_call(kern, out_shape=jax.ShapeDtypeStruct((n, d_model), F32), grid_spec=grid_spec,
                          compiler_params=_cparams(1), name="moe")(
        row_token, tile_group, tile_valid, x1, *w_args, wg, wu, wd)


def _rope_tables(pos, dk, rope_dim, nope):
    def cs(half):
        inv = ROPE_BASE ** (-jnp.arange(half, dtype=F32) / half)
        ang = pos[:, None] * inv[None, :]
        return jnp.cos(ang), jnp.sin(ang)

    c, s = cs(dk // 2)
    cosr = jnp.concatenate([c, c], axis=1)
    sinr = jnp.concatenate([-s, s], axis=1)
    c, s = cs(rope_dim // 2)
    n = pos.shape[0]
    tail = jnp.zeros((n, HEAD_PAD - nope - rope_dim), F32)
    cp = jnp.concatenate([jnp.ones((n, nope), F32), c, c, tail], axis=1)
    sp = jnp.concatenate([jnp.zeros((n, nope), F32), -s, s, tail], axis=1)
    return cosr, sinr, cp, sp


def _swap_halves(w):
    half = w.shape[-1] // 2
    return jnp.concatenate([w[..., half:], w[..., :half]], axis=-1)


def _layout_weights(norm_mix_g, w_in, q_norm_g, kv_norm_g, w_uq, w_ukv, w_ret_o, w_mla_o, w_out,
                    norm_ffn_g, w_router_group, w_router_expert, dims):
    n_rh, dk, dv, q_rank, kv_rank, rope_dim, n_mh, nope, dv_m = dims
    d_model = w_in.shape[0]
    qk_w, v_w = n_rh * dk, n_rh * dv
    o_cq = 2 * qk_w + 2 * v_w
    o_ckv = o_cq + q_rank
    o_kr = o_ckv + kv_rank
    o_gate = o_kr + rope_dim
    pad_tail = HEAD_PAD - nope - rope_dim
    zeros = lambda *shape: jnp.zeros(shape, F32)
    w_kr = w_in[:, o_kr:o_gate]
    place = lambda w: jnp.concatenate([zeros(d_model, nope), w, zeros(d_model, pad_tail)], axis=1)
    wsmall = jnp.concatenate([w_in[:, o_cq:o_kr], place(w_kr), place(_swap_halves(w_kr))], axis=1)
    wmain = jnp.concatenate([w_in[:, :o_cq], w_in[:, o_gate:]], axis=1)

    uq = w_uq.reshape(q_rank, n_mh, nope + rope_dim)
    uq_rope = uq[..., nope:]
    wq = jnp.concatenate([uq, zeros(q_rank, n_mh, pad_tail)], axis=-1)
    wqsw = jnp.concatenate([zeros(q_rank, n_mh, nope), _swap_halves(uq_rope),
                            zeros(q_rank, n_mh, pad_tail)], axis=-1)
    uk = w_ukv[..., :nope]
    uv = w_ukv[..., nope:]
    wk = jnp.concatenate([uk, zeros(kv_rank, n_mh, HEAD_PAD - nope)], axis=-1)
    wv = jnp.concatenate([uv, zeros(kv_rank, n_mh, HEAD_PAD - dv_m)], axis=-1)
    flat = lambda w: w.reshape(w.shape[0], n_mh * HEAD_PAD).astype(BF16)
    wmo = jnp.concatenate([w_mla_o.reshape(n_mh, dv_m, d_model),
                           zeros(n_mh, HEAD_PAD - dv_m, d_model)], axis=1).reshape(n_mh * HEAD_PAD, d_model)
    wabs = jnp.concatenate([jnp.transpose(uk, (1, 2, 0)), zeros(n_mh, HEAD_PAD - nope, kv_rank)], axis=1)
    wvh = jnp.transpose(wv, (1, 0, 2))
    n_grp = w_router_group.shape[1]
    n_exp = w_router_expert.shape[1]
    group_size = n_exp // n_grp
    assert n_exp <= LANES
    w_r = jnp.concatenate([jnp.repeat(w_router_group, group_size, axis=1), zeros(d_model, LANES - n_exp),
                           w_router_expert, zeros(d_model, LANES - n_exp)], axis=1)
    wr_hi = w_r.astype(BF16)
    wr_lo = (w_r - wr_hi.astype(F32)).astype(BF16)
    return dict(
        gmix=norm_mix_g.reshape(1, -1), wmain=wmain.astype(BF16), wsmall=wsmall.astype(BF16),
        gq=q_norm_g.reshape(1, -1), gkv=kv_norm_g.reshape(1, -1),
        wq=flat(wq), wqsw=flat(wqsw), wk=flat(wk), wv=flat(wv),
        wro=w_ret_o.astype(BF16), wmo=wmo.astype(BF16), wout=w_out.astype(BF16),
        gffn=norm_ffn_g.reshape(1, -1), wr_hi=wr_hi, wr_lo=wr_lo,
        wabs=wabs.astype(BF16), wvh=wvh.astype(BF16)), group_size, n_exp


def kernel(x_prompt, x_sample, cache_kv_latent, cache_k_rope, state_retention, page_table, norm_mix_g,
           w_in, q_norm_g, kv_norm_g, w_uq, w_ukv, ret_gn_g, w_ret_o, w_mla_o, w_out, norm_ffn_g,
           w_router_group, w_router_expert, w_gate, w_up, w_down, norm_final_g):
    depth = w_in.shape[0]
    assert depth == 1, "single-layer step only"
    b, t, d_model = x_prompt.shape
    bd, tn, _ = x_sample.shape
    assert tn == 1, "one new token per sample sequence"
    _, _, n_rh, dk, dv = state_retention.shape
    kv_rank = cache_kv_latent.shape[-1]
    rope_dim = cache_k_rope.shape[-1]
    page = cache_kv_latent.shape[2]
    n_mh = w_ukv.shape[2]
    q_rank = w_uq.shape[1]
    nope = w_uq.shape[2] // n_mh - rope_dim
    dv_m = w_ukv.shape[3] - nope
    assert nope + rope_dim <= HEAD_PAD and dv_m < HEAD_PAD
    dims = (n_rh, dk, dv, q_rank, kv_rank, rope_dim, n_mh, nope, dv_m)
    past_len = page_table.shape[1] * page

    wts, group_size, n_exp = _layout_weights(norm_mix_g[0], w_in[0], q_norm_g[0], kv_norm_g[0], w_uq[0],
                                             w_ukv[0], w_ret_o[0], w_mla_o[0], w_out[0], norm_ffn_g[0],
                                             w_router_group[0], w_router_expert[0], dims)
    gn = ret_gn_g[0]
    n_grp = n_exp // group_size
    d_exp = w_gate.shape[-1]
    wide = lambda w: jnp.transpose(w.astype(BF16).reshape(n_grp, group_size, d_model, d_exp),
                                   (0, 2, 1, 3)).reshape(n_grp, d_model, group_size * d_exp)
    wg, wu = wide(w_gate[0]), wide(w_up[0])
    wd = w_down[0].astype(BF16).reshape(n_grp, group_size * d_exp, d_model)
    gfin = norm_final_g.reshape(1, -1)

    tm_p = min(256, t)
    xp = x_prompt.reshape(b * t, d_model)
    tabs_p = _rope_tables(jnp.arange(t, dtype=F32), dk, rope_dim, nope)
    (rq, rk, rv, rgs, gret, gmla, lat_p, rope_p, q_p, k_p, v_p) = _inproj(xp, tabs_p, wts, dims, tm_p)
    ret_o, st_p = _ret_prompt(rq.reshape(b, t, -1), rk.reshape(b, t, -1), rv.reshape(b, t, -1),
                              rgs.reshape(b, t, -1), gn, n_rh, dk, dv)
    mla_o = _attn_prompt(q_p, k_p, v_p, b, t, min(ATTN_TQ, t), min(ATTN_HEADS_PER_STEP, n_mh), dv_m)
    x1, gid = _merge(xp, ret_o.reshape(b * t, -1), gret, mla_o, gmla, wts, group_size, n_exp, tm_p)
    y_prompt = _moe(x1, gid[:, 0], wts, wg, wu, wd, gfin, group_size, n_exp,
                    min(MOE_ROW_TILE, b * t)).reshape(b, t, d_model)

    xs = x_sample.reshape(bd, d_model)
    tabs_s = _rope_tables(jnp.full((bd,), float(past_len), F32), dk, rope_dim, nope)
    (rq, rk, rv, rgs, gret, gmla, lat_s, rope_s, q_s, _, _) = _inproj(xs, tabs_s, wts, dims, bd)
    ret_o_s, st_s = _ret_sample(rq, rk, rv, rgs, state_retention[0], gn, n_rh, dk, dv)
    q_lat = jnp.transpose(_headmm(q_s, wts["wabs"], F32), (1, 0, 2))
    q_pe = jnp.transpose(q_s[:, :, nope:nope + rope_dim].astype(F32), (1, 0, 2))
    o_lat = _attn_sample(page_table, q_lat, q_pe, lat_s.reshape(bd, 1, kv_rank),
                         rope_s.reshape(bd, 1, rope_dim), cache_kv_latent.reshape(-1, page, kv_rank),
                         jnp.transpose(cache_k_rope.reshape(-1, page, rope_dim), (0, 2, 1)),
                         min(32, page_table.shape[1]))
    mla_o_s = _headmm(jnp.transpose(o_lat, (1, 0, 2)), wts["wvh"], BF16)
    mla_o_s = jnp.transpose(mla_o_s, (1, 0, 2)).reshape(bd, n_mh * HEAD_PAD)
    x1, gid = _merge(xs, ret_o_s, gret, mla_o_s, gmla, wts, group_size, n_exp, bd)
    y_sample = _moe(x1, gid[:, 0], wts, wg, wu, wd, gfin, group_size, n_exp,
                    min(MOE_ROW_TILE, bd)).reshape(bd, tn, d_model)

    return (y_prompt, y_sample,
            lat_p.reshape(depth, b, t // page, page, kv_rank),
            rope_p.reshape(depth, b, t // page, page, rope_dim),
            st_p.reshape(depth, b, n_rh, dk, dv),
            lat_s.reshape(depth, bd, tn, kv_rank),
            rope_s.reshape(depth, bd, tn, rope_dim),
            st_s.reshape(depth, bd, n_rh, dk, dv))
```

```python
import functools
import math

import numpy as np
import jax
import jax.numpy as jnp
from jax import lax
from jax.experimental import pallas as pl
from jax.experimental.pallas import tpu as pltpu

F32 = jnp.float32
BF16 = jnp.bfloat16

ROPE_BASE = 10000.0
EPS = 1e-6
RET_CHUNK = 128
LANES = 128
HEAD_PAD = LANES
MOE_ROW_TILE = 256
RET_SAMPLE_SEQS = 4
DECODE_SLOTS = 3
ATTN_TQ = 512
ATTN_HEADS_PER_STEP = 8
VMEM_LIMIT = 48 * 1024 * 1024


def _cparams(n_axes):
    return pltpu.CompilerParams(dimension_semantics=("arbitrary",) * n_axes,
                                vmem_limit_bytes=VMEM_LIMIT)


def _const_spec(shape):
    nd = len(shape)
    return pl.BlockSpec(shape, lambda *_: (0,) * nd, pipeline_mode=pl.Buffered(1))


def _rms(x, g):
    return x * lax.rsqrt(jnp.mean(x * x, axis=-1, keepdims=True) + EPS) * g


def _sigmoid(x):
    return 1.0 / (1.0 + jnp.exp(-x))


def _dot(a, b):
    return jnp.dot(a, b, preferred_element_type=F32)


def _dot_nt(a, b):
    return lax.dot_general(a, b, (((1,), (1,)), ((), ())), preferred_element_type=F32)


def _inproj_kernel(x_ref, gmix_ref, wmain_ref, wsmall_ref, gq_ref, gkv_ref, wq_ref, wqsw_ref,
                   wk_ref, wv_ref, cosr_ref, sinr_ref, cp_ref, sp_ref,
                   rq_o, rk_o, rv_o, rgs_o, gret_o, gmla_o, lat_o, rope_o, q_o, k_o, v_o,
                   *, n_rh, dk, dv, q_rank, kv_rank, rope_dim, n_mh, nope, dv_m, k_scale, q_scale):
    x = x_ref[...]
    h = _rms(x, gmix_ref[...]).astype(BF16)
    cosr = cosr_ref[...]
    sinr = sinr_ref[...]
    qk_w = n_rh * dk
    v_w = n_rh * dv

    def rope_heads(z, scale):
        outs = []
        for i in range(n_rh):
            seg = z[:, i * dk:(i + 1) * dk]
            rot = seg * cosr + pltpu.roll(seg, dk // 2, 1) * sinr
            outs.append(rot if scale is None else rot * scale)
        return jnp.concatenate(outs, axis=1)

    off = 0
    rq_o[...] = rope_heads(_dot(h, wmain_ref[:, off:off + qk_w]), None).astype(BF16)
    off += qk_w
    rk_o[...] = rope_heads(_dot(h, wmain_ref[:, off:off + qk_w]), k_scale).astype(BF16)
    off += qk_w
    rv_o[...] = _dot(h, wmain_ref[:, off:off + v_w]).astype(BF16)
    off += v_w
    rg = _dot(h, wmain_ref[:, off:off + v_w])
    rgs_o[...] = (rg * _sigmoid(rg)).astype(BF16)
    off += v_w
    d_model = x.shape[1]
    gret_o[...] = _sigmoid(_dot(h, wmain_ref[:, off:off + d_model])).astype(BF16)
    off += d_model
    gmla_o[...] = _sigmoid(_dot(h, wmain_ref[:, off:off + d_model])).astype(BF16)

    zs = _dot(h, wsmall_ref[...])
    cq = zs[:, :q_rank]
    ckv = zs[:, q_rank:q_rank + kv_rank]
    krp = zs[:, q_rank + kv_rank:q_rank + kv_rank + HEAD_PAD]
    krsw = zs[:, q_rank + kv_rank + HEAD_PAD:]
    cp = cp_ref[...]
    sp = sp_ref[...]
    cqn = _rms(cq, gq_ref[...]).astype(BF16)
    ckvn = _rms(ckv, gkv_ref[...])
    lat_o[...] = ckvn
    ckvb = ckvn.astype(BF16)
    kr_rot = krp * cp + krsw * sp
    rope_o[...] = pltpu.roll(kr_rot, HEAD_PAD - nope, 1)[:, :rope_dim]

    qh = _dot(cqn, wq_ref[...])
    qs = _dot(cqn, wqsw_ref[...])
    kh = _dot(ckvb, wk_ref[...])
    vh = _dot(ckvb, wv_ref[...])
    sum_lane = lax.broadcasted_iota(jnp.int32, cp.shape, 1) == dv_m
    for i in range(n_mh):
        sl = slice(i * HEAD_PAD, (i + 1) * HEAD_PAD)
        q_o[i] = ((qh[:, sl] * cp + qs[:, sl] * sp) * q_scale).astype(BF16)
        k_o[i] = (kh[:, sl] + kr_rot).astype(BF16)
        v_o[i] = jnp.where(sum_lane, 1.0, vh[:, sl]).astype(BF16)


def _inproj(x, tables, wts, dims, tm):
    n, d_model = x.shape
    cosr, sinr, cp, sp = tables
    tab_blocks = cosr.shape[0] // tm
    n_rh, dk, dv, q_rank, kv_rank, rope_dim, n_mh, nope, dv_m = dims
    grid = (n // tm,)
    tok = lambda w: pl.BlockSpec((tm, w), lambda i: (i, 0))
    tab = lambda w: pl.BlockSpec((tm, w), lambda i: (i % tab_blocks, 0))
    head = pl.BlockSpec((n_mh, tm, HEAD_PAD), lambda i: (0, i, 0))
    w_names = ("gmix", "wmain", "wsmall", "gq", "gkv", "wq", "wqsw", "wk", "wv")
    w_args = [wts[k] for k in w_names]
    in_specs = ([tok(d_model)] + [_const_spec(a.shape) for a in w_args]
                + [tab(dk), tab(dk), tab(HEAD_PAD), tab(HEAD_PAD)])
    qk_w, v_w = n_rh * dk, n_rh * dv
    out_shape = (
        jax.ShapeDtypeStruct((n, qk_w), BF16), jax.ShapeDtypeStruct((n, qk_w), BF16),
        jax.ShapeDtypeStruct((n, v_w), BF16), jax.ShapeDtypeStruct((n, v_w), BF16),
        jax.ShapeDtypeStruct((n, d_model), BF16), jax.ShapeDtypeStruct((n, d_model), BF16),
        jax.ShapeDtypeStruct((n, kv_rank), F32), jax.ShapeDtypeStruct((n, rope_dim), F32),
        jax.ShapeDtypeStruct((n_mh, n, HEAD_PAD), BF16), jax.ShapeDtypeStruct((n_mh, n, HEAD_PAD), BF16),
        jax.ShapeDtypeStruct((n_mh, n, HEAD_PAD), BF16),
    )
    out_specs = (tok(qk_w), tok(qk_w), tok(v_w), tok(v_w), tok(d_model), tok(d_model),
                 tok(kv_rank), tok(rope_dim), head, head, head)
    kern = functools.partial(_inproj_kernel, n_rh=n_rh, dk=dk, dv=dv, q_rank=q_rank, kv_rank=kv_rank,
                             rope_dim=rope_dim, n_mh=n_mh, nope=nope, dv_m=dv_m, k_scale=dk ** -0.5,
                             q_scale=(nope + rope_dim) ** -0.5 * math.log2(math.e))
    return pl.pallas_call(kern, out_shape=out_shape, grid=grid, in_specs=in_specs, out_specs=out_specs,
                          compiler_params=_cparams(1), name="inproj")(x, *w_args, cosr, sinr, cp, sp)


def _ret_prompt_kernel(q_ref, k_ref, v_ref, gate_ref, mask_ref, qd_ref, kd_ref, gn_ref,
                       o_ref, st_ref, s_scr, *, n_rh, dk, dv, chunk_decay):
    c = pl.program_id(0)

    @pl.when(c == 0)
    def _():
        s_scr[...] = jnp.zeros_like(s_scr)

    for b in range(q_ref.shape[0]):
        for i in range(n_rh):
            q = q_ref[b, :, i * dk:(i + 1) * dk]
            k = k_ref[b, :, i * dk:(i + 1) * dk]
            v = v_ref[b, :, i * dv:(i + 1) * dv]
            s = _dot_nt(q, k) * mask_ref[i]
            state = s_scr[b, i]
            o = _dot(s.astype(BF16), v) + _dot(q, state.astype(BF16)) * qd_ref[i]
            kd = (k.astype(F32) * kd_ref[i]).T.astype(BF16)
            s_scr[b, i] = state * chunk_decay[i] + _dot(kd, v)
            o = _rms(o, gn_ref[i]) * gate_ref[b, :, i * dv:(i + 1) * dv].astype(F32)
            o_ref[b, :, i * dv:(i + 1) * dv] = o.astype(BF16)

    @pl.when(c == pl.num_programs(0) - 1)
    def _():
        st_ref[...] = s_scr[...]


def _ret_tables(n_rh, chunk, dk, dv):
    log_g = np.log1p(-np.exp2(-5.0 - np.arange(n_rh, dtype=np.float64)))
    idx = np.arange(chunk, dtype=np.float64)
    diff = idx[:, None] - idx[None, :]
    mask = np.where(diff >= 0, np.exp(log_g[:, None, None] * np.maximum(diff, 0.0)), 0.0)
    qd = np.exp(log_g[:, None] * (idx + 1.0))
    kd = np.exp(log_g[:, None] * (chunk - 1.0 - idx))
    cd = np.exp(log_g * chunk)
    qd = np.broadcast_to(qd[:, :, None], (n_rh, chunk, dv))
    kd = np.broadcast_to(kd[:, :, None], (n_rh, chunk, dk))
    return (jnp.asarray(mask, F32), jnp.asarray(qd, F32), jnp.asarray(kd, F32),
            tuple(float(np.float32(v)) for v in cd), tuple(float(np.float32(v)) for v in np.exp(log_g)))


def _ret_prompt(rq, rk, rv, rgs, gn, n_rh, dk, dv):
    b, t, _ = rq.shape
    chunk = RET_CHUNK
    mask, qd, kd, cd, _ = _ret_tables(n_rh, chunk, dk, dv)
    tokb = lambda w: pl.BlockSpec((b, chunk, w), lambda c: (0, c, 0))
    in_specs = [tokb(n_rh * dk), tokb(n_rh * dk), tokb(n_rh * dv), tokb(n_rh * dv),
                _const_spec(mask.shape), _const_spec(qd.shape), _const_spec(kd.shape),
                _const_spec((n_rh, 1, dv))]
    out_shape = (jax.ShapeDtypeStruct((b, t, n_rh * dv), BF16),
                 jax.ShapeDtypeStruct((b, n_rh, dk, dv), F32))
    out_specs = (tokb(n_rh * dv), pl.BlockSpec((b, n_rh, dk, dv), lambda c: (0, 0, 0, 0)))
    kern = functools.partial(_ret_prompt_kernel, n_rh=n_rh, dk=dk, dv=dv, chunk_decay=cd)
    return pl.pallas_call(kern, out_shape=out_shape, grid=(t // chunk,), in_specs=in_specs,
                          out_specs=out_specs, scratch_shapes=[pltpu.VMEM((b, n_rh, dk, dv), F32)],
                          compiler_params=_cparams(1), name="retention_prompt")(
        rq, rk, rv, rgs, mask, qd, kd, gn.reshape(n_rh, 1, dv))


def _ret_sample_kernel(qt_ref, kt_ref, v_ref, gate_ref, st_ref, gn_ref, o_ref, ns_ref,
                       *, n_rh, dk, dv, gamma):
    seqs = v_ref.shape[0]
    lane = lax.broadcasted_iota(jnp.int32, qt_ref.shape, 1)
    for j in range(seqs):
        sel = lane == pl.program_id(0) * seqs + j
        qcol = jnp.sum(jnp.where(sel, qt_ref[...], 0.0), axis=1, keepdims=True)
        kcol = jnp.sum(jnp.where(sel, kt_ref[...], 0.0), axis=1, keepdims=True)
        v = v_ref[j]
        gate = gate_ref[j]
        for i in range(n_rh):
            vi = v[:, i * dv:(i + 1) * dv]
            new = st_ref[j, i] * gamma[i] + kcol[i * dk:(i + 1) * dk] * vi
            ns_ref[j, i] = new
            o = jnp.sum(qcol[i * dk:(i + 1) * dk] * new, axis=0, keepdims=True)
            o_ref[j, :, i * dv:(i + 1) * dv] = _rms(o, gn_ref[i]) * gate[:, i * dv:(i + 1) * dv]


def _ret_sample(rq, rk, rv, rgs, state, gn, n_rh, dk, dv):
    nb = rq.shape[0]
    _, _, _, _, gamma = _ret_tables(n_rh, 1, dk, dv)
    qt = rq.astype(F32).T
    kt = rk.astype(F32).T
    seqs = math.gcd(nb, RET_SAMPLE_SEQS)
    row = lambda w: pl.BlockSpec((seqs, 1, w), lambda b: (b, 0, 0))
    st_spec = pl.BlockSpec((seqs, n_rh, dk, dv), lambda b: (b, 0, 0, 0))
    in_specs = [_const_spec(qt.shape), _const_spec(kt.shape), row(n_rh * dv), row(n_rh * dv), st_spec,
                _const_spec((n_rh, 1, dv))]
    out_shape = (jax.ShapeDtypeStruct((nb, 1, n_rh * dv), F32),
                 jax.ShapeDtypeStruct((nb, n_rh, dk, dv), F32))
    kern = functools.partial(_ret_sample_kernel, n_rh=n_rh, dk=dk, dv=dv, gamma=gamma)
    o, ns = pl.pallas_call(kern, out_shape=out_shape, grid=(nb // seqs,), in_specs=in_specs,
                           out_specs=(row(n_rh * dv), st_spec), compiler_params=_cparams(1),
                           name="retention_sample")(
        qt, kt, rv.astype(F32).reshape(nb, 1, -1), rgs.astype(F32).reshape(nb, 1, -1), state,
        gn.reshape(n_rh, 1, dv))
    return o.reshape(nb, n_rh * dv).astype(BF16), ns


def _attn_prompt_kernel(q_ref, k_ref, v_ref, o_ref, m_scr, acc_scr, *, tq, hpg, sum_lane):
    i = pl.program_id(2)
    m_scr[...] = jnp.full(m_scr.shape, -jnp.inf, F32)
    acc_scr[...] = jnp.zeros(acc_scr.shape, F32)

    def tile(j, masked):
        start = pl.multiple_of(j * tq, tq)
        for h in range(hpg):
            s = _dot_nt(q_ref[h], k_ref[h, pl.ds(start, tq), :])
            if masked:
                row = lax.broadcasted_iota(jnp.int32, s.shape, 0)
                col = lax.broadcasted_iota(jnp.int32, s.shape, 1)
                s = jnp.where(col <= row, s, -jnp.inf)
            m = m_scr[h]
            m_new = jnp.maximum(m, jnp.max(s, axis=1, keepdims=True))
            m_wide = jnp.concatenate([m_new] * (tq // HEAD_PAD), axis=1)
            p = jnp.exp2((s - m_wide).astype(BF16))
            acc_scr[h] = jnp.exp2(m - m_new) * acc_scr[h] + _dot(p, v_ref[h, pl.ds(start, tq), :])
            m_scr[h] = m_new

    def body(j, carry):
        tile(j, False)
        return carry

    lax.fori_loop(0, i, body, 0)
    tile(i, True)
    for h in range(hpg):
        acc = acc_scr[h]
        o_ref[:, h * HEAD_PAD:(h + 1) * HEAD_PAD] = (acc / acc[:, sum_lane:sum_lane + 1]).astype(BF16)


def _attn_prompt(q, k, v, b, t, tq, hpg, sum_lane):
    n_mh, n, _ = q.shape
    nq = t // tq
    kv_spec = pl.BlockSpec((hpg, t, HEAD_PAD), lambda bi, g, i: (g, bi, 0), pipeline_mode=pl.Buffered(1))
    in_specs = [pl.BlockSpec((hpg, tq, HEAD_PAD), lambda bi, g, i: (g, bi * nq + i, 0)), kv_spec, kv_spec]
    out_specs = pl.BlockSpec((tq, hpg * HEAD_PAD), lambda bi, g, i: (bi * nq + i, g))
    return pl.pallas_call(functools.partial(_attn_prompt_kernel, tq=tq, hpg=hpg, sum_lane=sum_lane),
                          out_shape=jax.ShapeDtypeStruct((n, n_mh * HEAD_PAD), BF16),
                          grid=(b, n_mh // hpg, nq), in_specs=in_specs, out_specs=out_specs,
                          scratch_shapes=[pltpu.VMEM((hpg, tq, HEAD_PAD), F32),
                                          pltpu.VMEM((hpg, tq, HEAD_PAD), F32)],
                          compiler_params=_cparams(3), name="attn_prompt")(q, k, v)


def _attn_sample_kernel(pt_ref, ql_ref, qp_ref, cn_ref, rn_ref, lat_hbm, rope_hbm, o_ref,
                        lat_slab, rope_slab, sems, m_scr, l_scr, acc_scr, *, npg, n_chunks, page):
    step = pl.program_id(0)
    n_steps = pl.num_programs(0)
    n_slots = lat_slab.shape[0]
    ahead = n_slots - 1
    slot = step % n_slots
    chunk = step % n_chunks

    def page_copies(page_ids, slot_):
        out = []
        for i in range(npg):
            rows = pl.ds(i * page, page)
            out.append(pltpu.make_async_copy(lat_hbm.at[page_ids(i)], lat_slab.at[slot_, rows, :],
                                             sems.at[0, slot_]))
            out.append(pltpu.make_async_copy(rope_hbm.at[page_ids(i)], rope_slab.at[slot_, :, rows],
                                             sems.at[1, slot_]))
        return out

    def start_step(step_, slot_):
        seq = step_ // n_chunks
        first = (step_ % n_chunks) * npg
        for cp in page_copies(lambda i: pt_ref[seq, first + i], slot_):
            cp.start()

    for k in range(ahead):
        @pl.when(jnp.logical_and(step == 0, k < n_steps))
        def _():
            start_step(k, k)

    @pl.when(step + ahead < n_steps)
    def _():
        start_step(step + ahead, (step + ahead) % n_slots)

    for cp in page_copies(lambda i: 0, slot):
        cp.wait()

    ql = ql_ref[0]
    qp = qp_ref[0]

    @pl.when(chunk == 0)
    def _():
        cn = cn_ref[0]
        s0 = (jnp.sum(ql * cn, axis=1, keepdims=True)
              + jnp.sum(qp * rn_ref[0], axis=1, keepdims=True))
        m_scr[...] = s0
        l_scr[...] = jnp.ones_like(l_scr)
        acc_scr[...] = jnp.broadcast_to(cn, acc_scr.shape)

    lat = lat_slab[slot].astype(BF16)
    s = _dot_nt(ql.astype(BF16), lat) + _dot(qp, rope_slab[slot])
    m_prev = m_scr[...]
    m_new = jnp.maximum(m_prev, jnp.max(s, axis=1, keepdims=True))
    alpha = jnp.exp2(m_prev - m_new)
    p = jnp.exp2(s - m_new)
    m_scr[...] = m_new
    l_scr[...] = alpha * l_scr[...] + jnp.sum(p, axis=1, keepdims=True)
    acc_scr[...] = alpha * acc_scr[...] + _dot(p.astype(BF16), lat)

    @pl.when(chunk == n_chunks - 1)
    def _():
        o_ref[0] = acc_scr[...] / l_scr[...]


def _attn_sample(page_table, q_lat, q_pe, c_new, r_new, cache_lat, cache_rope_t, npg):
    nb, n_mh, kv_rank = q_lat.shape
    rope_dim = q_pe.shape[-1]
    n_pages = page_table.shape[1]
    page = cache_lat.shape[1]
    n_chunks = n_pages // npg
    per_seq = lambda shp: pl.BlockSpec((1,) + shp, lambda s, pt: (s // n_chunks, 0, 0))
    hbm = pl.BlockSpec(memory_space=pl.ANY)
    in_specs = [per_seq((n_mh, kv_rank)), per_seq((n_mh, rope_dim)), per_seq((1, kv_rank)),
                per_seq((1, rope_dim)), hbm, hbm]
    grid_spec = pltpu.PrefetchScalarGridSpec(
        num_scalar_prefetch=1, grid=(nb * n_chunks,), in_specs=in_specs,
        out_specs=per_seq((n_mh, kv_rank)),
        scratch_shapes=[pltpu.VMEM((DECODE_SLOTS, npg * page, kv_rank), F32),
                        pltpu.VMEM((DECODE_SLOTS, rope_dim, npg * page), F32),
                        pltpu.SemaphoreType.DMA((2, DECODE_SLOTS)),
                        pltpu.VMEM((n_mh, 1), F32), pltpu.VMEM((n_mh, 1), F32),
                        pltpu.VMEM((n_mh, kv_rank), F32)])
    kern = functools.partial(_attn_sample_kernel, npg=npg, n_chunks=n_chunks, page=page)
    return pl.pallas_call(kern, out_shape=jax.ShapeDtypeStruct((nb, n_mh, kv_rank), F32),
                          grid_spec=grid_spec, compiler_params=_cparams(1), name="attn_sample")(
        page_table, q_lat, q_pe, c_new, r_new, cache_lat, cache_rope_t)


def _headmm_kernel(a_ref, w_ref, o_ref):
    o_ref[0] = _dot(a_ref[0].astype(BF16), w_ref[0]).astype(o_ref.dtype)


def _headmm(a, w, out_dtype):
    nh, m, kk = a.shape
    nn = w.shape[-1]
    spec = lambda r, c: pl.BlockSpec((1, r, c), lambda h: (h, 0, 0))
    return pl.pallas_call(_headmm_kernel, out_shape=jax.ShapeDtypeStruct((nh, m, nn), out_dtype),
                          grid=(nh,), in_specs=[spec(m, kk), spec(kk, nn)], out_specs=spec(m, nn),
                          compiler_params=_cparams(1), name="head_matmul")(a, w)


def _router_logits(h2, whi_ref, wlo_ref):
    hi = h2.astype(BF16)
    lo = (h2 - hi.astype(F32)).astype(BF16)
    lg = _dot(hi, whi_ref[...]) + (_dot(hi, wlo_ref[...]) + _dot(lo, whi_ref[...]))
    return hi, lg[:, :LANES], lg[:, LANES:]


def _merge_kernel(x_ref, ro_ref, gr_ref, mo_ref, gm_ref, wro_ref, wmo_ref, wout_ref, gffn_ref,
                  whi_ref, wlo_ref, x1_o, gid_o, *, group_size, n_exp):
    mixed = (_dot(ro_ref[...], wro_ref[...]) * gr_ref[...].astype(F32)
             + _dot(mo_ref[...], wmo_ref[...]) * gm_ref[...].astype(F32))
    x1 = x_ref[...] + _dot(mixed.astype(BF16), wout_ref[...])
    x1_o[...] = x1
    _, ge, _ = _router_logits(_rms(x1, gffn_ref[...]), whi_ref, wlo_ref)
    lane = lax.broadcasted_iota(jnp.int32, ge.shape, 1)
    ge = jnp.where(lane < n_exp, ge, -jnp.inf)
    grp = (lane // group_size).astype(F32)
    gmax = jnp.max(ge, axis=1, keepdims=True)
    g_idx = jnp.min(jnp.where(ge == gmax, grp, float(n_exp)), axis=1, keepdims=True)
    gid_o[...] = jnp.broadcast_to(g_idx, ge.shape).astype(jnp.int32)


def _merge(x, ro, gr, mo, gm, wts, group_size, n_exp, tm):
    n, d_model = x.shape
    tok = lambda w: pl.BlockSpec((tm, w), lambda i: (i, 0))
    w_args = [wts[k] for k in ("wro", "wmo", "wout", "gffn", "wr_hi", "wr_lo")]
    in_specs = [tok(d_model), tok(ro.shape[1]), tok(d_model), tok(mo.shape[1]), tok(d_model)] + [
        _const_spec(a.shape) for a in w_args]
    out_shape = (jax.ShapeDtypeStruct((n, d_model), F32), jax.ShapeDtypeStruct((n, LANES), jnp.int32))
    kern = functools.partial(_merge_kernel, group_size=group_size, n_exp=n_exp)
    return pl.pallas_call(kern, out_shape=out_shape, grid=(n // tm,), in_specs=in_specs,
                          out_specs=(tok(d_model), tok(LANES)),
                          compiler_params=_cparams(1), name="merge_router")(x, ro, gr, mo, gm, *w_args)


def _group_tiles(gid, n_grp, tmr):
    n = gid.shape[0]
    onehot = (gid[:, None] == jnp.arange(n_grp, dtype=jnp.int32)[None, :]).astype(jnp.int32)
    incl = jnp.cumsum(onehot, axis=0)
    rank = jnp.sum((incl - onehot) * onehot, axis=1)
    counts = incl[-1]
    padded = ((counts + tmr - 1) // tmr) * tmr
    ends = jnp.cumsum(padded)
    offs = ends - padded
    pos = jnp.sum(onehot * offs[None, :], axis=1) + rank
    n_tiles = n // tmr + n_grp
    row_token = jnp.zeros((n_tiles * tmr,), jnp.int32).at[pos].set(jnp.arange(n, dtype=jnp.int32))
    tile_start = jnp.arange(n_tiles, dtype=jnp.int32) * tmr
    tile_group = jnp.minimum(jnp.sum((tile_start[:, None] >= ends[None, :]).astype(jnp.int32), axis=1),
                             n_grp - 1)
    tile_valid = jnp.clip(counts[tile_group] - (tile_start - offs[tile_group]), 0, tmr)
    return row_token, tile_group, tile_valid.astype(jnp.int32)


def _moe_kernel(rt_ref, tg_ref, tv_ref, x1_hbm, gffn_ref, gfin_ref, whi_ref, wlo_ref, wg_ref, wu_ref, wd_ref,
                y_hbm, xbuf, obuf, trash, gsem, ssem, *, tmr, group_size, n_exp):
    t = pl.program_id(0)
    last = pl.num_programs(0) - 1
    slot = t % 2

    def gather_copy(tok, r, slot_):
        return pltpu.make_async_copy(x1_hbm.at[pl.ds(tok, 1)], xbuf.at[slot_, pl.ds(r, 1)], gsem.at[slot_])

    def start_gather(tile, slot_):
        for r in range(tmr):
            gather_copy(rt_ref[tile * tmr + r], r, slot_).start()

    def wait_gather(slot_):
        for r in range(tmr):
            gather_copy(0, r, slot_).wait()

    def scatter_copy(r, dst_row_ref, slot_):
        return pltpu.make_async_copy(obuf.at[slot_, pl.ds(r, 1)], dst_row_ref, ssem.at[slot_])

    def wait_scatter(slot_):
        for r in range(tmr):
            scatter_copy(r, trash.at[slot_, pl.ds(r, 1)], slot_).wait()

    @pl.when(t == 0)
    def _():
        start_gather(t, slot)

    wait_gather(slot)

    @pl.when(t >= 2)
    def _():
        wait_scatter(slot)

    start_gather(jnp.minimum(t + 1, last), 1 - slot)

    g = tg_ref[t]
    x = xbuf[slot]
    hb, ge, el = _router_logits(_rms(x, gffn_ref[...]), whi_ref, wlo_ref)
    lane_i = lax.broadcasted_iota(jnp.int32, ge.shape, 1)
    lane = lane_i.astype(F32)
    is_exp = lane_i < n_exp
    in_grp = (lane_i // group_size) == g
    gmax = jnp.max(jnp.where(is_exp, ge, -jnp.inf), axis=1, keepdims=True)
    gsum = jnp.sum(jnp.where(is_exp, jnp.exp(ge - gmax), 0.0), axis=1, keepdims=True) / group_size
    ge_g = jnp.max(jnp.where(in_grp, ge, -jnp.inf), axis=1, keepdims=True)
    g_w = jnp.exp(ge_g - gmax) / gsum
    e_in = jnp.where(in_grp, el, -jnp.inf)
    top1 = jnp.max(e_in, axis=1, keepdims=True)
    idx1 = jnp.min(jnp.where(e_in == top1, lane, float(LANES)), axis=1, keepdims=True)
    e_rest = jnp.where(lane == idx1, -jnp.inf, e_in)
    top2 = jnp.max(e_rest, axis=1, keepdims=True)
    idx2 = jnp.min(jnp.where(e_rest == top2, lane, float(LANES)), axis=1, keepdims=True)
    e2 = jnp.exp(top2 - top1)
    comb = (jnp.where(lane == idx1, g_w / (1.0 + e2), 0.0)
            + jnp.where(lane == idx2, g_w * e2 / (1.0 + e2), 0.0))

    first = (g * group_size).astype(F32)
    parts = []
    for e in range(group_size):
        ce = jnp.sum(jnp.where(lane == first + e, comb, 0.0), axis=1, keepdims=True)
        hg = _dot(hb, wg_ref[e])
        parts.append((hg * _sigmoid(hg) * _dot(hb, wu_ref[e]) * ce).astype(BF16))
    y = x + _dot(jnp.concatenate(parts, axis=1), wd_ref[0])
    obuf[slot] = _rms(y, gfin_ref[...])

    n_real = tv_ref[t]

    def scatter_real(r, carry):
        tok = rt_ref[t * tmr + r]
        scatter_copy(r, y_hbm.at[pl.ds(tok, 1)], slot).start()
        return carry

    def scatter_pad(r, carry):
        scatter_copy(r, trash.at[slot, pl.ds(r, 1)], slot).start()
        return carry

    @pl.when(n_real == tmr)
    def _():
        for r in range(tmr):
            scatter_real(r, 0)

    @pl.when(n_real < tmr)
    def _():
        lax.fori_loop(0, n_real, scatter_real, 0)
        lax.fori_loop(n_real, tmr, scatter_pad, 0)

    @pl.when(t == last)
    def _():
        wait_gather(1 - slot)
        wait_scatter(slot)

        @pl.when(t >= 1)
        def _():
            wait_scatter(1 - slot)


def _moe(x1, gid, wts, wg, wu, wd, gfin, group_size, n_exp, tmr):
    n, d_model = x1.shape
    n_grp = wd.shape[0]
    row_token, tile_group, tile_valid = _group_tiles(gid, n_grp, tmr)
    n_tiles = tile_group.shape[0]
    hbm = pl.BlockSpec(memory_space=pl.ANY)
    const = lambda a: pl.BlockSpec(a.shape, lambda t, rt, tg, tv: (0,) * a.ndim, pipeline_mode=pl.Buffered(1))
    grp_w = lambda a: pl.BlockSpec((a.shape[0] // n_grp,) + a.shape[1:], lambda t, rt, tg, tv: (tg[t], 0, 0))
    w_args = [wts["gffn"], gfin, wts["wr_hi"], wts["wr_lo"]]
    grid_spec = pltpu.PrefetchScalarGridSpec(
        num_scalar_prefetch=3, grid=(n_tiles,),
        in_specs=[hbm] + [const(a) for a in w_args] + [grp_w(wg), grp_w(wu), grp_w(wd)],
        out_specs=hbm,
        scratch_shapes=[pltpu.VMEM((2, tmr, d_model), F32), pltpu.VMEM((2, tmr, d_model), F32),
                        pltpu.VMEM((2, tmr, d_model), F32),
                        pltpu.SemaphoreType.DMA((2,)), pltpu.SemaphoreType.DMA((2,))])
    kern = functools.partial(_moe_kernel, tmr=tmr, group_size=group_size, n_exp=n_exp)
    return pl.pallas_call(kern, out_shape=jax.ShapeDtypeStruct((n, d_model), F32), grid_spec=grid_spec,
                          compiler_params=_cparams(1), name="moe")(
        row_token, tile_group, tile_valid, x1, *w_args, wg, wu, wd)


def _rope_tables(pos, dk, rope_dim, nope):
    def cs(half):
        inv = ROPE_BASE ** (-jnp.arange(half, dtype=F32) / half)
        ang = pos[:, None] * inv[None, :]
        return jnp.cos(ang), jnp.sin(ang)

    c, s = cs(dk // 2)
    cosr = jnp.concatenate([c, c], axis=1)
    sinr = jnp.concatenate([-s, s], axis=1)
    c, s = cs(rope_dim // 2)
    n = pos.shape[0]
    tail = jnp.zeros((n, HEAD_PAD - nope - rope_dim), F32)
    cp = jnp.concatenate([jnp.ones((n, nope), F32), c, c, tail], axis=1)
    sp = jnp.concatenate([jnp.zeros((n, nope), F32), -s, s, tail], axis=1)
    return cosr, sinr, cp, sp


def _swap_halves(w):
    half = w.shape[-1] // 2
    return jnp.concatenate([w[..., half:], w[..., :half]], axis=-1)


def _layout_weights(norm_mix_g, w_in, q_norm_g, kv_norm_g, w_uq, w_ukv, w_ret_o, w_mla_o, w_out,
                    norm_ffn_g, w_router_group, w_router_expert, dims):
    n_rh, dk, dv, q_rank, kv_rank, rope_dim, n_mh, nope, dv_m = dims
    d_model = w_in.shape[0]
    qk_w, v_w = n_rh * dk, n_rh * dv
    o_cq = 2 * qk_w + 2 * v_w
    o_ckv = o_cq + q_rank
    o_kr = o_ckv + kv_rank
    o_gate = o_kr + rope_dim
    pad_tail = HEAD_PAD - nope - rope_dim
    zeros = lambda *shape: jnp.zeros(shape, F32)
    w_kr = w_in[:, o_kr:o_gate]
    place = lambda w: jnp.concatenate([zeros(d_model, nope), w, zeros(d_model, pad_tail)], axis=1)
    wsmall = jnp.concatenate([w_in[:, o_cq:o_kr], place(w_kr), place(_swap_halves(w_kr))], axis=1)
    wmain = jnp.concatenate([w_in[:, :o_cq], w_in[:, o_gate:]], axis=1)

    uq = w_uq.reshape(q_rank, n_mh, nope + rope_dim)
    uq_rope = uq[..., nope:]
    wq = jnp.concatenate([uq, zeros(q_rank, n_mh, pad_tail)], axis=-1)
    wqsw = jnp.concatenate([zeros(q_rank, n_mh, nope), _swap_halves(uq_rope),
                            zeros(q_rank, n_mh, pad_tail)], axis=-1)
    uk = w_ukv[..., :nope]
    uv = w_ukv[..., nope:]
    wk = jnp.concatenate([uk, zeros(kv_rank, n_mh, HEAD_PAD - nope)], axis=-1)
    wv = jnp.concatenate([uv, zeros(kv_rank, n_mh, HEAD_PAD - dv_m)], axis=-1)
    flat = lambda w: w.reshape(w.shape[0], n_mh * HEAD_PAD).astype(BF16)
    wmo = jnp.concatenate([w_mla_o.reshape(n_mh, dv_m, d_model),
                           zeros(n_mh, HEAD_PAD - dv_m, d_model)], axis=1).reshape(n_mh * HEAD_PAD, d_model)
    wabs = jnp.concatenate([jnp.transpose(uk, (1, 2, 0)), zeros(n_mh, HEAD_PAD - nope, kv_rank)], axis=1)
    wvh = jnp.transpose(wv, (1, 0, 2))
    n_grp = w_router_group.shape[1]
    n_exp = w_router_expert.shape[1]
    group_size = n_exp // n_grp
    assert n_exp <= LANES
    w_r = jnp.concatenate([jnp.repeat(w_router_group, group_size, axis=1), zeros(d_model, LANES - n_exp),
                           w_router_expert, zeros(d_model, LANES - n_exp)], axis=1)
    wr_hi = w_r.astype(BF16)
    wr_lo = (w_r - wr_hi.astype(F32)).astype(BF16)
    return dict(
        gmix=norm_mix_g.reshape(1, -1), wmain=wmain.astype(BF16), wsmall=wsmall.astype(BF16),
        gq=q_norm_g.reshape(1, -1), gkv=kv_norm_g.reshape(1, -1),
        wq=flat(wq), wqsw=flat(wqsw), wk=flat(wk), wv=flat(wv),
        wro=w_ret_o.astype(BF16), wmo=wmo.astype(BF16), wout=w_out.astype(BF16),
        gffn=norm_ffn_g.reshape(1, -1), wr_hi=wr_hi, wr_lo=wr_lo,
        wabs=wabs.astype(BF16), wvh=wvh.astype(BF16)), group_size, n_exp


def kernel(x_prompt, x_sample, cache_kv_latent, cache_k_rope, state_retention, page_table, norm_mix_g,
           w_in, q_norm_g, kv_norm_g, w_uq, w_ukv, ret_gn_g, w_ret_o, w_mla_o, w_out, norm_ffn_g,
           w_router_group, w_router_expert, w_gate, w_up, w_down, norm_final_g):
    depth = w_in.shape[0]
    assert depth == 1, "single-layer step only"
    b, t, d_model = x_prompt.shape
    bd, tn, _ = x_sample.shape
    assert tn == 1, "one new token per sample sequence"
    _, _, n_rh, dk, dv = state_retention.shape
    kv_rank = cache_kv_latent.shape[-1]
    rope_dim = cache_k_rope.shape[-1]
    page = cache_kv_latent.shape[2]
    n_mh = w_ukv.shape[2]
    q_rank = w_uq.shape[1]
    nope = w_uq.shape[2] // n_mh - rope_dim
    dv_m = w_ukv.shape[3] - nope
    assert nope + rope_dim <= HEAD_PAD and dv_m < HEAD_PAD
    dims = (n_rh, dk, dv, q_rank, kv_rank, rope_dim, n_mh, nope, dv_m)
    past_len = page_table.shape[1] * page

    wts, group_size, n_exp = _layout_weights(norm_mix_g[0], w_in[0], q_norm_g[0], kv_norm_g[0], w_uq[0],
                                             w_ukv[0], w_ret_o[0], w_mla_o[0], w_out[0], norm_ffn_g[0],
                                             w_router_group[0], w_router_expert[0], dims)
    gn = ret_gn_g[0]
    n_grp = n_exp // group_size
    d_exp = w_gate.shape[-1]
    wg, wu = w_gate[0].astype(BF16), w_up[0].astype(BF16)
    wd = w_down[0].astype(BF16).reshape(n_grp, group_size * d_exp, d_model)
    gfin = norm_final_g.reshape(1, -1)

    tm_p = min(256, t)
    xp = x_prompt.reshape(b * t, d_model)
    tabs_p = _rope_tables(jnp.arange(t, dtype=F32), dk, rope_dim, nope)
    (rq, rk, rv, rgs, gret, gmla, lat_p, rope_p, q_p, k_p, v_p) = _inproj(xp, tabs_p, wts, dims, tm_p)
    ret_o, st_p = _ret_prompt(rq.reshape(b, t, -1), rk.reshape(b, t, -1), rv.reshape(b, t, -1),
                              rgs.reshape(b, t, -1), gn, n_rh, dk, dv)
    mla_o = _attn_prompt(q_p, k_p, v_p, b, t, min(ATTN_TQ, t), min(ATTN_HEADS_PER_STEP, n_mh), dv_m)
    x1, gid = _merge(xp, ret_o.reshape(b * t, -1), gret, mla_o, gmla, wts, group_size, n_exp, tm_p)
    y_prompt = _moe(x1, gid[:, 0], wts, wg, wu, wd, gfin, group_size, n_exp,
                    min(MOE_ROW_TILE, b * t)).reshape(b, t, d_model)

    xs = x_sample.reshape(bd, d_model)
    tabs_s = _rope_tables(jnp.full((bd,), float(past_len), F32), dk, rope_dim, nope)
    (rq, rk, rv, rgs, gret, gmla, lat_s, rope_s, q_s, _, _) = _inproj(xs, tabs_s, wts, dims, bd)
    ret_o_s, st_s = _ret_sample(rq, rk, rv, rgs, state_retention[0], gn, n_rh, dk, dv)
    q_lat = jnp.transpose(_headmm(q_s, wts["wabs"], F32), (1, 0, 2))
    q_pe = jnp.transpose(q_s[:, :, nope:nope + rope_dim].astype(F32), (1, 0, 2))
    o_lat = _attn_sample(page_table, q_lat, q_pe, lat_s.reshape(bd, 1, kv_rank),
                         rope_s.reshape(bd, 1, rope_dim), cache_kv_latent.reshape(-1, page, kv_rank),
                         jnp.transpose(cache_k_rope.reshape(-1, page, rope_dim), (0, 2, 1)),
                         min(32, page_table.shape[1]))
    mla_o_s = _headmm(jnp.transpose(o_lat, (1, 0, 2)), wts["wvh"], BF16)
    mla_o_s = jnp.transpose(mla_o_s, (1, 0, 2)).reshape(bd, n_mh * HEAD_PAD)
    x1, gid = _merge(xs, ret_o_s, gret, mla_o_s, gmla, wts, group_size, n_exp, bd)
    y_sample = _moe(x1, gid[:, 0], wts, wg, wu, wd, gfin, group_size, n_exp,
                    min(MOE_ROW_TILE, bd)).reshape(bd, tn, d_model)

    return (y_prompt, y_sample,
            lat_p.reshape(depth, b, t // page, page, kv_rank),
            rope_p.reshape(depth, b, t // page, page, rope_dim),
            st_p.reshape(depth, b, n_rh, dk, dv),
            lat_s.reshape(depth, bd, tn, kv_rank),
            rope_s.reshape(depth, bd, tn, rope_dim),
            st_s.reshape(depth, bd, n_rh, dk, dv))
```

```python
import functools
import math

import numpy as np
import jax
import jax.numpy as jnp
from jax import lax
from jax.experimental import pallas as pl
from jax.experimental.pallas import tpu as pltpu

F32 = jnp.float32
BF16 = jnp.bfloat16

ROPE_BASE = 10000.0
EPS = 1e-6
RET_CHUNK = 128
LANES = 128
HEAD_PAD = LANES
MERGE_ROW_TILE = 512
MOE_ROW_TILE = 256
RET_SAMPLE_SEQS = 4
DECODE_PAGES = 64
DECODE_SLOTS = 3
ATTN_TQ = 512
ATTN_HEADS_PER_STEP = 8
VMEM_LIMIT = 48 * 1024 * 1024


def _cparams(n_axes):
    return pltpu.CompilerParams(dimension_semantics=("arbitrary",) * n_axes,
                                vmem_limit_bytes=VMEM_LIMIT)


def _const_spec(shape):
    nd = len(shape)
    return pl.BlockSpec(shape, lambda *_: (0,) * nd, pipeline_mode=pl.Buffered(1))


def _rms(x, g):
    return x * lax.rsqrt(jnp.mean(x * x, axis=-1, keepdims=True) + EPS) * g


def _sigmoid(x):
    return 1.0 / (1.0 + jnp.exp(-x))


def _dot(a, b):
    return jnp.dot(a, b, preferred_element_type=F32)


def _dot_nt(a, b):
    return lax.dot_general(a, b, (((1,), (1,)), ((), ())), preferred_element_type=F32)


def _inproj_kernel(x_ref, gmix_ref, wmain_ref, wsmall_ref, gq_ref, gkv_ref, wq_ref, wqsw_ref,
                   wk_ref, wv_ref, cosr_ref, sinr_ref, cp_ref, sp_ref,
                   rq_o, rk_o, rv_o, rgs_o, gret_o, gmla_o, lat_o, rope_o, q_o, k_o, v_o,
                   *, n_rh, dk, dv, q_rank, kv_rank, rope_dim, n_mh, nope, dv_m, k_scale, q_scale):
    x = x_ref[...]
    h = _rms(x, gmix_ref[...]).astype(BF16)
    cosr = cosr_ref[...]
    sinr = sinr_ref[...]
    qk_w = n_rh * dk
    v_w = n_rh * dv

    def rope_heads(z, scale):
        outs = []
        for i in range(n_rh):
            seg = z[:, i * dk:(i + 1) * dk]
            rot = seg * cosr + pltpu.roll(seg, dk // 2, 1) * sinr
            outs.append(rot if scale is None else rot * scale)
        return jnp.concatenate(outs, axis=1)

    off = 0
    rq_o[...] = rope_heads(_dot(h, wmain_ref[:, off:off + qk_w]), None).astype(BF16)
    off += qk_w
    rk_o[...] = rope_heads(_dot(h, wmain_ref[:, off:off + qk_w]), k_scale).astype(BF16)
    off += qk_w
    rv_o[...] = _dot(h, wmain_ref[:, off:off + v_w]).astype(BF16)
    off += v_w
    rg = _dot(h, wmain_ref[:, off:off + v_w])
    rgs_o[...] = (rg * _sigmoid(rg)).astype(BF16)
    off += v_w
    d_model = x.shape[1]
    gret_o[...] = _sigmoid(_dot(h, wmain_ref[:, off:off + d_model])).astype(BF16)
    off += d_model
    gmla_o[...] = _sigmoid(_dot(h, wmain_ref[:, off:off + d_model])).astype(BF16)

    zs = _dot(h, wsmall_ref[...])
    cq = zs[:, :q_rank]
    ckv = zs[:, q_rank:q_rank + kv_rank]
    krp = zs[:, q_rank + kv_rank:q_rank + kv_rank + HEAD_PAD]
    krsw = zs[:, q_rank + kv_rank + HEAD_PAD:]
    cp = cp_ref[...]
    sp = sp_ref[...]
    cqn = _rms(cq, gq_ref[...]).astype(BF16)
    ckvn = _rms(ckv, gkv_ref[...])
    lat_o[...] = ckvn
    ckvb = ckvn.astype(BF16)
    kr_rot = krp * cp + krsw * sp
    rope_o[...] = pltpu.roll(kr_rot, HEAD_PAD - nope, 1)[:, :rope_dim]

    qh = _dot(cqn, wq_ref[...])
    qs = _dot(cqn, wqsw_ref[...])
    kh = _dot(ckvb, wk_ref[...])
    vh = _dot(ckvb, wv_ref[...])
    sum_lane = lax.broadcasted_iota(jnp.int32, cp.shape, 1) == dv_m
    for i in range(n_mh):
        sl = slice(i * HEAD_PAD, (i + 1) * HEAD_PAD)
        q_o[i] = ((qh[:, sl] * cp + qs[:, sl] * sp) * q_scale).astype(BF16)
        k_o[i] = (kh[:, sl] + kr_rot).astype(BF16)
        v_o[i] = jnp.where(sum_lane, 1.0, vh[:, sl]).astype(BF16)


def _inproj(x, tables, wts, dims, tm):
    n, d_model = x.shape
    cosr, sinr, cp, sp = tables
    tab_blocks = cosr.shape[0] // tm
    n_rh, dk, dv, q_rank, kv_rank, rope_dim, n_mh, nope, dv_m = dims
    grid = (n // tm,)
    tok = lambda w: pl.BlockSpec((tm, w), lambda i: (i, 0))
    tab = lambda w: pl.BlockSpec((tm, w), lambda i: (i % tab_blocks, 0))
    head = pl.BlockSpec((n_mh, tm, HEAD_PAD), lambda i: (0, i, 0))
    w_names = ("gmix", "wmain", "wsmall", "gq", "gkv", "wq", "wqsw", "wk", "wv")
    w_args = [wts[k] for k in w_names]
    in_specs = ([tok(d_model)] + [_const_spec(a.shape) for a in w_args]
                + [tab(dk), tab(dk), tab(HEAD_PAD), tab(HEAD_PAD)])
    qk_w, v_w = n_rh * dk, n_rh * dv
    out_shape = (
        jax.ShapeDtypeStruct((n, qk_w), BF16), jax.ShapeDtypeStruct((n, qk_w), BF16),
        jax.ShapeDtypeStruct((n, v_w), BF16), jax.ShapeDtypeStruct((n, v_w), BF16),
        jax.ShapeDtypeStruct((n, d_model), BF16), jax.ShapeDtypeStruct((n, d_model), BF16),
        jax.ShapeDtypeStruct((n, kv_rank), F32), jax.ShapeDtypeStruct((n, rope_dim), F32),
        jax.ShapeDtypeStruct((n_mh, n, HEAD_PAD), BF16), jax.ShapeDtypeStruct((n_mh, n, HEAD_PAD), BF16),
        jax.ShapeDtypeStruct((n_mh, n, HEAD_PAD), BF16),
    )
    out_specs = (tok(qk_w), tok(qk_w), tok(v_w), tok(v_w), tok(d_model), tok(d_model),
                 tok(kv_rank), tok(rope_dim), head, head, head)
    kern = functools.partial(_inproj_kernel, n_rh=n_rh, dk=dk, dv=dv, q_rank=q_rank, kv_rank=kv_rank,
                             rope_dim=rope_dim, n_mh=n_mh, nope=nope, dv_m=dv_m, k_scale=dk ** -0.5,
                             q_scale=(nope + rope_dim) ** -0.5 * math.log2(math.e))
    return pl.pallas_call(kern, out_shape=out_shape, grid=grid, in_specs=in_specs, out_specs=out_specs,
                          compiler_params=_cparams(1), name="inproj")(x, *w_args, cosr, sinr, cp, sp)


def _ret_prompt_kernel(q_ref, k_ref, v_ref, gate_ref, mask_ref, qd_ref, kd_ref, gn_ref,
                       o_ref, st_ref, s_scr, *, n_rh, dk, dv, chunk_decay):
    c = pl.program_id(0)

    @pl.when(c == 0)
    def _():
        s_scr[...] = jnp.zeros_like(s_scr)

    for b in range(q_ref.shape[0]):
        for i in range(n_rh):
            q = q_ref[b, :, i * dk:(i + 1) * dk]
            k = k_ref[b, :, i * dk:(i + 1) * dk]
            v = v_ref[b, :, i * dv:(i + 1) * dv]
            s = _dot_nt(q, k) * mask_ref[i]
            state = s_scr[b, i]
            o = _dot(s.astype(BF16), v) + _dot(q, state.astype(BF16)) * qd_ref[i]
            kd = (k.astype(F32) * kd_ref[i]).T.astype(BF16)
            s_scr[b, i] = state * chunk_decay[i] + _dot(kd, v)
            o = _rms(o, gn_ref[i]) * gate_ref[b, :, i * dv:(i + 1) * dv].astype(F32)
            o_ref[b, :, i * dv:(i + 1) * dv] = o.astype(BF16)

    @pl.when(c == pl.num_programs(0) - 1)
    def _():
        st_ref[...] = s_scr[...]


def _ret_tables(n_rh, chunk, dk, dv):
    log_g = np.log1p(-np.exp2(-5.0 - np.arange(n_rh, dtype=np.float64)))
    idx = np.arange(chunk, dtype=np.float64)
    diff = idx[:, None] - idx[None, :]
    mask = np.where(diff >= 0, np.exp(log_g[:, None, None] * np.maximum(diff, 0.0)), 0.0)
    qd = np.exp(log_g[:, None] * (idx + 1.0))
    kd = np.exp(log_g[:, None] * (chunk - 1.0 - idx))
    cd = np.exp(log_g * chunk)
    qd = np.broadcast_to(qd[:, :, None], (n_rh, chunk, dv))
    kd = np.broadcast_to(kd[:, :, None], (n_rh, chunk, dk))
    return (jnp.asarray(mask, F32), jnp.asarray(qd, F32), jnp.asarray(kd, F32),
            tuple(float(np.float32(v)) for v in cd), tuple(float(np.float32(v)) for v in np.exp(log_g)))


def _ret_prompt(rq, rk, rv, rgs, gn, n_rh, dk, dv):
    b, t, _ = rq.shape
    chunk = RET_CHUNK
    mask, qd, kd, cd, _ = _ret_tables(n_rh, chunk, dk, dv)
    tokb = lambda w: pl.BlockSpec((b, chunk, w), lambda c: (0, c, 0))
    in_specs = [tokb(n_rh * dk), tokb(n_rh * dk), tokb(n_rh * dv), tokb(n_rh * dv),
                _const_spec(mask.shape), _const_spec(qd.shape), _const_spec(kd.shape),
                _const_spec((n_rh, 1, dv))]
    out_shape = (jax.ShapeDtypeStruct((b, t, n_rh * dv), BF16),
                 jax.ShapeDtypeStruct((b, n_rh, dk, dv), F32))
    out_specs = (tokb(n_rh * dv), pl.BlockSpec((b, n_rh, dk, dv), lambda c: (0, 0, 0, 0)))
    kern = functools.partial(_ret_prompt_kernel, n_rh=n_rh, dk=dk, dv=dv, chunk_decay=cd)
    return pl.pallas_call(kern, out_shape=out_shape, grid=(t // chunk,), in_specs=in_specs,
                          out_specs=out_specs, scratch_shapes=[pltpu.VMEM((b, n_rh, dk, dv), F32)],
                          compiler_params=_cparams(1), name="retention_prompt")(
        rq, rk, rv, rgs, mask, qd, kd, gn.reshape(n_rh, 1, dv))


def _ret_sample_kernel(qt_ref, kt_ref, v_ref, gate_ref, st_ref, gn_ref, o_ref, ns_ref,
                       *, n_rh, dk, dv, gamma):
    seqs = v_ref.shape[0]
    lane = lax.broadcasted_iota(jnp.int32, qt_ref.shape, 1)
    for j in range(seqs):
        sel = lane == pl.program_id(0) * seqs + j
        qcol = jnp.sum(jnp.where(sel, qt_ref[...], 0.0), axis=1, keepdims=True)
        kcol = jnp.sum(jnp.where(sel, kt_ref[...], 0.0), axis=1, keepdims=True)
        v = v_ref[j]
        gate = gate_ref[j]
        for i in range(n_rh):
            vi = v[:, i * dv:(i + 1) * dv]
            new = st_ref[j, i] * gamma[i] + kcol[i * dk:(i + 1) * dk] * vi
            ns_ref[j, i] = new
            o = jnp.sum(qcol[i * dk:(i + 1) * dk] * new, axis=0, keepdims=True)
            o_ref[j, :, i * dv:(i + 1) * dv] = _rms(o, gn_ref[i]) * gate[:, i * dv:(i + 1) * dv]


def _ret_sample(rq, rk, rv, rgs, state, gn, n_rh, dk, dv):
    nb = rq.shape[0]
    _, _, _, _, gamma = _ret_tables(n_rh, 1, dk, dv)
    qt = rq.astype(F32).T
    kt = rk.astype(F32).T
    seqs = math.gcd(nb, RET_SAMPLE_SEQS)
    row = lambda w: pl.BlockSpec((seqs, 1, w), lambda b: (b, 0, 0))
    st_spec = pl.BlockSpec((seqs, n_rh, dk, dv), lambda b: (b, 0, 0, 0))
    in_specs = [_const_spec(qt.shape), _const_spec(kt.shape), row(n_rh * dv), row(n_rh * dv), st_spec,
                _const_spec((n_rh, 1, dv))]
    out_shape = (jax.ShapeDtypeStruct((nb, 1, n_rh * dv), F32),
                 jax.ShapeDtypeStruct((nb, n_rh, dk, dv), F32))
    kern = functools.partial(_ret_sample_kernel, n_rh=n_rh, dk=dk, dv=dv, gamma=gamma)
    o, ns = pl.pallas_call(kern, out_shape=out_shape, grid=(nb // seqs,), in_specs=in_specs,
                           out_specs=(row(n_rh * dv), st_spec), compiler_params=_cparams(1),
                           name="retention_sample")(
        qt, kt, rv.astype(F32).reshape(nb, 1, -1), rgs.astype(F32).reshape(nb, 1, -1), state,
        gn.reshape(n_rh, 1, dv))
    return o.reshape(nb, n_rh * dv).astype(BF16), ns


def _attn_prompt_kernel(q_ref, k_ref, v_ref, o_ref, m_scr, acc_scr, *, tq, hpg, sum_lane):
    i = pl.program_id(2)
    m_scr[...] = jnp.full(m_scr.shape, -jnp.inf, F32)
    acc_scr[...] = jnp.zeros(acc_scr.shape, F32)

    def tile(j, masked):
        start = pl.multiple_of(j * tq, tq)
        for h in range(hpg):
            s = _dot_nt(q_ref[h], k_ref[h, pl.ds(start, tq), :])
            if masked:
                row = lax.broadcasted_iota(jnp.int32, s.shape, 0)
                col = lax.broadcasted_iota(jnp.int32, s.shape, 1)
                s = jnp.where(col <= row, s, -jnp.inf)
            m = m_scr[h]
            m_new = jnp.maximum(m, jnp.max(s, axis=1, keepdims=True))
            m_wide = jnp.concatenate([m_new] * (tq // HEAD_PAD), axis=1)
            p = jnp.exp2((s - m_wide).astype(BF16))
            acc_scr[h] = jnp.exp2(m - m_new) * acc_scr[h] + _dot(p, v_ref[h, pl.ds(start, tq), :])
            m_scr[h] = m_new

    def body(j, carry):
        tile(j, False)
        return carry

    lax.fori_loop(0, i, body, 0)
    tile(i, True)
    for h in range(hpg):
        acc = acc_scr[h]
        o_ref[:, h * HEAD_PAD:(h + 1) * HEAD_PAD] = (acc / acc[:, sum_lane:sum_lane + 1]).astype(BF16)


def _attn_prompt(q, k, v, b, t, tq, hpg, sum_lane):
    n_mh, n, _ = q.shape
    nq = t // tq
    kv_spec = pl.BlockSpec((hpg, t, HEAD_PAD), lambda bi, g, i: (g, bi, 0), pipeline_mode=pl.Buffered(1))
    in_specs = [pl.BlockSpec((hpg, tq, HEAD_PAD), lambda bi, g, i: (g, bi * nq + i, 0)), kv_spec, kv_spec]
    out_specs = pl.BlockSpec((tq, hpg * HEAD_PAD), lambda bi, g, i: (bi * nq + i, g))
    return pl.pallas_call(functools.partial(_attn_prompt_kernel, tq=tq, hpg=hpg, sum_lane=sum_lane),
                          out_shape=jax.ShapeDtypeStruct((n, n_mh * HEAD_PAD), BF16),
                          grid=(b, n_mh // hpg, nq), in_specs=in_specs, out_specs=out_specs,
                          scratch_shapes=[pltpu.VMEM((hpg, tq, HEAD_PAD), F32),
                                          pltpu.VMEM((hpg, tq, HEAD_PAD), F32)],
                          compiler_params=_cparams(3), name="attn_prompt")(q, k, v)


def _attn_sample_kernel(pt_ref, ql_ref, qp_ref, cn_ref, rn_ref, lat_hbm, rope_hbm, o_ref,
                        lat_slab, rope_slab, sems, m_scr, l_scr, acc_scr, *, npg, n_chunks, page):
    step = pl.program_id(0)
    n_steps = pl.num_programs(0)
    n_slots = lat_slab.shape[0]
    ahead = n_slots - 1
    slot = step % n_slots
    chunk = step % n_chunks

    def page_copies(page_ids, slot_):
        out = []
        for i in range(npg):
            rows = pl.ds(i * page, page)
            out.append(pltpu.make_async_copy(lat_hbm.at[page_ids(i)], lat_slab.at[slot_, rows, :],
                                             sems.at[0, slot_]))
            out.append(pltpu.make_async_copy(rope_hbm.at[page_ids(i)], rope_slab.at[slot_, :, rows],
                                             sems.at[1, slot_]))
        return out

    def start_step(step_, slot_):
        seq = step_ // n_chunks
        first = (step_ % n_chunks) * npg
        for cp in page_copies(lambda i: pt_ref[seq, first + i], slot_):
            cp.start()

    for k in range(ahead):
        @pl.when(jnp.logical_and(step == 0, k < n_steps))
        def _():
            start_step(k, k)

    @pl.when(step + ahead < n_steps)
    def _():
        start_step(step + ahead, (step + ahead) % n_slots)

    for cp in page_copies(lambda i: 0, slot):
        cp.wait()

    ql = ql_ref[0]
    qp = qp_ref[0]

    @pl.when(chunk == 0)
    def _():
        cn = cn_ref[0]
        s0 = (jnp.sum(ql * cn, axis=1, keepdims=True)
              + jnp.sum(qp * rn_ref[0], axis=1, keepdims=True))
        m_scr[...] = s0
        l_scr[...] = jnp.ones_like(l_scr)
        acc_scr[...] = jnp.broadcast_to(cn, acc_scr.shape)

    lat = lat_slab[slot].astype(BF16)
    s = _dot_nt(ql.astype(BF16), lat) + _dot(qp, rope_slab[slot])
    m_prev = m_scr[...]
    m_new = jnp.maximum(m_prev, jnp.max(s, axis=1, keepdims=True))
    alpha = jnp.exp2(m_prev - m_new)
    p = jnp.exp2(s - m_new)
    m_scr[...] = m_new
    l_scr[...] = alpha * l_scr[...] + jnp.sum(p, axis=1, keepdims=True)
    acc_scr[...] = alpha * acc_scr[...] + _dot(p.astype(BF16), lat)

    @pl.when(chunk == n_chunks - 1)
    def _():
        o_ref[0] = acc_scr[...] / l_scr[...]


def _attn_sample(page_table, q_lat, q_pe, c_new, r_new, cache_lat, cache_rope_t, npg):
    nb, n_mh, kv_rank = q_lat.shape
    rope_dim = q_pe.shape[-1]
    n_pages = page_table.shape[1]
    page = cache_lat.shape[1]
    n_chunks = n_pages // npg
    per_seq = lambda shp: pl.BlockSpec((1,) + shp, lambda s, pt: (s // n_chunks, 0, 0))
    hbm = pl.BlockSpec(memory_space=pl.ANY)
    in_specs = [per_seq((n_mh, kv_rank)), per_seq((n_mh, rope_dim)), per_seq((1, kv_rank)),
                per_seq((1, rope_dim)), hbm, hbm]
    grid_spec = pltpu.PrefetchScalarGridSpec(
        num_scalar_prefetch=1, grid=(nb * n_chunks,), in_specs=in_specs,
        out_specs=per_seq((n_mh, kv_rank)),
        scratch_shapes=[pltpu.VMEM((DECODE_SLOTS, npg * page, kv_rank), F32),
                        pltpu.VMEM((DECODE_SLOTS, rope_dim, npg * page), F32),
                        pltpu.SemaphoreType.DMA((2, DECODE_SLOTS)),
                        pltpu.VMEM((n_mh, 1), F32), pltpu.VMEM((n_mh, 1), F32),
                        pltpu.VMEM((n_mh, kv_rank), F32)])
    kern = functools.partial(_attn_sample_kernel, npg=npg, n_chunks=n_chunks, page=page)
    return pl.pallas_call(kern, out_shape=jax.ShapeDtypeStruct((nb, n_mh, kv_rank), F32),
                          grid_spec=grid_spec, compiler_params=_cparams(1), name="attn_sample")(
        page_table, q_lat, q_pe, c_new, r_new, cache_lat, cache_rope_t)


def _headmm_kernel(a_ref, w_ref, o_ref):
    o_ref[0] = _dot(a_ref[0].astype(BF16), w_ref[0]).astype(o_ref.dtype)


def _headmm(a, w, out_dtype):
    nh, m, kk = a.shape
    nn = w.shape[-1]
    spec = lambda r, c: pl.BlockSpec((1, r, c), lambda h: (h, 0, 0))
    return pl.pallas_call(_headmm_kernel, out_shape=jax.ShapeDtypeStruct((nh, m, nn), out_dtype),
                          grid=(nh,), in_specs=[spec(m, kk), spec(kk, nn)], out_specs=spec(m, nn),
                          compiler_params=_cparams(1), name="head_matmul")(a, w)


def _router_logits(h2, whi_ref, wlo_ref):
    hi = h2.astype(BF16)
    lo = (h2 - hi.astype(F32)).astype(BF16)
    lg = _dot(hi, whi_ref[...]) + (_dot(hi, wlo_ref[...]) + _dot(lo, whi_ref[...]))
    return hi, lg[:, :LANES], lg[:, LANES:]


def _merge_kernel(x_ref, ro_ref, gr_ref, mo_ref, gm_ref, wro_ref, wmo_ref, wout_ref, gffn_ref,
                  whi_ref, wlo_ref, x1_o, gid_o, *, group_size, n_exp):
    mixed = (_dot(ro_ref[...], wro_ref[...]) * gr_ref[...].astype(F32)
             + _dot(mo_ref[...], wmo_ref[...]) * gm_ref[...].astype(F32))
    x1 = x_ref[...] + _dot(mixed.astype(BF16), wout_ref[...])
    x1_o[...] = x1
    _, ge, _ = _router_logits(_rms(x1, gffn_ref[...]), whi_ref, wlo_ref)
    lane = lax.broadcasted_iota(jnp.int32, ge.shape, 1)
    ge = jnp.where(lane < n_exp, ge, -jnp.inf)
    grp = (lane // group_size).astype(F32)
    gmax = jnp.max(ge, axis=1, keepdims=True)
    g_idx = jnp.min(jnp.where(ge == gmax, grp, float(n_exp)), axis=1, keepdims=True)
    gid_o[...] = jnp.broadcast_to(g_idx, ge.shape).astype(jnp.int32)


def _merge(x, ro, gr, mo, gm, wts, group_size, n_exp, tm):
    n, d_model = x.shape
    tok = lambda w: pl.BlockSpec((tm, w), lambda i: (i, 0))
    w_args = [wts[k] for k in ("wro", "wmo", "wout", "gffn", "wr_hi", "wr_lo")]
    in_specs = [tok(d_model), tok(ro.shape[1]), tok(d_model), tok(mo.shape[1]), tok(d_model)] + [
        _const_spec(a.shape) for a in w_args]
    out_shape = (jax.ShapeDtypeStruct((n, d_model), F32), jax.ShapeDtypeStruct((n, LANES), jnp.int32))
    kern = functools.partial(_merge_kernel, group_size=group_size, n_exp=n_exp)
    return pl.pallas_call(kern, out_shape=out_shape, grid=(n // tm,), in_specs=in_specs,
                          out_specs=(tok(d_model), tok(LANES)),
                          compiler_params=_cparams(1), name="merge_router")(x, ro, gr, mo, gm, *w_args)


def _group_tiles(gid, n_grp, tmr):
    n = gid.shape[0]
    token = jnp.arange(n, dtype=jnp.int32)
    counts = jnp.sum((gid[:, None] == jnp.arange(n_grp, dtype=jnp.int32)[None, :]).astype(jnp.int32), axis=0)
    by_group = jnp.sort(gid * n + token) % n
    dense_start = jnp.cumsum(counts) - counts
    padded = ((counts + tmr - 1) // tmr) * tmr
    ends = jnp.cumsum(padded)
    offs = ends - padded
    n_tiles = n // tmr + n_grp
    tile_start = jnp.arange(n_tiles, dtype=jnp.int32) * tmr
    tile_group = jnp.minimum(jnp.sum((tile_start[:, None] >= ends[None, :]).astype(jnp.int32), axis=1),
                             n_grp - 1)
    tile_valid = jnp.clip(counts[tile_group] - (tile_start - offs[tile_group]), 0, tmr)
    row_group = jnp.repeat(tile_group, tmr)
    in_group = jnp.arange(n_tiles * tmr, dtype=jnp.int32) - offs[row_group]
    src = jnp.clip(dense_start[row_group] + in_group, 0, n - 1)
    row_token = jnp.where(in_group < counts[row_group], by_group[src], 0)
    return row_token, tile_group, tile_valid.astype(jnp.int32)


def _moe_kernel(rt_ref, tg_ref, tv_ref, x1_hbm, gffn_ref, gfin_ref, whi_ref, wlo_ref, wg_ref, wu_ref, wd_ref,
                y_hbm, xbuf, obuf, trash, gsem, ssem, *, tmr, group_size, n_exp):
    t = pl.program_id(0)
    last = pl.num_programs(0) - 1
    slot = t % 2

    def gather_copy(tok, r, slot_):
        return pltpu.make_async_copy(x1_hbm.at[pl.ds(tok, 1)], xbuf.at[slot_, pl.ds(r, 1)], gsem.at[slot_])

    def start_gather(tile, slot_):
        for r in range(tmr):
            gather_copy(rt_ref[tile * tmr + r], r, slot_).start()

    def wait_gather(slot_):
        for r in range(tmr):
            gather_copy(0, r, slot_).wait()

    def scatter_copy(r, dst_row_ref, slot_):
        return pltpu.make_async_copy(obuf.at[slot_, pl.ds(r, 1)], dst_row_ref, ssem.at[slot_])

    def wait_scatter(slot_):
        for r in range(tmr):
            scatter_copy(r, trash.at[slot_, pl.ds(r, 1)], slot_).wait()

    @pl.when(t == 0)
    def _():
        start_gather(t, slot)

    wait_gather(slot)

    @pl.when(t >= 2)
    def _():
        wait_scatter(slot)

    start_gather(jnp.minimum(t + 1, last), 1 - slot)

    g = tg_ref[t]
    x = xbuf[slot]
    hb, ge, el = _router_logits(_rms(x, gffn_ref[...]), whi_ref, wlo_ref)
    lane_i = lax.broadcasted_iota(jnp.int32, ge.shape, 1)
    lane = lane_i.astype(F32)
    is_exp = lane_i < n_exp
    in_grp = (lane_i // group_size) == g
    gmax = jnp.max(jnp.where(is_exp, ge, -jnp.inf), axis=1, keepdims=True)
    gsum = jnp.sum(jnp.where(is_exp, jnp.exp(ge - gmax), 0.0), axis=1, keepdims=True) / group_size
    ge_g = jnp.max(jnp.where(in_grp, ge, -jnp.inf), axis=1, keepdims=True)
    g_w = jnp.exp(ge_g - gmax) / gsum
    e_in = jnp.where(in_grp, el, -jnp.inf)
    top1 = jnp.max(e_in, axis=1, keepdims=True)
    idx1 = jnp.min(jnp.where(e_in == top1, lane, float(LANES)), axis=1, keepdims=True)
    e_rest = jnp.where(lane == idx1, -jnp.inf, e_in)
    top2 = jnp.max(e_rest, axis=1, keepdims=True)
    idx2 = jnp.min(jnp.where(e_rest == top2, lane, float(LANES)), axis=1, keepdims=True)
    e2 = jnp.exp(top2 - top1)
    comb = (jnp.where(lane == idx1, g_w / (1.0 + e2), 0.0)
            + jnp.where(lane == idx2, g_w * e2 / (1.0 + e2), 0.0))

    first = (g * group_size).astype(F32)
    parts = []
    for e in range(group_size):
        ce = jnp.sum(jnp.where(lane == first + e, comb, 0.0), axis=1, keepdims=True)
        hg = _dot(hb, wg_ref[e])
        parts.append((hg * _sigmoid(hg) * _dot(hb, wu_ref[e]) * ce).astype(BF16))
    y = x + _dot(jnp.concatenate(parts, axis=1), wd_ref[0])
    obuf[slot] = _rms(y, gfin_ref[...])

    n_real = tv_ref[t]

    def scatter_real(r, carry):
        tok = rt_ref[t * tmr + r]
        scatter_copy(r, y_hbm.at[pl.ds(tok, 1)], slot).start()
        return carry

    def scatter_pad(r, carry):
        scatter_copy(r, trash.at[slot, pl.ds(r, 1)], slot).start()
        return carry

    @pl.when(n_real == tmr)
    def _():
        for r in range(tmr):
            scatter_real(r, 0)

    @pl.when(n_real < tmr)
    def _():
        lax.fori_loop(0, n_real, scatter_real, 0)
        lax.fori_loop(n_real, tmr, scatter_pad, 0)

    @pl.when(t == last)
    def _():
        wait_gather(1 - slot)
        wait_scatter(slot)

        @pl.when(t >= 1)
        def _():
            wait_scatter(1 - slot)


def _moe(x1, gid, wts, wg, wu, wd, gfin, group_size, n_exp, tmr):
    n, d_model = x1.shape
    n_grp = wd.shape[0]
    row_token, tile_group, tile_valid = _group_tiles(gid, n_grp, tmr)
    n_tiles = tile_group.shape[0]
    hbm = pl.BlockSpec(memory_space=pl.ANY)
    const = lambda a: pl.BlockSpec(a.shape, lambda t, rt, tg, tv: (0,) * a.ndim, pipeline_mode=pl.Buffered(1))
    grp_w = lambda a: pl.BlockSpec((a.shape[0] // n_grp,) + a.shape[1:], lambda t, rt, tg, tv: (tg[t], 0, 0))
    w_args = [wts["gffn"], gfin, wts["wr_hi"], wts["wr_lo"]]
    grid_spec = pltpu.PrefetchScalarGridSpec(
        num_scalar_prefetch=3, grid=(n_tiles,),
        in_specs=[hbm] + [const(a) for a in w_args] + [grp_w(wg), grp_w(wu), grp_w(wd)],
        out_specs=hbm,
        scratch_shapes=[pltpu.VMEM((2, tmr, d_model), F32), pltpu.VMEM((2, tmr, d_model), F32),
                        pltpu.VMEM((2, tmr, d_model), F32),
                        pltpu.SemaphoreType.DMA((2,)), pltpu.SemaphoreType.DMA((2,))])
    kern = functools.partial(_moe_kernel, tmr=tmr, group_size=group_size, n_exp=n_exp)
    return pl.pallas_call(kern, out_shape=jax.ShapeDtypeStruct((n, d_model), F32), grid_spec=grid_spec,
                          compiler_params=_cparams(1), name="moe")(
        row_token, tile_group, tile_valid, x1, *w_args, wg, wu, wd)


def _rope_tables(start, count, repeat, dk, rope_dim, nope):
    n_hi = -(-count // LANES)

    def cs(half):
        inv = ROPE_BASE ** (-jnp.arange(half, dtype=F32) / half)
        a = (start + LANES * jnp.arange(n_hi, dtype=F32))[:, None, None] * inv
        b = jnp.arange(LANES, dtype=F32)[None, :, None] * inv
        ca, sa, cb, sb = jnp.cos(a), jnp.sin(a), jnp.cos(b), jnp.sin(b)
        c = (ca * cb - sa * sb).reshape(n_hi * LANES, half)[:count]
        s = (sa * cb + ca * sb).reshape(n_hi * LANES, half)[:count]
        return jnp.repeat(c, repeat, axis=0), jnp.repeat(s, repeat, axis=0)

    c, s = cs(dk // 2)
    cosr = jnp.concatenate([c, c], axis=1)
    sinr = jnp.concatenate([-s, s], axis=1)
    c, s = cs(rope_dim // 2)
    n = count * repeat
    tail = jnp.zeros((n, HEAD_PAD - nope - rope_dim), F32)
    cp = jnp.concatenate([jnp.ones((n, nope), F32), c, c, tail], axis=1)
    sp = jnp.concatenate([jnp.zeros((n, nope), F32), -s, s, tail], axis=1)
    return cosr, sinr, cp, sp


def _swap_halves(w):
    half = w.shape[-1] // 2
    return jnp.concatenate([w[..., half:], w[..., :half]], axis=-1)


def _layout_weights(norm_mix_g, w_in, q_norm_g, kv_norm_g, w_uq, w_ukv, w_ret_o, w_mla_o, w_out,
                    norm_ffn_g, w_router_group, w_router_expert, dims):
    n_rh, dk, dv, q_rank, kv_rank, rope_dim, n_mh, nope, dv_m = dims
    d_model = w_in.shape[0]
    qk_w, v_w = n_rh * dk, n_rh * dv
    o_cq = 2 * qk_w + 2 * v_w
    o_ckv = o_cq + q_rank
    o_kr = o_ckv + kv_rank
    o_gate = o_kr + rope_dim
    pad_tail = HEAD_PAD - nope - rope_dim
    zeros = lambda *shape: jnp.zeros(shape, F32)
    w_kr = w_in[:, o_kr:o_gate]
    place = lambda w: jnp.concatenate([zeros(d_model, nope), w, zeros(d_model, pad_tail)], axis=1)
    wsmall = jnp.concatenate([w_in[:, o_cq:o_kr], place(w_kr), place(_swap_halves(w_kr))], axis=1)
    wmain = jnp.concatenate([w_in[:, :o_cq], w_in[:, o_gate:]], axis=1)

    uq = w_uq.reshape(q_rank, n_mh, nope + rope_dim)
    uq_rope = uq[..., nope:]
    wq = jnp.concatenate([uq, zeros(q_rank, n_mh, pad_tail)], axis=-1)
    wqsw = jnp.concatenate([zeros(q_rank, n_mh, nope), _swap_halves(uq_rope),
                            zeros(q_rank, n_mh, pad_tail)], axis=-1)
    uk = w_ukv[..., :nope]
    uv = w_ukv[..., nope:]
    wk = jnp.concatenate([uk, zeros(kv_rank, n_mh, HEAD_PAD - nope)], axis=-1)
    wv = jnp.concatenate([uv, zeros(kv_rank, n_mh, HEAD_PAD - dv_m)], axis=-1)
    flat = lambda w: w.reshape(w.shape[0], n_mh * HEAD_PAD).astype(BF16)
    wmo = jnp.concatenate([w_mla_o.reshape(n_mh, dv_m, d_model),
                           zeros(n_mh, HEAD_PAD - dv_m, d_model)], axis=1).reshape(n_mh * HEAD_PAD, d_model)
    wabs = jnp.concatenate([jnp.transpose(uk, (1, 2, 0)), zeros(n_mh, HEAD_PAD - nope, kv_rank)], axis=1)
    wvh = jnp.transpose(wv, (1, 0, 2))
    n_grp = w_router_group.shape[1]
    n_exp = w_router_expert.shape[1]
    group_size = n_exp // n_grp
    assert n_exp <= LANES
    w_r = jnp.concatenate([jnp.repeat(w_router_group, group_size, axis=1), zeros(d_model, LANES - n_exp),
                           w_router_expert, zeros(d_model, LANES - n_exp)], axis=1)
    wr_hi = w_r.astype(BF16)
    wr_lo = (w_r - wr_hi.astype(F32)).astype(BF16)
    return dict(
        gmix=norm_mix_g.reshape(1, -1), wmain=wmain.astype(BF16), wsmall=wsmall.astype(BF16),
        gq=q_norm_g.reshape(1, -1), gkv=kv_norm_g.reshape(1, -1),
        wq=flat(wq), wqsw=flat(wqsw), wk=flat(wk), wv=flat(wv),
        wro=w_ret_o.astype(BF16), wmo=wmo.astype(BF16), wout=w_out.astype(BF16),
        gffn=norm_ffn_g.reshape(1, -1), wr_hi=wr_hi, wr_lo=wr_lo,
        wabs=wabs.astype(BF16), wvh=wvh.astype(BF16)), group_size, n_exp


def kernel(x_prompt, x_sample, cache_kv_latent, cache_k_rope, state_retention, page_table, norm_mix_g,
           w_in, q_norm_g, kv_norm_g, w_uq, w_ukv, ret_gn_g, w_ret_o, w_mla_o, w_out, norm_ffn_g,
           w_router_group, w_router_expert, w_gate, w_up, w_down, norm_final_g):
    depth = w_in.shape[0]
    assert depth == 1, "single-layer step only"
    b, t, d_model = x_prompt.shape
    bd, tn, _ = x_sample.shape
    assert tn == 1, "one new token per sample sequence"
    _, _, n_rh, dk, dv = state_retention.shape
    kv_rank = cache_kv_latent.shape[-1]
    rope_dim = cache_k_rope.shape[-1]
    page = cache_kv_latent.shape[2]
    n_mh = w_ukv.shape[2]
    q_rank = w_uq.shape[1]
    nope = w_uq.shape[2] // n_mh - rope_dim
    dv_m = w_ukv.shape[3] - nope
    assert nope + rope_dim <= HEAD_PAD and dv_m < HEAD_PAD
    dims = (n_rh, dk, dv, q_rank, kv_rank, rope_dim, n_mh, nope, dv_m)
    past_len = page_table.shape[1] * page

    wts, group_size, n_exp = _layout_weights(norm_mix_g[0], w_in[0], q_norm_g[0], kv_norm_g[0], w_uq[0],
                                             w_ukv[0], w_ret_o[0], w_mla_o[0], w_out[0], norm_ffn_g[0],
                                             w_router_group[0], w_router_expert[0], dims)
    gn = ret_gn_g[0]
    n_grp = n_exp // group_size
    d_exp = w_gate.shape[-1]
    wg, wu = w_gate[0].astype(BF16), w_up[0].astype(BF16)
    wd = w_down[0].astype(BF16).reshape(n_grp, group_size * d_exp, d_model)
    gfin = norm_final_g.reshape(1, -1)

    tm_p = min(256, t)
    xp = x_prompt.reshape(b * t, d_model)
    tabs_p = _rope_tables(0.0, t, 1, dk, rope_dim, nope)
    (rq, rk, rv, rgs, gret, gmla, lat_p, rope_p, q_p, k_p, v_p) = _inproj(xp, tabs_p, wts, dims, tm_p)
    ret_o, st_p = _ret_prompt(rq.reshape(b, t, -1), rk.reshape(b, t, -1), rv.reshape(b, t, -1),
                              rgs.reshape(b, t, -1), gn, n_rh, dk, dv)
    mla_o = _attn_prompt(q_p, k_p, v_p, b, t, min(ATTN_TQ, t), min(ATTN_HEADS_PER_STEP, n_mh), dv_m)
    x1, gid = _merge(xp, ret_o.reshape(b * t, -1), gret, mla_o, gmla, wts, group_size, n_exp,
                     min(MERGE_ROW_TILE, t))
    y_prompt = _moe(x1, gid[:, 0], wts, wg, wu, wd, gfin, group_size, n_exp,
                    min(MOE_ROW_TILE, b * t)).reshape(b, t, d_model)

    xs = x_sample.reshape(bd, d_model)
    tabs_s = _rope_tables(float(past_len), tn, bd, dk, rope_dim, nope)
    (rq, rk, rv, rgs, gret, gmla, lat_s, rope_s, q_s, _, _) = _inproj(xs, tabs_s, wts, dims, bd)
    ret_o_s, st_s = _ret_sample(rq, rk, rv, rgs, state_retention[0], gn, n_rh, dk, dv)
    q_lat = jnp.transpose(_headmm(q_s, wts["wabs"], F32), (1, 0, 2))
    q_pe = jnp.transpose(q_s[:, :, nope:nope + rope_dim].astype(F32), (1, 0, 2))
    o_lat = _attn_sample(page_table, q_lat, q_pe, lat_s.reshape(bd, 1, kv_rank),
                         rope_s.reshape(bd, 1, rope_dim), cache_kv_latent.reshape(-1, page, kv_rank),
                         jnp.transpose(cache_k_rope.reshape(-1, page, rope_dim), (0, 2, 1)),
                         min(DECODE_PAGES, page_table.shape[1]))
    mla_o_s = _headmm(jnp.transpose(o_lat, (1, 0, 2)), wts["wvh"], BF16)
    mla_o_s = jnp.transpose(mla_o_s, (1, 0, 2)).reshape(bd, n_mh * HEAD_PAD)
    x1, gid = _merge(xs, ret_o_s, gret, mla_o_s, gmla, wts, group_size, n_exp, bd)
    y_sample = _moe(x1, gid[:, 0], wts, wg, wu, wd, gfin, group_size, n_exp,
                    min(MOE_ROW_TILE, bd)).reshape(bd, tn, d_model)

    return (y_prompt, y_sample,
            lat_p.reshape(depth, b, t // page, page, kv_rank),
            rope_p.reshape(depth, b, t // page, page, rope_dim),
            st_p.reshape(depth, b, n_rh, dk, dv),
            lat_s.reshape(depth, bd, tn, kv_rank),
            rope_s.reshape(depth, bd, tn, rope_dim),
            st_s.reshape(depth, bd, n_rh, dk, dv))
```

```python
import functools
import math

import numpy as np
import jax
import jax.numpy as jnp
from jax import lax
from jax.experimental import pallas as pl
from jax.experimental.pallas import tpu as pltpu

F32 = jnp.float32
BF16 = jnp.bfloat16

ROPE_BASE = 10000.0
EPS = 1e-6
RET_CHUNK = 128
LANES = 128
HEAD_PAD = LANES
MERGE_ROW_TILE = 512
MOE_ROW_TILE = 256
RET_SAMPLE_SEQS = 4
DECODE_PAGES = 64
DECODE_SLOTS = 3
ATTN_TQ = 512
ATTN_HEADS_PER_STEP = 8
VMEM_LIMIT = 48 * 1024 * 1024


def _cparams(n_axes):
    return pltpu.CompilerParams(dimension_semantics=("arbitrary",) * n_axes,
                                vmem_limit_bytes=VMEM_LIMIT)


def _const_spec(shape):
    nd = len(shape)
    return pl.BlockSpec(shape, lambda *_: (0,) * nd, pipeline_mode=pl.Buffered(1))


def _rms(x, g):
    return x * lax.rsqrt(jnp.mean(x * x, axis=-1, keepdims=True) + EPS) * g


def _sigmoid(x):
    return 1.0 / (1.0 + jnp.exp(-x))


def _dot(a, b):
    return jnp.dot(a, b, preferred_element_type=F32)


def _dot_nt(a, b):
    return lax.dot_general(a, b, (((1,), (1,)), ((), ())), preferred_element_type=F32)


def _retention_chunk(q, k, v, gate, state, i, mask_ref, qd_ref, kd_ref, gn_ref, chunk_decay):
    s = _dot_nt(q, k) * mask_ref[i]
    o = _dot(s.astype(BF16), v) + _dot(q, state.astype(BF16)) * qd_ref[i]
    kd = (k.astype(F32) * kd_ref[i]).T.astype(BF16)
    new_state = state * chunk_decay[i] + _dot(kd, v)
    return (_rms(o, gn_ref[i]) * gate.astype(F32)).astype(BF16), new_state


def _inproj_kernel(x_ref, gmix_ref, wmain_ref, wsmall_ref, gq_ref, gkv_ref, wq_ref, wqsw_ref,
                   wk_ref, wv_ref, cosr_ref, sinr_ref, cp_ref, sp_ref, *rest,
                   n_rh, dk, dv, q_rank, kv_rank, rope_dim, n_mh, nope, dv_m, k_scale, q_scale,
                   ret_chunk, blocks_per_seq, chunk_decay):
    if ret_chunk:
        (mask_ref, qd_ref, kd_ref, gn_ref, reto_o, st_o,
         gret_o, gmla_o, lat_o, rope_o, q_o, k_o, v_o, s_scr) = rest
    else:
        rq_o, rk_o, rv_o, rgs_o, gret_o, gmla_o, lat_o, rope_o, q_o, k_o, v_o = rest
    x = x_ref[...]
    h = _rms(x, gmix_ref[...]).astype(BF16)
    cosr = cosr_ref[...]
    sinr = sinr_ref[...]
    qk_w = n_rh * dk
    v_w = n_rh * dv

    def rope_heads(z, scale):
        outs = []
        for i in range(n_rh):
            seg = z[:, i * dk:(i + 1) * dk]
            rot = seg * cosr + pltpu.roll(seg, dk // 2, 1) * sinr
            outs.append(rot if scale is None else rot * scale)
        return jnp.concatenate(outs, axis=1)

    off = 0
    rq = rope_heads(_dot(h, wmain_ref[:, off:off + qk_w]), None).astype(BF16)
    off += qk_w
    rk = rope_heads(_dot(h, wmain_ref[:, off:off + qk_w]), k_scale).astype(BF16)
    off += qk_w
    rv = _dot(h, wmain_ref[:, off:off + v_w]).astype(BF16)
    off += v_w
    rg = _dot(h, wmain_ref[:, off:off + v_w])
    rgs = (rg * _sigmoid(rg)).astype(BF16)
    off += v_w
    if ret_chunk:
        blk = pl.program_id(0) % blocks_per_seq

        @pl.when(blk == 0)
        def _():
            s_scr[...] = jnp.zeros_like(s_scr)

        for c in range(x.shape[0] // ret_chunk):
            rows = slice(c * ret_chunk, (c + 1) * ret_chunk)
            for i in range(n_rh):
                qs, vs = slice(i * dk, (i + 1) * dk), slice(i * dv, (i + 1) * dv)
                o, s_scr[i] = _retention_chunk(rq[rows, qs], rk[rows, qs], rv[rows, vs], rgs[rows, vs],
                                               s_scr[i], i, mask_ref, qd_ref, kd_ref, gn_ref, chunk_decay)
                reto_o[rows, vs] = o

        @pl.when(blk == blocks_per_seq - 1)
        def _():
            st_o[0] = s_scr[...]
    else:
        rq_o[...] = rq
        rk_o[...] = rk
        rv_o[...] = rv
        rgs_o[...] = rgs
    d_model = x.shape[1]
    gret_o[...] = _sigmoid(_dot(h, wmain_ref[:, off:off + d_model])).astype(BF16)
    off += d_model
    gmla_o[...] = _sigmoid(_dot(h, wmain_ref[:, off:off + d_model])).astype(BF16)

    zs = _dot(h, wsmall_ref[...])
    cq = zs[:, :q_rank]
    ckv = zs[:, q_rank:q_rank + kv_rank]
    krp = zs[:, q_rank + kv_rank:q_rank + kv_rank + HEAD_PAD]
    krsw = zs[:, q_rank + kv_rank + HEAD_PAD:]
    cp = cp_ref[...]
    sp = sp_ref[...]
    cqn = _rms(cq, gq_ref[...]).astype(BF16)
    ckvn = _rms(ckv, gkv_ref[...])
    lat_o[...] = ckvn
    ckvb = ckvn.astype(BF16)
    kr_rot = krp * cp + krsw * sp
    rope_o[...] = pltpu.roll(kr_rot, HEAD_PAD - nope, 1)[:, :rope_dim]

    qh = _dot(cqn, wq_ref[...])
    qs = _dot(cqn, wqsw_ref[...])
    kh = _dot(ckvb, wk_ref[...])
    vh = _dot(ckvb, wv_ref[...])
    sum_lane = lax.broadcasted_iota(jnp.int32, cp.shape, 1) == dv_m
    for i in range(n_mh):
        sl = slice(i * HEAD_PAD, (i + 1) * HEAD_PAD)
        q_o[i] = ((qh[:, sl] * cp + qs[:, sl] * sp) * q_scale).astype(BF16)
        k_o[i] = (kh[:, sl] + kr_rot).astype(BF16)
        v_o[i] = jnp.where(sum_lane, 1.0, vh[:, sl]).astype(BF16)


def _inproj(x, tables, wts, dims, tm, ret_gn=None, n_seq=1):
    n, d_model = x.shape
    cosr, sinr, cp, sp = tables
    tab_blocks = cosr.shape[0] // tm
    n_rh, dk, dv, q_rank, kv_rank, rope_dim, n_mh, nope, dv_m = dims
    grid = (n // tm,)
    tok = lambda w: pl.BlockSpec((tm, w), lambda i: (i, 0))
    tab = lambda w: pl.BlockSpec((tm, w), lambda i: (i % tab_blocks, 0))
    head = pl.BlockSpec((n_mh, tm, HEAD_PAD), lambda i: (0, i, 0))
    w_names = ("gmix", "wmain", "wsmall", "gq", "gkv", "wq", "wqsw", "wk", "wv")
    args = [x] + [wts[k] for k in w_names] + [cosr, sinr, cp, sp]
    in_specs = ([tok(d_model)] + [_const_spec(wts[k].shape) for k in w_names]
                + [tab(dk), tab(dk), tab(HEAD_PAD), tab(HEAD_PAD)])
    qk_w, v_w = n_rh * dk, n_rh * dv
    sds = jax.ShapeDtypeStruct
    common_shape = (sds((n, d_model), BF16), sds((n, d_model), BF16), sds((n, kv_rank), F32),
                    sds((n, rope_dim), F32), sds((n_mh, n, HEAD_PAD), BF16), sds((n_mh, n, HEAD_PAD), BF16),
                    sds((n_mh, n, HEAD_PAD), BF16))
    common_specs = (tok(d_model), tok(d_model), tok(kv_rank), tok(rope_dim), head, head, head)
    scratch = []
    if ret_gn is None:
        ret_chunk, blocks_per_seq, cd = 0, 1, None
        out_shape = (sds((n, qk_w), BF16), sds((n, qk_w), BF16), sds((n, v_w), BF16),
                     sds((n, v_w), BF16)) + common_shape
        out_specs = (tok(qk_w), tok(qk_w), tok(v_w), tok(v_w)) + common_specs
    else:
        ret_chunk = min(RET_CHUNK, tm)
        blocks_per_seq = n // n_seq // tm
        mask, qd, kd, cd, _ = _ret_tables(n_rh, ret_chunk, dk, dv)
        tabs = [mask, qd, kd, ret_gn.reshape(n_rh, 1, dv)]
        args += tabs
        in_specs += [_const_spec(a.shape) for a in tabs]
        out_shape = (sds((n, v_w), BF16), sds((n_seq, n_rh, dk, dv), F32)) + common_shape
        out_specs = (tok(v_w), pl.BlockSpec((1, n_rh, dk, dv), lambda i: (i // blocks_per_seq, 0, 0, 0))
                     ) + common_specs
        scratch = [pltpu.VMEM((n_rh, dk, dv), F32)]
    kern = functools.partial(_inproj_kernel, n_rh=n_rh, dk=dk, dv=dv, q_rank=q_rank, kv_rank=kv_rank,
                             rope_dim=rope_dim, n_mh=n_mh, nope=nope, dv_m=dv_m, k_scale=dk ** -0.5,
                             q_scale=(nope + rope_dim) ** -0.5 * math.log2(math.e),
                             ret_chunk=ret_chunk, blocks_per_seq=blocks_per_seq, chunk_decay=cd)
    return pl.pallas_call(kern, out_shape=out_shape, grid=grid, in_specs=in_specs, out_specs=out_specs,
                          scratch_shapes=scratch, compiler_params=_cparams(1), name="inproj")(*args)


def _ret_tables(n_rh, chunk, dk, dv):
    log_g = np.log1p(-np.exp2(-5.0 - np.arange(n_rh, dtype=np.float64)))
    idx = np.arange(chunk, dtype=np.float64)
    diff = idx[:, None] - idx[None, :]
    mask = np.where(diff >= 0, np.exp(log_g[:, None, None] * np.maximum(diff, 0.0)), 0.0)
    qd = np.exp(log_g[:, None] * (idx + 1.0))
    kd = np.exp(log_g[:, None] * (chunk - 1.0 - idx))
    cd = np.exp(log_g * chunk)
    qd = np.broadcast_to(qd[:, :, None], (n_rh, chunk, dv))
    kd = np.broadcast_to(kd[:, :, None], (n_rh, chunk, dk))
    return (jnp.asarray(mask, F32), jnp.asarray(qd, F32), jnp.asarray(kd, F32),
            tuple(float(np.float32(v)) for v in cd), tuple(float(np.float32(v)) for v in np.exp(log_g)))


def _ret_sample_kernel(qt_ref, kt_ref, v_ref, gate_ref, st_ref, gn_ref, o_ref, ns_ref,
                       *, n_rh, dk, dv, gamma):
    seqs = v_ref.shape[0]
    lane = lax.broadcasted_iota(jnp.int32, qt_ref.shape, 1)
    for j in range(seqs):
        sel = lane == pl.program_id(0) * seqs + j
        qcol = jnp.sum(jnp.where(sel, qt_ref[...], 0.0), axis=1, keepdims=True)
        kcol = jnp.sum(jnp.where(sel, kt_ref[...], 0.0), axis=1, keepdims=True)
        v = v_ref[j]
        gate = gate_ref[j]
        for i in range(n_rh):
            vi = v[:, i * dv:(i + 1) * dv]
            new = st_ref[j, i] * gamma[i] + kcol[i * dk:(i + 1) * dk] * vi
            ns_ref[j, i] = new
            o = jnp.sum(qcol[i * dk:(i + 1) * dk] * new, axis=0, keepdims=True)
            o_ref[j, :, i * dv:(i + 1) * dv] = _rms(o, gn_ref[i]) * gate[:, i * dv:(i + 1) * dv]


def _ret_sample(rq, rk, rv, rgs, state, gn, n_rh, dk, dv):
    nb = rq.shape[0]
    _, _, _, _, gamma = _ret_tables(n_rh, 1, dk, dv)
    qt = rq.astype(F32).T
    kt = rk.astype(F32).T
    seqs = math.gcd(nb, RET_SAMPLE_SEQS)
    row = lambda w: pl.BlockSpec((seqs, 1, w), lambda b: (b, 0, 0))
    st_spec = pl.BlockSpec((seqs, n_rh, dk, dv), lambda b: (b, 0, 0, 0))
    in_specs = [_const_spec(qt.shape), _const_spec(kt.shape), row(n_rh * dv), row(n_rh * dv), st_spec,
                _const_spec((n_rh, 1, dv))]
    out_shape = (jax.ShapeDtypeStruct((nb, 1, n_rh * dv), F32),
                 jax.ShapeDtypeStruct((nb, n_rh, dk, dv), F32))
    kern = functools.partial(_ret_sample_kernel, n_rh=n_rh, dk=dk, dv=dv, gamma=gamma)
    o, ns = pl.pallas_call(kern, out_shape=out_shape, grid=(nb // seqs,), in_specs=in_specs,
                           out_specs=(row(n_rh * dv), st_spec), compiler_params=_cparams(1),
                           name="retention_sample")(
        qt, kt, rv.astype(F32).reshape(nb, 1, -1), rgs.astype(F32).reshape(nb, 1, -1), state,
        gn.reshape(n_rh, 1, dv))
    return o.reshape(nb, n_rh * dv).astype(BF16), ns


def _attn_prompt_kernel(q_ref, k_ref, v_ref, o_ref, m_scr, acc_scr, *, tq, hpg, sum_lane):
    i = pl.program_id(2)
    m_scr[...] = jnp.full(m_scr.shape, -jnp.inf, F32)
    acc_scr[...] = jnp.zeros(acc_scr.shape, F32)

    def tile(j, masked):
        start = pl.multiple_of(j * tq, tq)
        for h in range(hpg):
            s = _dot_nt(q_ref[h], k_ref[h, pl.ds(start, tq), :])
            if masked:
                row = lax.broadcasted_iota(jnp.int32, s.shape, 0)
                col = lax.broadcasted_iota(jnp.int32, s.shape, 1)
                s = jnp.where(col <= row, s, -jnp.inf)
            m = m_scr[h]
            m_new = jnp.maximum(m, jnp.max(s, axis=1, keepdims=True))
            m_wide = jnp.concatenate([m_new] * (tq // HEAD_PAD), axis=1)
            p = jnp.exp2((s - m_wide).astype(BF16))
            acc_scr[h] = jnp.exp2(m - m_new) * acc_scr[h] + _dot(p, v_ref[h, pl.ds(start, tq), :])
            m_scr[h] = m_new

    def body(j, carry):
        tile(j, False)
        return carry

    lax.fori_loop(0, i, body, 0)
    tile(i, True)
    for h in range(hpg):
        acc = acc_scr[h]
        o = acc / acc[:, sum_lane:sum_lane + 1]
        o_ref[:, h * sum_lane:(h + 1) * sum_lane] = o[:, :sum_lane].astype(BF16)


def _attn_prompt(q, k, v, b, t, tq, hpg, sum_lane):
    n_mh, n, _ = q.shape
    nq = t // tq
    kv_spec = pl.BlockSpec((hpg, t, HEAD_PAD), lambda bi, g, i: (g, bi, 0), pipeline_mode=pl.Buffered(1))
    in_specs = [pl.BlockSpec((hpg, tq, HEAD_PAD), lambda bi, g, i: (g, bi * nq + i, 0)), kv_spec, kv_spec]
    out_specs = pl.BlockSpec((tq, hpg * sum_lane), lambda bi, g, i: (bi * nq + i, g))
    return pl.pallas_call(functools.partial(_attn_prompt_kernel, tq=tq, hpg=hpg, sum_lane=sum_lane),
                          out_shape=jax.ShapeDtypeStruct((n, n_mh * sum_lane), BF16),
                          grid=(b, n_mh // hpg, nq), in_specs=in_specs, out_specs=out_specs,
                          scratch_shapes=[pltpu.VMEM((hpg, tq, HEAD_PAD), F32),
                                          pltpu.VMEM((hpg, tq, HEAD_PAD), F32)],
                          compiler_params=_cparams(3), name="attn_prompt")(q, k, v)


def _attn_sample_kernel(pt_ref, ql_ref, qp_ref, cn_ref, rn_ref, lat_hbm, rope_hbm, o_ref,
                        lat_slab, rope_slab, sems, m_scr, l_scr, acc_scr, *, npg, n_chunks, page):
    step = pl.program_id(0)
    n_steps = pl.num_programs(0)
    n_slots = lat_slab.shape[0]
    ahead = n_slots - 1
    slot = step % n_slots
    chunk = step % n_chunks

    def page_copies(page_ids, slot_):
        out = []
        for i in range(npg):
            rows = pl.ds(i * page, page)
            out.append(pltpu.make_async_copy(lat_hbm.at[page_ids(i)], lat_slab.at[slot_, rows, :],
                                             sems.at[0, slot_]))
            out.append(pltpu.make_async_copy(rope_hbm.at[page_ids(i)], rope_slab.at[slot_, :, rows],
                                             sems.at[1, slot_]))
        return out

    def start_step(step_, slot_):
        seq = step_ // n_chunks
        first = (step_ % n_chunks) * npg
        for cp in page_copies(lambda i: pt_ref[seq, first + i], slot_):
            cp.start()

    for k in range(ahead):
        @pl.when(jnp.logical_and(step == 0, k < n_steps))
        def _():
            start_step(k, k)

    @pl.when(step + ahead < n_steps)
    def _():
        start_step(step + ahead, (step + ahead) % n_slots)

    for cp in page_copies(lambda i: 0, slot):
        cp.wait()

    ql = ql_ref[0]
    qp = qp_ref[0]

    @pl.when(chunk == 0)
    def _():
        cn = cn_ref[0]
        s0 = (jnp.sum(ql * cn, axis=1, keepdims=True)
              + jnp.sum(qp * rn_ref[0], axis=1, keepdims=True))
        m_scr[...] = s0
        l_scr[...] = jnp.ones_like(l_scr)
        acc_scr[...] = jnp.broadcast_to(cn, acc_scr.shape)

    lat = lat_slab[slot].astype(BF16)
    s = _dot_nt(ql.astype(BF16), lat) + _dot(qp, rope_slab[slot])
    m_prev = m_scr[...]
    m_new = jnp.maximum(m_prev, jnp.max(s, axis=1, keepdims=True))
    alpha = jnp.exp2(m_prev - m_new)
    p = jnp.exp2(s - m_new)
    m_scr[...] = m_new
    l_scr[...] = alpha * l_scr[...] + jnp.sum(p, axis=1, keepdims=True)
    acc_scr[...] = alpha * acc_scr[...] + _dot(p.astype(BF16), lat)

    @pl.when(chunk == n_chunks - 1)
    def _():
        o_ref[0] = acc_scr[...] / l_scr[...]


def _attn_sample(page_table, q_lat, q_pe, c_new, r_new, cache_lat, cache_rope_t, npg):
    nb, n_mh, kv_rank = q_lat.shape
    rope_dim = q_pe.shape[-1]
    n_pages = page_table.shape[1]
    page = cache_lat.shape[1]
    n_chunks = n_pages // npg
    per_seq = lambda shp: pl.BlockSpec((1,) + shp, lambda s, pt: (s // n_chunks, 0, 0))
    hbm = pl.BlockSpec(memory_space=pl.ANY)
    in_specs = [per_seq((n_mh, kv_rank)), per_seq((n_mh, rope_dim)), per_seq((1, kv_rank)),
                per_seq((1, rope_dim)), hbm, hbm]
    grid_spec = pltpu.PrefetchScalarGridSpec(
        num_scalar_prefetch=1, grid=(nb * n_chunks,), in_specs=in_specs,
        out_specs=per_seq((n_mh, kv_rank)),
        scratch_shapes=[pltpu.VMEM((DECODE_SLOTS, npg * page, kv_rank), F32),
                        pltpu.VMEM((DECODE_SLOTS, rope_dim, npg * page), F32),
                        pltpu.SemaphoreType.DMA((2, DECODE_SLOTS)),
                        pltpu.VMEM((n_mh, 1), F32), pltpu.VMEM((n_mh, 1), F32),
                        pltpu.VMEM((n_mh, kv_rank), F32)])
    kern = functools.partial(_attn_sample_kernel, npg=npg, n_chunks=n_chunks, page=page)
    return pl.pallas_call(kern, out_shape=jax.ShapeDtypeStruct((nb, n_mh, kv_rank), F32),
                          grid_spec=grid_spec, compiler_params=_cparams(1), name="attn_sample")(
        page_table, q_lat, q_pe, c_new, r_new, cache_lat, cache_rope_t)


def _headmm_kernel(a_ref, w_ref, o_ref):
    o_ref[0] = _dot(a_ref[0].astype(BF16), w_ref[0]).astype(o_ref.dtype)


def _headmm(a, w, out_dtype):
    nh, m, kk = a.shape
    nn = w.shape[-1]
    spec = lambda r, c: pl.BlockSpec((1, r, c), lambda h: (h, 0, 0))
    return pl.pallas_call(_headmm_kernel, out_shape=jax.ShapeDtypeStruct((nh, m, nn), out_dtype),
                          grid=(nh,), in_specs=[spec(m, kk), spec(kk, nn)], out_specs=spec(m, nn),
                          compiler_params=_cparams(1), name="head_matmul")(a, w)


def _router_logits(h2, whi_ref, wlo_ref):
    hi = h2.astype(BF16)
    lo = (h2 - hi.astype(F32)).astype(BF16)
    lg = _dot(hi, whi_ref[...]) + (_dot(hi, wlo_ref[...]) + _dot(lo, whi_ref[...]))
    return hi, lg[:, :LANES], lg[:, LANES:]


def _merge_kernel(x_ref, ro_ref, gr_ref, mo_ref, gm_ref, wro_ref, wmo_ref, wout_ref, gffn_ref,
                  whi_ref, wlo_ref, x1_o, gid_o, *, group_size, n_exp):
    mixed = (_dot(ro_ref[...], wro_ref[...]) * gr_ref[...].astype(F32)
             + _dot(mo_ref[...], wmo_ref[...]) * gm_ref[...].astype(F32))
    x1 = x_ref[...] + _dot(mixed.astype(BF16), wout_ref[...])
    x1_o[...] = x1
    _, ge, _ = _router_logits(_rms(x1, gffn_ref[...]), whi_ref, wlo_ref)
    lane = lax.broadcasted_iota(jnp.int32, ge.shape, 1)
    ge = jnp.where(lane < n_exp, ge, -jnp.inf)
    grp = (lane // group_size).astype(F32)
    gmax = jnp.max(ge, axis=1, keepdims=True)
    g_idx = jnp.min(jnp.where(ge == gmax, grp, float(n_exp)), axis=1, keepdims=True)
    gid_o[...] = jnp.broadcast_to(g_idx, ge.shape).astype(jnp.int32)


def _merge(x, ro, gr, mo, gm, wts, group_size, n_exp, tm):
    n, d_model = x.shape
    tok = lambda w: pl.BlockSpec((tm, w), lambda i: (i, 0))
    w_args = [wts[k] for k in ("wro", "wmo", "wout", "gffn", "wr_hi", "wr_lo")]
    in_specs = [tok(d_model), tok(ro.shape[1]), tok(d_model), tok(mo.shape[1]), tok(d_model)] + [
        _const_spec(a.shape) for a in w_args]
    out_shape = (jax.ShapeDtypeStruct((n, d_model), F32), jax.ShapeDtypeStruct((n, LANES), jnp.int32))
    kern = functools.partial(_merge_kernel, group_size=group_size, n_exp=n_exp)
    return pl.pallas_call(kern, out_shape=out_shape, grid=(n // tm,), in_specs=in_specs,
                          out_specs=(tok(d_model), tok(LANES)),
                          compiler_params=_cparams(1), name="merge_router")(x, ro, gr, mo, gm, *w_args)


def _group_tiles(gid, n_grp, tmr):
    n = gid.shape[0]
    token = jnp.arange(n, dtype=jnp.int32)
    counts = jnp.sum((gid[:, None] == jnp.arange(n_grp, dtype=jnp.int32)[None, :]).astype(jnp.int32), axis=0)
    by_group = jnp.sort(gid * n + token) % n
    dense_start = jnp.cumsum(counts) - counts
    padded = ((counts + tmr - 1) // tmr) * tmr
    ends = jnp.cumsum(padded)
    offs = ends - padded
    n_tiles = n // tmr + n_grp
    tile_start = jnp.arange(n_tiles, dtype=jnp.int32) * tmr
    tile_group = jnp.minimum(jnp.sum((tile_start[:, None] >= ends[None, :]).astype(jnp.int32), axis=1),
                             n_grp - 1)
    tile_valid = jnp.clip(counts[tile_group] - (tile_start - offs[tile_group]), 0, tmr)
    row_group = jnp.repeat(tile_group, tmr)
    in_group = jnp.arange(n_tiles * tmr, dtype=jnp.int32) - offs[row_group]
    src = jnp.clip(dense_start[row_group] + in_group, 0, n - 1)
    row_token = jnp.where(in_group < counts[row_group], by_group[src], 0)
    return row_token, tile_group, tile_valid.astype(jnp.int32)


def _moe_kernel(rt_ref, tg_ref, tv_ref, x1_hbm, gffn_ref, gfin_ref, whi_ref, wlo_ref, wg_ref, wu_ref, wd_ref,
                y_hbm, xbuf, obuf, trash, gsem, ssem, *, tmr, group_size, n_exp):
    t = pl.program_id(0)
    last = pl.num_programs(0) - 1
    slot = t % 2

    def gather_copy(tok, r, slot_):
        return pltpu.make_async_copy(x1_hbm.at[pl.ds(tok, 1)], xbuf.at[slot_, pl.ds(r, 1)], gsem.at[slot_])

    def start_gather(tile, slot_):
        for r in range(tmr):
            gather_copy(rt_ref[tile * tmr + r], r, slot_).start()

    def wait_gather(slot_):
        for r in range(tmr):
            gather_copy(0, r, slot_).wait()

    def scatter_copy(r, dst_row_ref, slot_):
        return pltpu.make_async_copy(obuf.at[slot_, pl.ds(r, 1)], dst_row_ref, ssem.at[slot_])

    def wait_scatter(slot_):
        for r in range(tmr):
            scatter_copy(r, trash.at[slot_, pl.ds(r, 1)], slot_).wait()

    @pl.when(t == 0)
    def _():
        start_gather(t, slot)

    wait_gather(slot)

    @pl.when(t >= 2)
    def _():
        wait_scatter(slot)

    start_gather(jnp.minimum(t + 1, last), 1 - slot)

    g = tg_ref[t]
    x = xbuf[slot]
    hb, ge, el = _router_logits(_rms(x, gffn_ref[...]), whi_ref, wlo_ref)
    lane_i = lax.broadcasted_iota(jnp.int32, ge.shape, 1)
    lane = lane_i.astype(F32)
    is_exp = lane_i < n_exp
    in_grp = (lane_i // group_size) == g
    gmax = jnp.max(jnp.where(is_exp, ge, -jnp.inf), axis=1, keepdims=True)
    gsum = jnp.sum(jnp.where(is_exp, jnp.exp(ge - gmax), 0.0), axis=1, keepdims=True) / group_size
    ge_g = jnp.max(jnp.where(in_grp, ge, -jnp.inf), axis=1, keepdims=True)
    g_w = jnp.exp(ge_g - gmax) / gsum
    e_in = jnp.where(in_grp, el, -jnp.inf)
    top1 = jnp.max(e_in, axis=1, keepdims=True)
    idx1 = jnp.min(jnp.where(e_in == top1, lane, float(LANES)), axis=1, keepdims=True)
    e_rest = jnp.where(lane == idx1, -jnp.inf, e_in)
    top2 = jnp.max(e_rest, axis=1, keepdims=True)
    idx2 = jnp.min(jnp.where(e_rest == top2, lane, float(LANES)), axis=1, keepdims=True)
    e2 = jnp.exp(top2 - top1)
    comb = (jnp.where(lane == idx1, g_w / (1.0 + e2), 0.0)
            + jnp.where(lane == idx2, g_w * e2 / (1.0 + e2), 0.0))

    first = (g * group_size).astype(F32)
    parts = []
    for e in range(group_size):
        ce = jnp.sum(jnp.where(lane == first + e, comb, 0.0), axis=1, keepdims=True)
        hg = _dot(hb, wg_ref[e])
        parts.append((hg * _sigmoid(hg) * _dot(hb, wu_ref[e]) * ce).astype(BF16))
    y = x + _dot(jnp.concatenate(parts, axis=1), wd_ref[0])
    obuf[slot] = _rms(y, gfin_ref[...])

    n_real = tv_ref[t]

    def scatter_real(r, carry):
        tok = rt_ref[t * tmr + r]
        scatter_copy(r, y_hbm.at[pl.ds(tok, 1)], slot).start()
        return carry

    def scatter_pad(r, carry):
        scatter_copy(r, trash.at[slot, pl.ds(r, 1)], slot).start()
        return carry

    @pl.when(n_real == tmr)
    def _():
        for r in range(tmr):
            scatter_real(r, 0)

    @pl.when(n_real < tmr)
    def _():
        lax.fori_loop(0, n_real, scatter_real, 0)
        lax.fori_loop(n_real, tmr, scatter_pad, 0)

    @pl.when(t == last)
    def _():
        wait_gather(1 - slot)
        wait_scatter(slot)

        @pl.when(t >= 1)
        def _():
            wait_scatter(1 - slot)


def _moe(x1, gid, wts, wg, wu, wd, gfin, group_size, n_exp, tmr):
    n, d_model = x1.shape
    n_grp = wd.shape[0]
    row_token, tile_group, tile_valid = _group_tiles(gid, n_grp, tmr)
    n_tiles = tile_group.shape[0]
    hbm = pl.BlockSpec(memory_space=pl.ANY)
    const = lambda a: pl.BlockSpec(a.shape, lambda t, rt, tg, tv: (0,) * a.ndim, pipeline_mode=pl.Buffered(1))
    grp_w = lambda a: pl.BlockSpec((a.shape[0] // n_grp,) + a.shape[1:], lambda t, rt, tg, tv: (tg[t], 0, 0))
    w_args = [wts["gffn"], gfin, wts["wr_hi"], wts["wr_lo"]]
    grid_spec = pltpu.PrefetchScalarGridSpec(
        num_scalar_prefetch=3, grid=(n_tiles,),
        in_specs=[hbm] + [const(a) for a in w_args] + [grp_w(wg), grp_w(wu), grp_w(wd)],
        out_specs=hbm,
        scratch_shapes=[pltpu.VMEM((2, tmr, d_model), F32), pltpu.VMEM((2, tmr, d_model), F32),
                        pltpu.VMEM((2, tmr, d_model), F32),
                        pltpu.SemaphoreType.DMA((2,)), pltpu.SemaphoreType.DMA((2,))])
    kern = functools.partial(_moe_kernel, tmr=tmr, group_size=group_size, n_exp=n_exp)
    return pl.pallas_call(kern, out_shape=jax.ShapeDtypeStruct((n, d_model), F32), grid_spec=grid_spec,
                          compiler_params=_cparams(1), name="moe")(
        row_token, tile_group, tile_valid, x1, *w_args, wg, wu, wd)


def _rope_tables(start, count, repeat, dk, rope_dim, nope):
    n_hi = -(-count // LANES)

    def cs(half):
        inv = ROPE_BASE ** (-jnp.arange(half, dtype=F32) / half)
        a = (start + LANES * jnp.arange(n_hi, dtype=F32))[:, None, None] * inv
        b = jnp.arange(LANES, dtype=F32)[None, :, None] * inv
        ca, sa, cb, sb = jnp.cos(a), jnp.sin(a), jnp.cos(b), jnp.sin(b)
        c = (ca * cb - sa * sb).reshape(n_hi * LANES, half)[:count]
        s = (sa * cb + ca * sb).reshape(n_hi * LANES, half)[:count]
        return jnp.repeat(c, repeat, axis=0), jnp.repeat(s, repeat, axis=0)

    c, s = cs(dk // 2)
    cosr = jnp.concatenate([c, c], axis=1)
    sinr = jnp.concatenate([-s, s], axis=1)
    c, s = cs(rope_dim // 2)
    n = count * repeat
    tail = jnp.zeros((n, HEAD_PAD - nope - rope_dim), F32)
    cp = jnp.concatenate([jnp.ones((n, nope), F32), c, c, tail], axis=1)
    sp = jnp.concatenate([jnp.zeros((n, nope), F32), -s, s, tail], axis=1)
    return cosr, sinr, cp, sp


def _swap_halves(w):
    half = w.shape[-1] // 2
    return jnp.concatenate([w[..., half:], w[..., :half]], axis=-1)


def _layout_weights(norm_mix_g, w_in, q_norm_g, kv_norm_g, w_uq, w_ukv, w_ret_o, w_mla_o, w_out,
                    norm_ffn_g, w_router_group, w_router_expert, dims):
    n_rh, dk, dv, q_rank, kv_rank, rope_dim, n_mh, nope, dv_m = dims
    d_model = w_in.shape[0]
    qk_w, v_w = n_rh * dk, n_rh * dv
    o_cq = 2 * qk_w + 2 * v_w
    o_ckv = o_cq + q_rank
    o_kr = o_ckv + kv_rank
    o_gate = o_kr + rope_dim
    pad_tail = HEAD_PAD - nope - rope_dim
    zeros = lambda *shape: jnp.zeros(shape, F32)
    w_kr = w_in[:, o_kr:o_gate]
    place = lambda w: jnp.concatenate([zeros(d_model, nope), w, zeros(d_model, pad_tail)], axis=1)
    wsmall = jnp.concatenate([w_in[:, o_cq:o_kr], place(w_kr), place(_swap_halves(w_kr))], axis=1)
    wmain = jnp.concatenate([w_in[:, :o_cq], w_in[:, o_gate:]], axis=1)

    uq = w_uq.reshape(q_rank, n_mh, nope + rope_dim)
    uq_rope = uq[..., nope:]
    wq = jnp.concatenate([uq, zeros(q_rank, n_mh, pad_tail)], axis=-1)
    wqsw = jnp.concatenate([zeros(q_rank, n_mh, nope), _swap_halves(uq_rope),
                            zeros(q_rank, n_mh, pad_tail)], axis=-1)
    uk = w_ukv[..., :nope]
    uv = w_ukv[..., nope:]
    wk = jnp.concatenate([uk, zeros(kv_rank, n_mh, HEAD_PAD - nope)], axis=-1)
    wv = jnp.concatenate([uv, zeros(kv_rank, n_mh, HEAD_PAD - dv_m)], axis=-1)
    flat = lambda w: w.reshape(w.shape[0], n_mh * HEAD_PAD).astype(BF16)
    wmo = w_mla_o
    wabs = jnp.concatenate([jnp.transpose(uk, (1, 2, 0)), zeros(n_mh, HEAD_PAD - nope, kv_rank)], axis=1)
    wvh = jnp.transpose(uv, (1, 0, 2))
    n_grp = w_router_group.shape[1]
    n_exp = w_router_expert.shape[1]
    group_size = n_exp // n_grp
    assert n_exp <= LANES
    w_r = jnp.concatenate([jnp.repeat(w_router_group, group_size, axis=1), zeros(d_model, LANES - n_exp),
                           w_router_expert, zeros(d_model, LANES - n_exp)], axis=1)
    wr_hi = w_r.astype(BF16)
    wr_lo = (w_r - wr_hi.astype(F32)).astype(BF16)
    return dict(
        gmix=norm_mix_g.reshape(1, -1), wmain=wmain.astype(BF16), wsmall=wsmall.astype(BF16),
        gq=q_norm_g.reshape(1, -1), gkv=kv_norm_g.reshape(1, -1),
        wq=flat(wq), wqsw=flat(wqsw), wk=flat(wk), wv=flat(wv),
        wro=w_ret_o.astype(BF16), wmo=wmo.astype(BF16), wout=w_out.astype(BF16),
        gffn=norm_ffn_g.reshape(1, -1), wr_hi=wr_hi, wr_lo=wr_lo,
        wabs=wabs.astype(BF16), wvh=wvh.astype(BF16)), group_size, n_exp


def kernel(x_prompt, x_sample, cache_kv_latent, cache_k_rope, state_retention, page_table, norm_mix_g,
           w_in, q_norm_g, kv_norm_g, w_uq, w_ukv, ret_gn_g, w_ret_o, w_mla_o, w_out, norm_ffn_g,
           w_router_group, w_router_expert, w_gate, w_up, w_down, norm_final_g):
    depth = w_in.shape[0]
    assert depth == 1, "single-layer step only"
    b, t, d_model = x_prompt.shape
    bd, tn, _ = x_sample.shape
    assert tn == 1, "one new token per sample sequence"
    _, _, n_rh, dk, dv = state_retention.shape
    kv_rank = cache_kv_latent.shape[-1]
    rope_dim = cache_k_rope.shape[-1]
    page = cache_kv_latent.shape[2]
    n_mh = w_ukv.shape[2]
    q_rank = w_uq.shape[1]
    nope = w_uq.shape[2] // n_mh - rope_dim
    dv_m = w_ukv.shape[3] - nope
    assert nope + rope_dim <= HEAD_PAD and dv_m < HEAD_PAD
    dims = (n_rh, dk, dv, q_rank, kv_rank, rope_dim, n_mh, nope, dv_m)
    past_len = page_table.shape[1] * page

    wts, group_size, n_exp = _layout_weights(norm_mix_g[0], w_in[0], q_norm_g[0], kv_norm_g[0], w_uq[0],
                                             w_ukv[0], w_ret_o[0], w_mla_o[0], w_out[0], norm_ffn_g[0],
                                             w_router_group[0], w_router_expert[0], dims)
    gn = ret_gn_g[0]
    n_grp = n_exp // group_size
    d_exp = w_gate.shape[-1]
    wg, wu = w_gate[0].astype(BF16), w_up[0].astype(BF16)
    wd = w_down[0].astype(BF16).reshape(n_grp, group_size * d_exp, d_model)
    gfin = norm_final_g.reshape(1, -1)

    tm_p = min(256, t)
    xp = x_prompt.reshape(b * t, d_model)
    tabs_p = _rope_tables(0.0, t, 1, dk, rope_dim, nope)
    (ret_o, st_p, gret, gmla, lat_p, rope_p, q_p, k_p, v_p) = _inproj(xp, tabs_p, wts, dims, tm_p,
                                                                      ret_gn=gn, n_seq=b)
    mla_o = _attn_prompt(q_p, k_p, v_p, b, t, min(ATTN_TQ, t), min(ATTN_HEADS_PER_STEP, n_mh), dv_m)
    x1, gid = _merge(xp, ret_o, gret, mla_o, gmla, wts, group_size, n_exp,
                     min(MERGE_ROW_TILE, t))
    y_prompt = _moe(x1, gid[:, 0], wts, wg, wu, wd, gfin, group_size, n_exp,
                    min(MOE_ROW_TILE, b * t)).reshape(b, t, d_model)

    xs = x_sample.reshape(bd, d_model)
    tabs_s = _rope_tables(float(past_len), tn, bd, dk, rope_dim, nope)
    (rq, rk, rv, rgs, gret, gmla, lat_s, rope_s, q_s, _, _) = _inproj(xs, tabs_s, wts, dims, bd)
    ret_o_s, st_s = _ret_sample(rq, rk, rv, rgs, state_retention[0], gn, n_rh, dk, dv)
    q_lat = jnp.transpose(_headmm(q_s, wts["wabs"], F32), (1, 0, 2))
    q_pe = jnp.transpose(q_s[:, :, nope:nope + rope_dim].astype(F32), (1, 0, 2))
    o_lat = _attn_sample(page_table, q_lat, q_pe, lat_s.reshape(bd, 1, kv_rank),
                         rope_s.reshape(bd, 1, rope_dim), cache_kv_latent.reshape(-1, page, kv_rank),
                         jnp.transpose(cache_k_rope.reshape(-1, page, rope_dim), (0, 2, 1)),
                         min(DECODE_PAGES, page_table.shape[1]))
    mla_o_s = _headmm(jnp.transpose(o_lat, (1, 0, 2)), wts["wvh"], BF16)
    mla_o_s = jnp.transpose(mla_o_s, (1, 0, 2)).reshape(bd, n_mh * dv_m)
    x1, gid = _merge(xs, ret_o_s, gret, mla_o_s, gmla, wts, group_size, n_exp, bd)
    y_sample = _moe(x1, gid[:, 0], wts, wg, wu, wd, gfin, group_size, n_exp,
                    min(MOE_ROW_TILE, bd)).reshape(bd, tn, d_model)

    return (y_prompt, y_sample,
            lat_p.reshape(depth, b, t // page, page, kv_rank),
            rope_p.reshape(depth, b, t // page, page, rope_dim),
            st_p.reshape(depth, b, n_rh, dk, dv),
            lat_s.reshape(depth, bd, tn, kv_rank),
            rope_s.reshape(depth, bd, tn, rope_dim),
            st_s.reshape(depth, bd, n_rh, dk, dv))
```

```python
import functools
import math

import numpy as np
import jax
import jax.numpy as jnp
from jax import lax
from jax.experimental import pallas as pl
from jax.experimental.pallas import tpu as pltpu

F32 = jnp.float32
BF16 = jnp.bfloat16

ROPE_BASE = 10000.0
EPS = 1e-6
RET_CHUNK = 128
LANES = 128
HEAD_PAD = LANES
MERGE_ROW_TILE = 512
MOE_ROW_TILE = 256
RET_SAMPLE_SEQS = 8
DECODE_PAGES = 64
DECODE_SLOTS = 3
ATTN_TQ = 512
ATTN_HEADS_PER_STEP = 8
VMEM_LIMIT = 48 * 1024 * 1024


def _cparams(n_axes):
    return pltpu.CompilerParams(dimension_semantics=("arbitrary",) * n_axes,
                                vmem_limit_bytes=VMEM_LIMIT)


def _const_spec(shape):
    nd = len(shape)
    return pl.BlockSpec(shape, lambda *_: (0,) * nd, pipeline_mode=pl.Buffered(1))


def _rms(x, g):
    return x * lax.rsqrt(jnp.mean(x * x, axis=-1, keepdims=True) + EPS) * g


def _sigmoid(x):
    return 1.0 / (1.0 + jnp.exp(-x))


def _dot(a, b):
    return jnp.dot(a, b, preferred_element_type=F32)


def _dot_nt(a, b):
    return lax.dot_general(a, b, (((1,), (1,)), ((), ())), preferred_element_type=F32)


def _retention_chunk(q, k, v, gate, state, i, mask_ref, qd_ref, kd_ref, gn_ref, chunk_decay):
    s = _dot_nt(q, k) * mask_ref[i]
    o = _dot(s.astype(BF16), v) + _dot(q, state.astype(BF16)) * qd_ref[i]
    kd = (k.astype(F32) * kd_ref[i]).T.astype(BF16)
    new_state = state * chunk_decay[i] + _dot(kd, v)
    return (_rms(o, gn_ref[i]) * gate.astype(F32)).astype(BF16), new_state


def _inproj_kernel(x_ref, gmix_ref, wmain_ref, wsmall_ref, gq_ref, gkv_ref, wq_ref, wqsw_ref,
                   wk_ref, wv_ref, cosr_ref, sinr_ref, cp_ref, sp_ref, *rest,
                   n_rh, dk, dv, q_rank, kv_rank, rope_dim, n_mh, nope, dv_m, k_scale, q_scale,
                   ret_chunk, blocks_per_seq, chunk_decay):
    if ret_chunk:
        (mask_ref, qd_ref, kd_ref, gn_ref, reto_o, st_o,
         gret_o, gmla_o, lat_o, rope_o, q_o, k_o, v_o, s_scr) = rest
    else:
        rq_o, rk_o, rv_o, rgs_o, gret_o, gmla_o, lat_o, rope_o, q_o, k_o, v_o = rest
    x = x_ref[...]
    h = _rms(x, gmix_ref[...]).astype(BF16)
    cosr = cosr_ref[...]
    sinr = sinr_ref[...]
    qk_w = n_rh * dk
    v_w = n_rh * dv

    def rope_heads(z, scale):
        outs = []
        for i in range(n_rh):
            seg = z[:, i * dk:(i + 1) * dk]
            rot = seg * cosr + pltpu.roll(seg, dk // 2, 1) * sinr
            outs.append(rot if scale is None else rot * scale)
        return jnp.concatenate(outs, axis=1)

    off = 0
    rq = rope_heads(_dot(h, wmain_ref[:, off:off + qk_w]), None).astype(BF16)
    off += qk_w
    rk = rope_heads(_dot(h, wmain_ref[:, off:off + qk_w]), k_scale).astype(BF16)
    off += qk_w
    rv = _dot(h, wmain_ref[:, off:off + v_w]).astype(BF16)
    off += v_w
    rg = _dot(h, wmain_ref[:, off:off + v_w])
    rgs = (rg * _sigmoid(rg)).astype(BF16)
    off += v_w
    if ret_chunk:
        blk = pl.program_id(0) % blocks_per_seq

        @pl.when(blk == 0)
        def _():
            s_scr[...] = jnp.zeros_like(s_scr)

        for c in range(x.shape[0] // ret_chunk):
            rows = slice(c * ret_chunk, (c + 1) * ret_chunk)
            for i in range(n_rh):
                qs, vs = slice(i * dk, (i + 1) * dk), slice(i * dv, (i + 1) * dv)
                o, s_scr[i] = _retention_chunk(rq[rows, qs], rk[rows, qs], rv[rows, vs], rgs[rows, vs],
                                               s_scr[i], i, mask_ref, qd_ref, kd_ref, gn_ref, chunk_decay)
                reto_o[rows, vs] = o

        @pl.when(blk == blocks_per_seq - 1)
        def _():
            st_o[0] = s_scr[...]
    else:
        rq_o[...] = rq
        rk_o[...] = rk
        rv_o[...] = rv
        rgs_o[...] = rgs
    d_model = x.shape[1]
    gret_o[...] = _sigmoid(_dot(h, wmain_ref[:, off:off + d_model])).astype(BF16)
    off += d_model
    gmla_o[...] = _sigmoid(_dot(h, wmain_ref[:, off:off + d_model])).astype(BF16)

    zs = _dot(h, wsmall_ref[...])
    cq = zs[:, :q_rank]
    ckv = zs[:, q_rank:q_rank + kv_rank]
    krp = zs[:, q_rank + kv_rank:q_rank + kv_rank + HEAD_PAD]
    krsw = zs[:, q_rank + kv_rank + HEAD_PAD:]
    cp = cp_ref[...]
    sp = sp_ref[...]
    cqn = _rms(cq, gq_ref[...]).astype(BF16)
    ckvn = _rms(ckv, gkv_ref[...])
    lat_o[...] = ckvn
    ckvb = ckvn.astype(BF16)
    kr_rot = krp * cp + krsw * sp
    rope_o[...] = pltpu.roll(kr_rot, HEAD_PAD - nope, 1)[:, :rope_dim]

    qh = _dot(cqn, wq_ref[...])
    qs = _dot(cqn, wqsw_ref[...])
    kh = _dot(ckvb, wk_ref[...])
    vh = _dot(ckvb, wv_ref[...])
    sum_lane = lax.broadcasted_iota(jnp.int32, cp.shape, 1) == dv_m
    for i in range(n_mh):
        sl = slice(i * HEAD_PAD, (i + 1) * HEAD_PAD)
        q_o[i] = ((qh[:, sl] * cp + qs[:, sl] * sp) * q_scale).astype(BF16)
        k_o[i] = (kh[:, sl] + kr_rot).astype(BF16)
        v_o[i] = jnp.where(sum_lane, 1.0, vh[:, sl]).astype(BF16)


def _inproj(x, tables, wts, dims, tm, ret_gn=None, n_seq=1):
    n, d_model = x.shape
    cosr, sinr, cp, sp = tables
    tab_blocks = cosr.shape[0] // tm
    n_rh, dk, dv, q_rank, kv_rank, rope_dim, n_mh, nope, dv_m = dims
    grid = (n // tm,)
    tok = lambda w: pl.BlockSpec((tm, w), lambda i: (i, 0))
    tab = lambda w: pl.BlockSpec((tm, w), lambda i: (i % tab_blocks, 0))
    head = pl.BlockSpec((n_mh, tm, HEAD_PAD), lambda i: (0, i, 0))
    w_names = ("gmix", "wmain", "wsmall", "gq", "gkv", "wq", "wqsw", "wk", "wv")
    args = [x] + [wts[k] for k in w_names] + [cosr, sinr, cp, sp]
    in_specs = ([tok(d_model)] + [_const_spec(wts[k].shape) for k in w_names]
                + [tab(dk), tab(dk), tab(HEAD_PAD), tab(HEAD_PAD)])
    qk_w, v_w = n_rh * dk, n_rh * dv
    sds = jax.ShapeDtypeStruct
    common_shape = (sds((n, d_model), BF16), sds((n, d_model), BF16), sds((n, kv_rank), F32),
                    sds((n, rope_dim), F32), sds((n_mh, n, HEAD_PAD), BF16), sds((n_mh, n, HEAD_PAD), BF16),
                    sds((n_mh, n, HEAD_PAD), BF16))
    common_specs = (tok(d_model), tok(d_model), tok(kv_rank), tok(rope_dim), head, head, head)
    scratch = []
    if ret_gn is None:
        ret_chunk, blocks_per_seq, cd = 0, 1, None
        out_shape = (sds((n, qk_w), BF16), sds((n, qk_w), BF16), sds((n, v_w), BF16),
                     sds((n, v_w), BF16)) + common_shape
        out_specs = (tok(qk_w), tok(qk_w), tok(v_w), tok(v_w)) + common_specs
    else:
        ret_chunk = min(RET_CHUNK, tm)
        blocks_per_seq = n // n_seq // tm
        mask, qd, kd, cd, _ = _ret_tables(n_rh, ret_chunk, dk, dv)
        tabs = [mask, qd, kd, ret_gn.reshape(n_rh, 1, dv)]
        args += tabs
        in_specs += [_const_spec(a.shape) for a in tabs]
        out_shape = (sds((n, v_w), BF16), sds((n_seq, n_rh, dk, dv), F32)) + common_shape
        out_specs = (tok(v_w), pl.BlockSpec((1, n_rh, dk, dv), lambda i: (i // blocks_per_seq, 0, 0, 0))
                     ) + common_specs
        scratch = [pltpu.VMEM((n_rh, dk, dv), F32)]
    kern = functools.partial(_inproj_kernel, n_rh=n_rh, dk=dk, dv=dv, q_rank=q_rank, kv_rank=kv_rank,
                             rope_dim=rope_dim, n_mh=n_mh, nope=nope, dv_m=dv_m, k_scale=dk ** -0.5,
                             q_scale=(nope + rope_dim) ** -0.5 * math.log2(math.e),
                             ret_chunk=ret_chunk, blocks_per_seq=blocks_per_seq, chunk_decay=cd)
    return pl.pallas_call(kern, out_shape=out_shape, grid=grid, in_specs=in_specs, out_specs=out_specs,
                          scratch_shapes=scratch, compiler_params=_cparams(1), name="inproj")(*args)


def _ret_tables(n_rh, chunk, dk, dv):
    log_g = np.log1p(-np.exp2(-5.0 - np.arange(n_rh, dtype=np.float64)))
    idx = np.arange(chunk, dtype=np.float64)
    diff = idx[:, None] - idx[None, :]
    mask = np.where(diff >= 0, np.exp(log_g[:, None, None] * np.maximum(diff, 0.0)), 0.0)
    qd = np.exp(log_g[:, None] * (idx + 1.0))
    kd = np.exp(log_g[:, None] * (chunk - 1.0 - idx))
    cd = np.exp(log_g * chunk)
    qd = np.broadcast_to(qd[:, :, None], (n_rh, chunk, dv))
    kd = np.broadcast_to(kd[:, :, None], (n_rh, chunk, dk))
    return (jnp.asarray(mask, F32), jnp.asarray(qd, F32), jnp.asarray(kd, F32),
            tuple(float(np.float32(v)) for v in cd), tuple(float(np.float32(v)) for v in np.exp(log_g)))


def _ret_sample_kernel(qt_ref, kt_ref, v_ref, gate_ref, st_ref, gn_ref, o_ref, ns_ref,
                       *, n_rh, dk, dv, gamma):
    seqs = v_ref.shape[0]
    lane = lax.broadcasted_iota(jnp.int32, qt_ref.shape, 1)
    for j in range(seqs):
        sel = lane == pl.program_id(0) * seqs + j
        qcol = jnp.sum(jnp.where(sel, qt_ref[...], 0.0), axis=1, keepdims=True)
        kcol = jnp.sum(jnp.where(sel, kt_ref[...], 0.0), axis=1, keepdims=True)
        v = v_ref[j]
        gate = gate_ref[j]
        for i in range(n_rh):
            vi = v[:, i * dv:(i + 1) * dv]
            new = st_ref[j, i] * gamma[i] + kcol[i * dk:(i + 1) * dk] * vi
            ns_ref[j, i] = new
            o = jnp.sum(qcol[i * dk:(i + 1) * dk] * new, axis=0, keepdims=True)
            o_ref[j, :, i * dv:(i + 1) * dv] = _rms(o, gn_ref[i]) * gate[:, i * dv:(i + 1) * dv]


def _ret_sample(rq, rk, rv, rgs, state, gn, n_rh, dk, dv):
    nb = rq.shape[0]
    _, _, _, _, gamma = _ret_tables(n_rh, 1, dk, dv)
    qt = rq.astype(F32).T
    kt = rk.astype(F32).T
    seqs = math.gcd(nb, RET_SAMPLE_SEQS)
    row = lambda w: pl.BlockSpec((seqs, 1, w), lambda b: (b, 0, 0))
    st_spec = pl.BlockSpec((seqs, n_rh, dk, dv), lambda b: (b, 0, 0, 0))
    in_specs = [_const_spec(qt.shape), _const_spec(kt.shape), row(n_rh * dv), row(n_rh * dv), st_spec,
                _const_spec((n_rh, 1, dv))]
    out_shape = (jax.ShapeDtypeStruct((nb, 1, n_rh * dv), F32),
                 jax.ShapeDtypeStruct((nb, n_rh, dk, dv), F32))
    kern = functools.partial(_ret_sample_kernel, n_rh=n_rh, dk=dk, dv=dv, gamma=gamma)
    o, ns = pl.pallas_call(kern, out_shape=out_shape, grid=(nb // seqs,), in_specs=in_specs,
                           out_specs=(row(n_rh * dv), st_spec), compiler_params=_cparams(1),
                           name="retention_sample")(
        qt, kt, rv.astype(F32).reshape(nb, 1, -1), rgs.astype(F32).reshape(nb, 1, -1), state,
        gn.reshape(n_rh, 1, dv))
    return o.reshape(nb, n_rh * dv).astype(BF16), ns


def _attn_prompt_kernel(q_ref, k_ref, v_ref, o_ref, m_scr, acc_scr, *, tq, hpg, sum_lane):
    i = pl.program_id(2)
    m_scr[...] = jnp.full(m_scr.shape, -jnp.inf, F32)
    acc_scr[...] = jnp.zeros(acc_scr.shape, F32)

    def tile(j, masked):
        start = pl.multiple_of(j * tq, tq)
        for h in range(hpg):
            s = _dot_nt(q_ref[h], k_ref[h, pl.ds(start, tq), :])
            if masked:
                row = lax.broadcasted_iota(jnp.int32, s.shape, 0)
                col = lax.broadcasted_iota(jnp.int32, s.shape, 1)
                s = jnp.where(col <= row, s, -jnp.inf)
            m = m_scr[h]
            m_new = jnp.maximum(m, jnp.max(s, axis=1, keepdims=True))
            m_wide = jnp.concatenate([m_new] * (tq // HEAD_PAD), axis=1)
            p = jnp.exp2((s - m_wide).astype(BF16))
            acc_scr[h] = jnp.exp2(m - m_new) * acc_scr[h] + _dot(p, v_ref[h, pl.ds(start, tq), :])
            m_scr[h] = m_new

    def body(j, carry):
        tile(j, False)
        return carry

    lax.fori_loop(0, i, body, 0)
    tile(i, True)
    for h in range(hpg):
        acc = acc_scr[h]
        o = acc / acc[:, sum_lane:sum_lane + 1]
        o_ref[:, h * sum_lane:(h + 1) * sum_lane] = o[:, :sum_lane].astype(BF16)


def _attn_prompt(q, k, v, b, t, tq, hpg, sum_lane):
    n_mh, n, _ = q.shape
    nq = t // tq
    kv_spec = pl.BlockSpec((hpg, t, HEAD_PAD), lambda bi, g, i: (g, bi, 0), pipeline_mode=pl.Buffered(1))
    in_specs = [pl.BlockSpec((hpg, tq, HEAD_PAD), lambda bi, g, i: (g, bi * nq + i, 0)), kv_spec, kv_spec]
    out_specs = pl.BlockSpec((tq, hpg * sum_lane), lambda bi, g, i: (bi * nq + i, g))
    return pl.pallas_call(functools.partial(_attn_prompt_kernel, tq=tq, hpg=hpg, sum_lane=sum_lane),
                          out_shape=jax.ShapeDtypeStruct((n, n_mh * sum_lane), BF16),
                          grid=(b, n_mh // hpg, nq), in_specs=in_specs, out_specs=out_specs,
                          scratch_shapes=[pltpu.VMEM((hpg, tq, HEAD_PAD), F32),
                                          pltpu.VMEM((hpg, tq, HEAD_PAD), F32)],
                          compiler_params=_cparams(3), name="attn_prompt")(q, k, v)


def _attn_sample_kernel(pt_ref, ql_ref, qp_ref, cn_ref, rn_ref, lat_hbm, rope_hbm, o_ref,
                        lat_slab, rope_slab, sems, m_scr, l_scr, acc_scr, *, npg, n_chunks, page):
    step = pl.program_id(0)
    n_steps = pl.num_programs(0)
    n_slots = lat_slab.shape[0]
    ahead = n_slots - 1
    slot = step % n_slots
    chunk = step % n_chunks

    def page_copies(page_ids, slot_):
        out = []
        for i in range(npg):
            rows = pl.ds(i * page, page)
            out.append(pltpu.make_async_copy(lat_hbm.at[page_ids(i)], lat_slab.at[slot_, rows, :],
                                             sems.at[0, slot_]))
            out.append(pltpu.make_async_copy(rope_hbm.at[page_ids(i)], rope_slab.at[slot_, :, rows],
                                             sems.at[1, slot_]))
        return out

    def start_step(step_, slot_):
        seq = step_ // n_chunks
        first = (step_ % n_chunks) * npg
        for cp in page_copies(lambda i: pt_ref[seq, first + i], slot_):
            cp.start()

    for k in range(ahead):
        @pl.when(jnp.logical_and(step == 0, k < n_steps))
        def _():
            start_step(k, k)

    @pl.when(step + ahead < n_steps)
    def _():
        start_step(step + ahead, (step + ahead) % n_slots)

    for cp in page_copies(lambda i: 0, slot):
        cp.wait()

    ql = ql_ref[0]
    qp = qp_ref[0]

    @pl.when(chunk == 0)
    def _():
        cn = cn_ref[0]
        s0 = (jnp.sum(ql * cn, axis=1, keepdims=True)
              + jnp.sum(qp * rn_ref[0], axis=1, keepdims=True))
        m_scr[...] = s0
        l_scr[...] = jnp.ones_like(l_scr)
        acc_scr[...] = jnp.broadcast_to(cn, acc_scr.shape)

    lat = lat_slab[slot].astype(BF16)
    s = _dot_nt(ql.astype(BF16), lat) + _dot(qp, rope_slab[slot])
    m_prev = m_scr[...]
    m_new = jnp.maximum(m_prev, jnp.max(s, axis=1, keepdims=True))
    alpha = jnp.exp2(m_prev - m_new)
    p = jnp.exp2(s - m_new)
    m_scr[...] = m_new
    l_scr[...] = alpha * l_scr[...] + jnp.sum(p, axis=1, keepdims=True)
    acc_scr[...] = alpha * acc_scr[...] + _dot(p.astype(BF16), lat)

    @pl.when(chunk == n_chunks - 1)
    def _():
        o_ref[0] = acc_scr[...] / l_scr[...]


def _attn_sample(page_table, q_lat, q_pe, c_new, r_new, cache_lat, cache_rope_t, npg):
    nb, n_mh, kv_rank = q_lat.shape
    rope_dim = q_pe.shape[-1]
    n_pages = page_table.shape[1]
    page = cache_lat.shape[1]
    n_chunks = n_pages // npg
    per_seq = lambda shp: pl.BlockSpec((1,) + shp, lambda s, pt: (s // n_chunks, 0, 0))
    hbm = pl.BlockSpec(memory_space=pl.ANY)
    in_specs = [per_seq((n_mh, kv_rank)), per_seq((n_mh, rope_dim)), per_seq((1, kv_rank)),
                per_seq((1, rope_dim)), hbm, hbm]
    grid_spec = pltpu.PrefetchScalarGridSpec(
        num_scalar_prefetch=1, grid=(nb * n_chunks,), in_specs=in_specs,
        out_specs=per_seq((n_mh, kv_rank)),
        scratch_shapes=[pltpu.VMEM((DECODE_SLOTS, npg * page, kv_rank), F32),
                        pltpu.VMEM((DECODE_SLOTS, rope_dim, npg * page), F32),
                        pltpu.SemaphoreType.DMA((2, DECODE_SLOTS)),
                        pltpu.VMEM((n_mh, 1), F32), pltpu.VMEM((n_mh, 1), F32),
                        pltpu.VMEM((n_mh, kv_rank), F32)])
    kern = functools.partial(_attn_sample_kernel, npg=npg, n_chunks=n_chunks, page=page)
    return pl.pallas_call(kern, out_shape=jax.ShapeDtypeStruct((nb, n_mh, kv_rank), F32),
                          grid_spec=grid_spec, compiler_params=_cparams(1), name="attn_sample")(
        page_table, q_lat, q_pe, c_new, r_new, cache_lat, cache_rope_t)


def _headmm_kernel(a_ref, w_ref, o_ref):
    o_ref[0] = _dot(a_ref[0].astype(BF16), w_ref[0]).astype(o_ref.dtype)


def _headmm(a, w, out_dtype):
    nh, m, kk = a.shape
    nn = w.shape[-1]
    spec = lambda r, c: pl.BlockSpec((1, r, c), lambda h: (h, 0, 0))
    return pl.pallas_call(_headmm_kernel, out_shape=jax.ShapeDtypeStruct((nh, m, nn), out_dtype),
                          grid=(nh,), in_specs=[spec(m, kk), spec(kk, nn)], out_specs=spec(m, nn),
                          compiler_params=_cparams(1), name="head_matmul")(a, w)


def _router_logits(h2, whi_ref, wlo_ref):
    hi = h2.astype(BF16)
    lo = (h2 - hi.astype(F32)).astype(BF16)
    lg = _dot(hi, whi_ref[...]) + (_dot(hi, wlo_ref[...]) + _dot(lo, whi_ref[...]))
    return hi, lg[:, :LANES], lg[:, LANES:]


def _merge_kernel(x_ref, ro_ref, gr_ref, mo_ref, gm_ref, wro_ref, wmo_ref, wout_ref, gffn_ref,
                  whi_ref, wlo_ref, x1_o, gid_o, *, group_size, n_exp):
    mixed = (_dot(ro_ref[...], wro_ref[...]) * gr_ref[...].astype(F32)
             + _dot(mo_ref[...], wmo_ref[...]) * gm_ref[...].astype(F32))
    x1 = x_ref[...] + _dot(mixed.astype(BF16), wout_ref[...])
    x1_o[...] = x1
    _, ge, _ = _router_logits(_rms(x1, gffn_ref[...]), whi_ref, wlo_ref)
    lane = lax.broadcasted_iota(jnp.int32, ge.shape, 1)
    ge = jnp.where(lane < n_exp, ge, -jnp.inf)
    grp = (lane // group_size).astype(F32)
    gmax = jnp.max(ge, axis=1, keepdims=True)
    g_idx = jnp.min(jnp.where(ge == gmax, grp, float(n_exp)), axis=1, keepdims=True)
    gid_o[...] = jnp.broadcast_to(g_idx, ge.shape).astype(jnp.int32)


def _merge(x, ro, gr, mo, gm, wts, group_size, n_exp, tm):
    n, d_model = x.shape
    tok = lambda w: pl.BlockSpec((tm, w), lambda i: (i, 0))
    w_args = [wts[k] for k in ("wro", "wmo", "wout", "gffn", "wr_hi", "wr_lo")]
    in_specs = [tok(d_model), tok(ro.shape[1]), tok(d_model), tok(mo.shape[1]), tok(d_model)] + [
        _const_spec(a.shape) for a in w_args]
    out_shape = (jax.ShapeDtypeStruct((n, d_model), F32), jax.ShapeDtypeStruct((n, LANES), jnp.int32))
    kern = functools.partial(_merge_kernel, group_size=group_size, n_exp=n_exp)
    return pl.pallas_call(kern, out_shape=out_shape, grid=(n // tm,), in_specs=in_specs,
                          out_specs=(tok(d_model), tok(LANES)),
                          compiler_params=_cparams(1), name="merge_router")(x, ro, gr, mo, gm, *w_args)


def _group_tiles(gid, n_grp, tmr):
    n = gid.shape[0]
    token = jnp.arange(n, dtype=jnp.int32)
    counts = jnp.sum((gid[:, None] == jnp.arange(n_grp, dtype=jnp.int32)[None, :]).astype(jnp.int32), axis=0)
    by_group = jnp.sort(gid * n + token) % n
    dense_start = jnp.cumsum(counts) - counts
    padded = ((counts + tmr - 1) // tmr) * tmr
    ends = jnp.cumsum(padded)
    offs = ends - padded
    n_tiles = n // tmr + n_grp
    tile_start = jnp.arange(n_tiles, dtype=jnp.int32) * tmr
    tile_group = jnp.minimum(jnp.sum((tile_start[:, None] >= ends[None, :]).astype(jnp.int32), axis=1),
                             n_grp - 1)
    tile_valid = jnp.clip(counts[tile_group] - (tile_start - offs[tile_group]), 0, tmr)
    row_group = jnp.repeat(tile_group, tmr)
    in_group = jnp.arange(n_tiles * tmr, dtype=jnp.int32) - offs[row_group]
    src = jnp.clip(dense_start[row_group] + in_group, 0, n - 1)
    row_token = jnp.where(in_group < counts[row_group], by_group[src], 0)
    return row_token, tile_group, tile_valid.astype(jnp.int32)


def _moe_kernel(rt_ref, tg_ref, tv_ref, x1_hbm, gffn_ref, gfin_ref, whi_ref, wlo_ref, wg_ref, wu_ref, wd_ref,
                y_hbm, xbuf, obuf, trash, gsem, ssem, *, tmr, group_size, n_exp):
    t = pl.program_id(0)
    last = pl.num_programs(0) - 1
    slot = t % 2

    def gather_copy(tok, r, slot_):
        return pltpu.make_async_copy(x1_hbm.at[pl.ds(tok, 1)], xbuf.at[slot_, pl.ds(r, 1)], gsem.at[slot_])

    def start_gather(tile, slot_):
        for r in range(tmr):
            gather_copy(rt_ref[tile * tmr + r], r, slot_).start()

    def wait_gather(slot_):
        for r in range(tmr):
            gather_copy(0, r, slot_).wait()

    def scatter_copy(r, dst_row_ref, slot_):
        return pltpu.make_async_copy(obuf.at[slot_, pl.ds(r, 1)], dst_row_ref, ssem.at[slot_])

    def wait_scatter(slot_):
        for r in range(tmr):
            scatter_copy(r, trash.at[slot_, pl.ds(r, 1)], slot_).wait()

    @pl.when(t == 0)
    def _():
        start_gather(t, slot)

    wait_gather(slot)

    @pl.when(jnp.logical_and(t >= 2, tv_ref[jnp.maximum(t - 2, 0)] > 0))
    def _():
        wait_scatter(slot)

    start_gather(jnp.minimum(t + 1, last), 1 - slot)

    n_real = tv_ref[t]

    @pl.when(n_real > 0)
    def _():
        g = tg_ref[t]
        x = xbuf[slot]
        hb, ge, el = _router_logits(_rms(x, gffn_ref[...]), whi_ref, wlo_ref)
        lane_i = lax.broadcasted_iota(jnp.int32, ge.shape, 1)
        lane = lane_i.astype(F32)
        is_exp = lane_i < n_exp
        in_grp = (lane_i // group_size) == g
        gmax = jnp.max(jnp.where(is_exp, ge, -jnp.inf), axis=1, keepdims=True)
        gsum = jnp.sum(jnp.where(is_exp, jnp.exp(ge - gmax), 0.0), axis=1, keepdims=True) / group_size
        ge_g = jnp.max(jnp.where(in_grp, ge, -jnp.inf), axis=1, keepdims=True)
        g_w = jnp.exp(ge_g - gmax) / gsum
        e_in = jnp.where(in_grp, el, -jnp.inf)
        top1 = jnp.max(e_in, axis=1, keepdims=True)
        idx1 = jnp.min(jnp.where(e_in == top1, lane, float(LANES)), axis=1, keepdims=True)
        e_rest = jnp.where(lane == idx1, -jnp.inf, e_in)
        top2 = jnp.max(e_rest, axis=1, keepdims=True)
        idx2 = jnp.min(jnp.where(e_rest == top2, lane, float(LANES)), axis=1, keepdims=True)
        e2 = jnp.exp(top2 - top1)
        comb = (jnp.where(lane == idx1, g_w / (1.0 + e2), 0.0)
                + jnp.where(lane == idx2, g_w * e2 / (1.0 + e2), 0.0))

        first = (g * group_size).astype(F32)
        parts = []
        for e in range(group_size):
            ce = jnp.sum(jnp.where(lane == first + e, comb, 0.0), axis=1, keepdims=True)
            hg = _dot(hb, wg_ref[e])
            parts.append((hg * _sigmoid(hg) * _dot(hb, wu_ref[e]) * ce).astype(BF16))
        y = x + _dot(jnp.concatenate(parts, axis=1), wd_ref[0])
        obuf[slot] = _rms(y, gfin_ref[...])

    def scatter_real(r, carry):
        tok = rt_ref[t * tmr + r]
        scatter_copy(r, y_hbm.at[pl.ds(tok, 1)], slot).start()
        return carry

    def scatter_pad(r, carry):
        scatter_copy(r, trash.at[slot, pl.ds(r, 1)], slot).start()
        return carry

    @pl.when(n_real == tmr)
    def _():
        for r in range(tmr):
            scatter_real(r, 0)

    @pl.when(jnp.logical_and(n_real > 0, n_real < tmr))
    def _():
        lax.fori_loop(0, n_real, scatter_real, 0)
        lax.fori_loop(n_real, tmr, scatter_pad, 0)

    @pl.when(t == last)
    def _():
        wait_gather(1 - slot)

        @pl.when(n_real > 0)
        def _():
            wait_scatter(slot)

        @pl.when(jnp.logical_and(t >= 1, tv_ref[jnp.maximum(t - 1, 0)] > 0))
        def _():
            wait_scatter(1 - slot)


def _moe(x1, gid, wts, wg, wu, wd, gfin, group_size, n_exp, tmr):
    n, d_model = x1.shape
    n_grp = wd.shape[0]
    row_token, tile_group, tile_valid = _group_tiles(gid, n_grp, tmr)
    n_tiles = tile_group.shape[0]
    hbm = pl.BlockSpec(memory_space=pl.ANY)
    const = lambda a: pl.BlockSpec(a.shape, lambda t, rt, tg, tv: (0,) * a.ndim, pipeline_mode=pl.Buffered(1))
    grp_w = lambda a: pl.BlockSpec((a.shape[0] // n_grp,) + a.shape[1:], lambda t, rt, tg, tv: (tg[t], 0, 0))
    w_args = [wts["gffn"], gfin, wts["wr_hi"], wts["wr_lo"]]
    grid_spec = pltpu.PrefetchScalarGridSpec(
        num_scalar_prefetch=3, grid=(n_tiles,),
        in_specs=[hbm] + [const(a) for a in w_args] + [grp_w(wg), grp_w(wu), grp_w(wd)],
        out_specs=hbm,
        scratch_shapes=[pltpu.VMEM((2, tmr, d_model), F32), pltpu.VMEM((2, tmr, d_model), F32),
                        pltpu.VMEM((2, tmr, d_model), F32),
                        pltpu.SemaphoreType.DMA((2,)), pltpu.SemaphoreType.DMA((2,))])
    kern = functools.partial(_moe_kernel, tmr=tmr, group_size=group_size, n_exp=n_exp)
    return pl.pallas_call(kern, out_shape=jax.ShapeDtypeStruct((n, d_model), F32), grid_spec=grid_spec,
                          compiler_params=_cparams(1), name="moe")(
        row_token, tile_group, tile_valid, x1, *w_args, wg, wu, wd)


def _rope_tables(start, count, repeat, dk, rope_dim, nope):
    n_hi = -(-count // LANES)

    def cs(half):
        inv = ROPE_BASE ** (-jnp.arange(half, dtype=F32) / half)
        a = (start + LANES * jnp.arange(n_hi, dtype=F32))[:, None, None] * inv
        b = jnp.arange(LANES, dtype=F32)[None, :, None] * inv
        ca, sa, cb, sb = jnp.cos(a), jnp.sin(a), jnp.cos(b), jnp.sin(b)
        c = (ca * cb - sa * sb).reshape(n_hi * LANES, half)[:count]
        s = (sa * cb + ca * sb).reshape(n_hi * LANES, half)[:count]
        return jnp.repeat(c, repeat, axis=0), jnp.repeat(s, repeat, axis=0)

    c, s = cs(dk // 2)
    cosr = jnp.concatenate([c, c], axis=1)
    sinr = jnp.concatenate([-s, s], axis=1)
    c, s = cs(rope_dim // 2)
    n = count * repeat
    tail = jnp.zeros((n, HEAD_PAD - nope - rope_dim), F32)
    cp = jnp.concatenate([jnp.ones((n, nope), F32), c, c, tail], axis=1)
    sp = jnp.concatenate([jnp.zeros((n, nope), F32), -s, s, tail], axis=1)
    return cosr, sinr, cp, sp


def _swap_halves(w):
    half = w.shape[-1] // 2
    return jnp.concatenate([w[..., half:], w[..., :half]], axis=-1)


def _layout_weights(norm_mix_g, w_in, q_norm_g, kv_norm_g, w_uq, w_ukv, w_ret_o, w_mla_o, w_out,
                    norm_ffn_g, w_router_group, w_router_expert, dims):
    n_rh, dk, dv, q_rank, kv_rank, rope_dim, n_mh, nope, dv_m = dims
    d_model = w_in.shape[0]
    qk_w, v_w = n_rh * dk, n_rh * dv
    o_cq = 2 * qk_w + 2 * v_w
    o_ckv = o_cq + q_rank
    o_kr = o_ckv + kv_rank
    o_gate = o_kr + rope_dim
    pad_tail = HEAD_PAD - nope - rope_dim
    zeros = lambda *shape: jnp.zeros(shape, F32)
    w_kr = w_in[:, o_kr:o_gate]
    place = lambda w: jnp.concatenate([zeros(d_model, nope), w, zeros(d_model, pad_tail)], axis=1)
    wsmall = jnp.concatenate([w_in[:, o_cq:o_kr], place(w_kr), place(_swap_halves(w_kr))], axis=1)
    wmain = jnp.concatenate([w_in[:, :o_cq], w_in[:, o_gate:]], axis=1)

    uq = w_uq.reshape(q_rank, n_mh, nope + rope_dim)
    uq_rope = uq[..., nope:]
    wq = jnp.concatenate([uq, zeros(q_rank, n_mh, pad_tail)], axis=-1)
    wqsw = jnp.concatenate([zeros(q_rank, n_mh, nope), _swap_halves(uq_rope),
                            zeros(q_rank, n_mh, pad_tail)], axis=-1)
    uk = w_ukv[..., :nope]
    uv = w_ukv[..., nope:]
    wk = jnp.concatenate([uk, zeros(kv_rank, n_mh, HEAD_PAD - nope)], axis=-1)
    wv = jnp.concatenate([uv, zeros(kv_rank, n_mh, HEAD_PAD - dv_m)], axis=-1)
    flat = lambda w: w.reshape(w.shape[0], n_mh * HEAD_PAD).astype(BF16)
    wmo = w_mla_o
    wabs = jnp.concatenate([jnp.transpose(uk, (1, 2, 0)), zeros(n_mh, HEAD_PAD - nope, kv_rank)], axis=1)
    wvh = jnp.transpose(uv, (1, 0, 2))
    n_grp = w_router_group.shape[1]
    n_exp = w_router_expert.shape[1]
    group_size = n_exp // n_grp
    assert n_exp <= LANES
    w_r = jnp.concatenate([jnp.repeat(w_router_group, group_size, axis=1), zeros(d_model, LANES - n_exp),
                           w_router_expert, zeros(d_model, LANES - n_exp)], axis=1)
    wr_hi = w_r.astype(BF16)
    wr_lo = (w_r - wr_hi.astype(F32)).astype(BF16)
    return dict(
        gmix=norm_mix_g.reshape(1, -1), wmain=wmain.astype(BF16), wsmall=wsmall.astype(BF16),
        gq=q_norm_g.reshape(1, -1), gkv=kv_norm_g.reshape(1, -1),
        wq=flat(wq), wqsw=flat(wqsw), wk=flat(wk), wv=flat(wv),
        wro=w_ret_o.astype(BF16), wmo=wmo.astype(BF16), wout=w_out.astype(BF16),
        gffn=norm_ffn_g.reshape(1, -1), wr_hi=wr_hi, wr_lo=wr_lo,
        wabs=wabs.astype(BF16), wvh=wvh.astype(BF16)), group_size, n_exp


def kernel(x_prompt, x_sample, cache_kv_latent, cache_k_rope, state_retention, page_table, norm_mix_g,
           w_in, q_norm_g, kv_norm_g, w_uq, w_ukv, ret_gn_g, w_ret_o, w_mla_o, w_out, norm_ffn_g,
           w_router_group, w_router_expert, w_gate, w_up, w_down, norm_final_g):
    depth = w_in.shape[0]
    assert depth == 1, "single-layer step only"
    b, t, d_model = x_prompt.shape
    bd, tn, _ = x_sample.shape
    assert tn == 1, "one new token per sample sequence"
    _, _, n_rh, dk, dv = state_retention.shape
    kv_rank = cache_kv_latent.shape[-1]
    rope_dim = cache_k_rope.shape[-1]
    page = cache_kv_latent.shape[2]
    n_mh = w_ukv.shape[2]
    q_rank = w_uq.shape[1]
    nope = w_uq.shape[2] // n_mh - rope_dim
    dv_m = w_ukv.shape[3] - nope
    assert nope + rope_dim <= HEAD_PAD and dv_m < HEAD_PAD
    dims = (n_rh, dk, dv, q_rank, kv_rank, rope_dim, n_mh, nope, dv_m)
    past_len = page_table.shape[1] * page

    wts, group_size, n_exp = _layout_weights(norm_mix_g[0], w_in[0], q_norm_g[0], kv_norm_g[0], w_uq[0],
                                             w_ukv[0], w_ret_o[0], w_mla_o[0], w_out[0], norm_ffn_g[0],
                                             w_router_group[0], w_router_expert[0], dims)
    gn = ret_gn_g[0]
    n_grp = n_exp // group_size
    d_exp = w_gate.shape[-1]
    wg, wu = w_gate[0].astype(BF16), w_up[0].astype(BF16)
    wd = w_down[0].astype(BF16).reshape(n_grp, group_size * d_exp, d_model)
    gfin = norm_final_g.reshape(1, -1)

    tm_p = min(512, t)
    xp = x_prompt.reshape(b * t, d_model)
    tabs_p = _rope_tables(0.0, t, 1, dk, rope_dim, nope)
    (ret_o, st_p, gret, gmla, lat_p, rope_p, q_p, k_p, v_p) = _inproj(xp, tabs_p, wts, dims, tm_p,
                                                                      ret_gn=gn, n_seq=b)
    mla_o = _attn_prompt(q_p, k_p, v_p, b, t, min(ATTN_TQ, t), min(ATTN_HEADS_PER_STEP, n_mh), dv_m)
    x1, gid = _merge(xp, ret_o, gret, mla_o, gmla, wts, group_size, n_exp,
                     min(MERGE_ROW_TILE, t))
    y_prompt = _moe(x1, gid[:, 0], wts, wg, wu, wd, gfin, group_size, n_exp,
                    min(MOE_ROW_TILE, b * t)).reshape(b, t, d_model)

    xs = x_sample.reshape(bd, d_model)
    tabs_s = _rope_tables(float(past_len), tn, bd, dk, rope_dim, nope)
    (rq, rk, rv, rgs, gret, gmla, lat_s, rope_s, q_s, _, _) = _inproj(xs, tabs_s, wts, dims, bd)
    ret_o_s, st_s = _ret_sample(rq, rk, rv, rgs, state_retention[0], gn, n_rh, dk, dv)
    q_lat = jnp.transpose(_headmm(q_s, wts["wabs"], F32), (1, 0, 2))
    q_pe = jnp.transpose(q_s[:, :, nope:nope + rope_dim].astype(F32), (1, 0, 2))
    o_lat = _attn_sample(page_table, q_lat, q_pe, lat_s.reshape(bd, 1, kv_rank),
                         rope_s.reshape(bd, 1, rope_dim), cache_kv_latent.reshape(-1, page, kv_rank),
                         jnp.transpose(cache_k_rope.reshape(-1, page, rope_dim), (0, 2, 1)),
                         min(DECODE_PAGES, page_table.shape[1]))
    mla_o_s = _headmm(jnp.transpose(o_lat, (1, 0, 2)), wts["wvh"], BF16)
    mla_o_s = jnp.transpose(mla_o_s, (1, 0, 2)).reshape(bd, n_mh * dv_m)
    x1, gid = _merge(xs, ret_o_s, gret, mla_o_s, gmla, wts, group_size, n_exp, bd)
    y_sample = _moe(x1, gid[:, 0], wts, wg, wu, wd, gfin, group_size, n_exp,
                    min(MOE_ROW_TILE, bd)).reshape(bd, tn, d_model)

    return (y_prompt, y_sample,
            lat_p.reshape(depth, b, t // page, page, kv_rank),
            rope_p.reshape(depth, b, t // page, page, rope_dim),
            st_p.reshape(depth, b, n_rh, dk, dv),
            lat_s.reshape(depth, bd, tn, kv_rank),
            rope_s.reshape(depth, bd, tn, rope_dim),
            st_s.reshape(depth, bd, n_rh, dk, dv))
```

```python
import functools
import math

import numpy as np
import jax
import jax.numpy as jnp
from jax import lax
from jax.experimental import pallas as pl
from jax.experimental.pallas import tpu as pltpu

F32 = jnp.float32
BF16 = jnp.bfloat16

ROPE_BASE = 10000.0
EPS = 1e-6
RET_CHUNK = 128
LANES = 128
HEAD_PAD = LANES
INPROJ_ROW_TILE = 512
MERGE_ROW_TILE = 512
MOE_ROW_TILE = 256
RET_SAMPLE_SEQS = 8
DECODE_PAGES = 64
DECODE_SLOTS = 3
ATTN_TQ = 512
ATTN_HEADS_PER_STEP = 8
VMEM_LIMIT = 48 * 1024 * 1024


def _cparams(n_axes):
    return pltpu.CompilerParams(dimension_semantics=("arbitrary",) * n_axes,
                                vmem_limit_bytes=VMEM_LIMIT)


def _const_spec(shape):
    nd = len(shape)
    return pl.BlockSpec(shape, lambda *_: (0,) * nd, pipeline_mode=pl.Buffered(1))


def _rms(x, g):
    return x * lax.rsqrt(jnp.mean(x * x, axis=-1, keepdims=True) + EPS) * g


def _sigmoid(x):
    return 1.0 / (1.0 + jnp.exp(-x))


def _dot(a, b):
    return jnp.dot(a, b, preferred_element_type=F32)


def _dot_nt(a, b):
    return lax.dot_general(a, b, (((1,), (1,)), ((), ())), preferred_element_type=F32)


def _retention_chunk(q, k, v, gate, state, i, mask_ref, qd_ref, kd_ref, gn_ref, chunk_decay):
    s = _dot_nt(q, k) * mask_ref[i]
    o = _dot(s.astype(BF16), v) + _dot(q, state.astype(BF16)) * qd_ref[i]
    kd = (k.astype(F32) * kd_ref[i]).T.astype(BF16)
    new_state = state * chunk_decay[i] + _dot(kd, v)
    return (_rms(o, gn_ref[i]) * gate.astype(F32)).astype(BF16), new_state


def _inproj_kernel(x_ref, gmix_ref, wmain_ref, wsmall_ref, gq_ref, gkv_ref, wq_ref, wqsw_ref,
                   wk_ref, wv_ref, cosr_ref, sinr_ref, cp_ref, sp_ref, *rest,
                   n_rh, dk, dv, q_rank, kv_rank, rope_dim, n_mh, nope, dv_m, k_scale, q_scale,
                   ret_chunk, blocks_per_seq, chunk_decay):
    if ret_chunk:
        (mask_ref, qd_ref, kd_ref, gn_ref, reto_o, st_o,
         gret_o, gmla_o, lat_o, rope_o, q_o, k_o, v_o, s_scr) = rest
    else:
        rq_o, rk_o, rv_o, rgs_o, gret_o, gmla_o, lat_o, rope_o, q_o, k_o, v_o = rest
    x = x_ref[...]
    h = _rms(x, gmix_ref[...]).astype(BF16)
    cosr = cosr_ref[...]
    sinr = sinr_ref[...]
    qk_w = n_rh * dk
    v_w = n_rh * dv

    def rope_heads(z, scale):
        outs = []
        for i in range(n_rh):
            seg = z[:, i * dk:(i + 1) * dk]
            rot = seg * cosr + pltpu.roll(seg, dk // 2, 1) * sinr
            outs.append(rot if scale is None else rot * scale)
        return jnp.concatenate(outs, axis=1)

    off = 0
    rq = rope_heads(_dot(h, wmain_ref[:, off:off + qk_w]), None).astype(BF16)
    off += qk_w
    rk = rope_heads(_dot(h, wmain_ref[:, off:off + qk_w]), k_scale).astype(BF16)
    off += qk_w
    rv = _dot(h, wmain_ref[:, off:off + v_w]).astype(BF16)
    off += v_w
    rg = _dot(h, wmain_ref[:, off:off + v_w])
    rgs = (rg * _sigmoid(rg)).astype(BF16)
    off += v_w
    if ret_chunk:
        blk = pl.program_id(0) % blocks_per_seq

        @pl.when(blk == 0)
        def _():
            s_scr[...] = jnp.zeros_like(s_scr)

        for c in range(x.shape[0] // ret_chunk):
            rows = slice(c * ret_chunk, (c + 1) * ret_chunk)
            for i in range(n_rh):
                qs, vs = slice(i * dk, (i + 1) * dk), slice(i * dv, (i + 1) * dv)
                o, s_scr[i] = _retention_chunk(rq[rows, qs], rk[rows, qs], rv[rows, vs], rgs[rows, vs],
                                               s_scr[i], i, mask_ref, qd_ref, kd_ref, gn_ref, chunk_decay)
                reto_o[rows, vs] = o

        @pl.when(blk == blocks_per_seq - 1)
        def _():
            st_o[0] = s_scr[...]
    else:
        rq_o[...] = rq
        rk_o[...] = rk
        rv_o[...] = rv
        rgs_o[...] = rgs
    d_model = x.shape[1]
    gret_o[...] = _sigmoid(_dot(h, wmain_ref[:, off:off + d_model])).astype(BF16)
    off += d_model
    gmla_o[...] = _sigmoid(_dot(h, wmain_ref[:, off:off + d_model])).astype(BF16)

    zs = _dot(h, wsmall_ref[...])
    cq = zs[:, :q_rank]
    ckv = zs[:, q_rank:q_rank + kv_rank]
    krp = zs[:, q_rank + kv_rank:q_rank + kv_rank + HEAD_PAD]
    krsw = zs[:, q_rank + kv_rank + HEAD_PAD:]
    cp = cp_ref[...]
    sp = sp_ref[...]
    cqn = _rms(cq, gq_ref[...]).astype(BF16)
    ckvn = _rms(ckv, gkv_ref[...])
    lat_o[...] = ckvn
    ckvb = ckvn.astype(BF16)
    kr_rot = krp * cp + krsw * sp
    rope_o[...] = pltpu.roll(kr_rot, HEAD_PAD - nope, 1)[:, :rope_dim]

    qh = _dot(cqn, wq_ref[...])
    qs = _dot(cqn, wqsw_ref[...])
    kh = _dot(ckvb, wk_ref[...])
    vh = _dot(ckvb, wv_ref[...])
    sum_lane = lax.broadcasted_iota(jnp.int32, cp.shape, 1) == dv_m
    for i in range(n_mh):
        sl = slice(i * HEAD_PAD, (i + 1) * HEAD_PAD)
        q_o[i] = ((qh[:, sl] * cp + qs[:, sl] * sp) * q_scale).astype(BF16)
        k_o[i] = (kh[:, sl] + kr_rot).astype(BF16)
        v_o[i] = jnp.where(sum_lane, 1.0, vh[:, sl]).astype(BF16)


def _inproj(x, tables, wts, dims, tm, ret_gn=None, n_seq=1):
    n, d_model = x.shape
    cosr, sinr, cp, sp = tables
    tab_blocks = cosr.shape[0] // tm
    n_rh, dk, dv, q_rank, kv_rank, rope_dim, n_mh, nope, dv_m = dims
    grid = (n // tm,)
    tok = lambda w: pl.BlockSpec((tm, w), lambda i: (i, 0))
    tab = lambda w: pl.BlockSpec((tm, w), lambda i: (i % tab_blocks, 0))
    head = pl.BlockSpec((n_mh, tm, HEAD_PAD), lambda i: (0, i, 0))
    w_names = ("gmix", "wmain", "wsmall", "gq", "gkv", "wq", "wqsw", "wk", "wv")
    args = [x] + [wts[k] for k in w_names] + [cosr, sinr, cp, sp]
    in_specs = ([tok(d_model)] + [_const_spec(wts[k].shape) for k in w_names]
                + [tab(dk), tab(dk), tab(HEAD_PAD), tab(HEAD_PAD)])
    qk_w, v_w = n_rh * dk, n_rh * dv
    sds = jax.ShapeDtypeStruct
    common_shape = (sds((n, d_model), BF16), sds((n, d_model), BF16), sds((n, kv_rank), F32),
                    sds((n, rope_dim), F32), sds((n_mh, n, HEAD_PAD), BF16), sds((n_mh, n, HEAD_PAD), BF16),
                    sds((n_mh, n, HEAD_PAD), BF16))
    common_specs = (tok(d_model), tok(d_model), tok(kv_rank), tok(rope_dim), head, head, head)
    scratch = []
    if ret_gn is None:
        ret_chunk, blocks_per_seq, cd = 0, 1, None
        out_shape = (sds((n, qk_w), BF16), sds((n, qk_w), BF16), sds((n, v_w), BF16),
                     sds((n, v_w), BF16)) + common_shape
        out_specs = (tok(qk_w), tok(qk_w), tok(v_w), tok(v_w)) + common_specs
    else:
        ret_chunk = min(RET_CHUNK, tm)
        blocks_per_seq = n // n_seq // tm
        mask, qd, kd, cd, _ = _ret_tables(n_rh, ret_chunk, dk, dv)
        tabs = [mask, qd, kd, ret_gn.reshape(n_rh, 1, dv)]
        args += tabs
        in_specs += [_const_spec(a.shape) for a in tabs]
        out_shape = (sds((n, v_w), BF16), sds((n_seq, n_rh, dk, dv), F32)) + common_shape
        out_specs = (tok(v_w), pl.BlockSpec((1, n_rh, dk, dv), lambda i: (i // blocks_per_seq, 0, 0, 0))
                     ) + common_specs
        scratch = [pltpu.VMEM((n_rh, dk, dv), F32)]
    kern = functools.partial(_inproj_kernel, n_rh=n_rh, dk=dk, dv=dv, q_rank=q_rank, kv_rank=kv_rank,
                             rope_dim=rope_dim, n_mh=n_mh, nope=nope, dv_m=dv_m, k_scale=dk ** -0.5,
                             q_scale=(nope + rope_dim) ** -0.5 * math.log2(math.e),
                             ret_chunk=ret_chunk, blocks_per_seq=blocks_per_seq, chunk_decay=cd)
    return pl.pallas_call(kern, out_shape=out_shape, grid=grid, in_specs=in_specs, out_specs=out_specs,
                          scratch_shapes=scratch, compiler_params=_cparams(1), name="inproj")(*args)


def _ret_tables(n_rh, chunk, dk, dv):
    log_g = np.log1p(-np.exp2(-5.0 - np.arange(n_rh, dtype=np.float64)))
    idx = np.arange(chunk, dtype=np.float64)
    diff = idx[:, None] - idx[None, :]
    mask = np.where(diff >= 0, np.exp(log_g[:, None, None] * np.maximum(diff, 0.0)), 0.0)
    qd = np.exp(log_g[:, None] * (idx + 1.0))
    kd = np.exp(log_g[:, None] * (chunk - 1.0 - idx))
    cd = np.exp(log_g * chunk)
    qd = np.broadcast_to(qd[:, :, None], (n_rh, chunk, dv))
    kd = np.broadcast_to(kd[:, :, None], (n_rh, chunk, dk))
    return (jnp.asarray(mask, F32), jnp.asarray(qd, F32), jnp.asarray(kd, F32),
            tuple(float(np.float32(v)) for v in cd), tuple(float(np.float32(v)) for v in np.exp(log_g)))


def _ret_sample_kernel(qt_ref, kt_ref, v_ref, gate_ref, st_ref, gn_ref, o_ref, ns_ref,
                       *, n_rh, dk, dv, gamma):
    seqs = v_ref.shape[0]
    lane = lax.broadcasted_iota(jnp.int32, qt_ref.shape, 1)
    for j in range(seqs):
        sel = lane == pl.program_id(0) * seqs + j
        qcol = jnp.sum(jnp.where(sel, qt_ref[...], 0.0), axis=1, keepdims=True)
        kcol = jnp.sum(jnp.where(sel, kt_ref[...], 0.0), axis=1, keepdims=True)
        v = v_ref[j]
        gate = gate_ref[j]
        for i in range(n_rh):
            vi = v[:, i * dv:(i + 1) * dv]
            new = st_ref[j, i] * gamma[i] + kcol[i * dk:(i + 1) * dk] * vi
            ns_ref[j, i] = new
            o = jnp.sum(qcol[i * dk:(i + 1) * dk] * new, axis=0, keepdims=True)
            o_ref[j, :, i * dv:(i + 1) * dv] = _rms(o, gn_ref[i]) * gate[:, i * dv:(i + 1) * dv]


def _ret_sample(rq, rk, rv, rgs, state, gn, n_rh, dk, dv):
    nb = rq.shape[0]
    _, _, _, _, gamma = _ret_tables(n_rh, 1, dk, dv)
    qt = rq.astype(F32).T
    kt = rk.astype(F32).T
    seqs = math.gcd(nb, RET_SAMPLE_SEQS)
    row = lambda w: pl.BlockSpec((seqs, 1, w), lambda b: (b, 0, 0))
    st_spec = pl.BlockSpec((seqs, n_rh, dk, dv), lambda b: (b, 0, 0, 0))
    in_specs = [_const_spec(qt.shape), _const_spec(kt.shape), row(n_rh * dv), row(n_rh * dv), st_spec,
                _const_spec((n_rh, 1, dv))]
    out_shape = (jax.ShapeDtypeStruct((nb, 1, n_rh * dv), F32),
                 jax.ShapeDtypeStruct((nb, n_rh, dk, dv), F32))
    kern = functools.partial(_ret_sample_kernel, n_rh=n_rh, dk=dk, dv=dv, gamma=gamma)
    o, ns = pl.pallas_call(kern, out_shape=out_shape, grid=(nb // seqs,), in_specs=in_specs,
                           out_specs=(row(n_rh * dv), st_spec), compiler_params=_cparams(1),
                           name="retention_sample")(
        qt, kt, rv.astype(F32).reshape(nb, 1, -1), rgs.astype(F32).reshape(nb, 1, -1), state,
        gn.reshape(n_rh, 1, dv))
    return o.reshape(nb, n_rh * dv).astype(BF16), ns


def _attn_prompt_kernel(q_ref, k_ref, v_ref, o_ref, m_scr, acc_scr, *, tq, hpg, sum_lane):
    i = pl.program_id(2)
    m_scr[...] = jnp.full(m_scr.shape, -jnp.inf, F32)
    acc_scr[...] = jnp.zeros(acc_scr.shape, F32)

    def tile(j, masked):
        start = pl.multiple_of(j * tq, tq)
        for h in range(hpg):
            s = _dot_nt(q_ref[h], k_ref[h, pl.ds(start, tq), :])
            if masked:
                row = lax.broadcasted_iota(jnp.int32, s.shape, 0)
                col = lax.broadcasted_iota(jnp.int32, s.shape, 1)
                s = jnp.where(col <= row, s, -jnp.inf)
            m = m_scr[h]
            m_new = jnp.maximum(m, jnp.max(s, axis=1, keepdims=True))
            m_wide = jnp.concatenate([m_new] * (tq // HEAD_PAD), axis=1)
            p = jnp.exp2((s - m_wide).astype(BF16))
            acc_scr[h] = jnp.exp2(m - m_new) * acc_scr[h] + _dot(p, v_ref[h, pl.ds(start, tq), :])
            m_scr[h] = m_new

    def body(j, carry):
        tile(j, False)
        return carry

    lax.fori_loop(0, i, body, 0)
    tile(i, True)
    for h in range(hpg):
        acc = acc_scr[h]
        o = acc / acc[:, sum_lane:sum_lane + 1]
        o_ref[:, h * sum_lane:(h + 1) * sum_lane] = o[:, :sum_lane].astype(BF16)


def _attn_prompt(q, k, v, b, t, tq, hpg, sum_lane):
    n_mh, n, _ = q.shape
    nq = t // tq
    kv_spec = pl.BlockSpec((hpg, t, HEAD_PAD), lambda bi, g, i: (g, bi, 0), pipeline_mode=pl.Buffered(1))
    in_specs = [pl.BlockSpec((hpg, tq, HEAD_PAD), lambda bi, g, i: (g, bi * nq + i, 0)), kv_spec, kv_spec]
    out_specs = pl.BlockSpec((tq, hpg * sum_lane), lambda bi, g, i: (bi * nq + i, g))
    return pl.pallas_call(functools.partial(_attn_prompt_kernel, tq=tq, hpg=hpg, sum_lane=sum_lane),
                          out_shape=jax.ShapeDtypeStruct((n, n_mh * sum_lane), BF16),
                          grid=(b, n_mh // hpg, nq), in_specs=in_specs, out_specs=out_specs,
                          scratch_shapes=[pltpu.VMEM((hpg, tq, HEAD_PAD), F32),
                                          pltpu.VMEM((hpg, tq, HEAD_PAD), F32)],
                          compiler_params=_cparams(3), name="attn_prompt")(q, k, v)


def _attn_sample_kernel(pt_ref, ql_ref, qp_ref, cn_ref, rn_ref, lat_hbm, rope_hbm, o_ref,
                        lat_slab, rope_slab, sems, m_scr, l_scr, acc_scr, *, npg, n_chunks, page):
    step = pl.program_id(0)
    n_steps = pl.num_programs(0)
    n_slots = lat_slab.shape[0]
    ahead = n_slots - 1
    slot = step % n_slots
    chunk = step % n_chunks

    def page_copies(page_ids, slot_):
        out = []
        for i in range(npg):
            rows = pl.ds(i * page, page)
            out.append(pltpu.make_async_copy(lat_hbm.at[page_ids(i)], lat_slab.at[slot_, rows, :],
                                             sems.at[0, slot_]))
            out.append(pltpu.make_async_copy(rope_hbm.at[page_ids(i)], rope_slab.at[slot_, :, rows],
                                             sems.at[1, slot_]))
        return out

    def start_step(step_, slot_):
        seq = step_ // n_chunks
        first = (step_ % n_chunks) * npg
        for cp in page_copies(lambda i: pt_ref[seq, first + i], slot_):
            cp.start()

    for k in range(ahead):
        @pl.when(jnp.logical_and(step == 0, k < n_steps))
        def _():
            start_step(k, k)

    @pl.when(step + ahead < n_steps)
    def _():
        start_step(step + ahead, (step + ahead) % n_slots)

    for cp in page_copies(lambda i: 0, slot):
        cp.wait()

    ql = ql_ref[0]
    qp = qp_ref[0]

    @pl.when(chunk == 0)
    def _():
        cn = cn_ref[0]
        s0 = (jnp.sum(ql * cn, axis=1, keepdims=True)
              + jnp.sum(qp * rn_ref[0], axis=1, keepdims=True))
        m_scr[...] = s0
        l_scr[...] = jnp.ones_like(l_scr)
        acc_scr[...] = jnp.broadcast_to(cn, acc_scr.shape)

    lat = lat_slab[slot].astype(BF16)
    s = _dot_nt(ql.astype(BF16), lat) + _dot(qp, rope_slab[slot])
    m_prev = m_scr[...]
    m_new = jnp.maximum(m_prev, jnp.max(s, axis=1, keepdims=True))
    alpha = jnp.exp2(m_prev - m_new)
    p = jnp.exp2(s - m_new)
    m_scr[...] = m_new
    l_scr[...] = alpha * l_scr[...] + jnp.sum(p, axis=1, keepdims=True)
    acc_scr[...] = alpha * acc_scr[...] + _dot(p.astype(BF16), lat)

    @pl.when(chunk == n_chunks - 1)
    def _():
        o_ref[0] = acc_scr[...] / l_scr[...]


def _attn_sample(page_table, q_lat, q_pe, c_new, r_new, cache_lat, cache_rope_t, npg):
    nb, n_mh, kv_rank = q_lat.shape
    rope_dim = q_pe.shape[-1]
    n_pages = page_table.shape[1]
    page = cache_lat.shape[1]
    n_chunks = n_pages // npg
    per_seq = lambda shp: pl.BlockSpec((1,) + shp, lambda s, pt: (s // n_chunks, 0, 0))
    hbm = pl.BlockSpec(memory_space=pl.ANY)
    in_specs = [per_seq((n_mh, kv_rank)), per_seq((n_mh, rope_dim)), per_seq((1, kv_rank)),
                per_seq((1, rope_dim)), hbm, hbm]
    grid_spec = pltpu.PrefetchScalarGridSpec(
        num_scalar_prefetch=1, grid=(nb * n_chunks,), in_specs=in_specs,
        out_specs=per_seq((n_mh, kv_rank)),
        scratch_shapes=[pltpu.VMEM((DECODE_SLOTS, npg * page, kv_rank), F32),
                        pltpu.VMEM((DECODE_SLOTS, rope_dim, npg * page), F32),
                        pltpu.SemaphoreType.DMA((2, DECODE_SLOTS)),
                        pltpu.VMEM((n_mh, 1), F32), pltpu.VMEM((n_mh, 1), F32),
                        pltpu.VMEM((n_mh, kv_rank), F32)])
    kern = functools.partial(_attn_sample_kernel, npg=npg, n_chunks=n_chunks, page=page)
    return pl.pallas_call(kern, out_shape=jax.ShapeDtypeStruct((nb, n_mh, kv_rank), F32),
                          grid_spec=grid_spec, compiler_params=_cparams(1), name="attn_sample")(
        page_table, q_lat, q_pe, c_new, r_new, cache_lat, cache_rope_t)


def _headmm_kernel(a_ref, w_ref, o_ref):
    o_ref[0] = _dot(a_ref[0].astype(BF16), w_ref[0]).astype(o_ref.dtype)


def _headmm(a, w, out_dtype):
    nh, m, kk = a.shape
    nn = w.shape[-1]
    spec = lambda r, c: pl.BlockSpec((1, r, c), lambda h: (h, 0, 0))
    return pl.pallas_call(_headmm_kernel, out_shape=jax.ShapeDtypeStruct((nh, m, nn), out_dtype),
                          grid=(nh,), in_specs=[spec(m, kk), spec(kk, nn)], out_specs=spec(m, nn),
                          compiler_params=_cparams(1), name="head_matmul")(a, w)


def _router_logits(h2, whi_ref, wlo_ref):
    hi = h2.astype(BF16)
    lo = (h2 - hi.astype(F32)).astype(BF16)
    lg = _dot(hi, whi_ref[...]) + (_dot(hi, wlo_ref[...]) + _dot(lo, whi_ref[...]))
    return hi, lg[:, :LANES], lg[:, LANES:]


def _merge_kernel(x_ref, ro_ref, gr_ref, mo_ref, gm_ref, wro_ref, wmo_ref, wout_ref, gffn_ref,
                  whi_ref, wlo_ref, x1_o, gid_o, *, group_size, n_exp):
    mixed = (_dot(ro_ref[...], wro_ref[...]) * gr_ref[...].astype(F32)
             + _dot(mo_ref[...], wmo_ref[...]) * gm_ref[...].astype(F32))
    x1 = x_ref[...] + _dot(mixed.astype(BF16), wout_ref[...])
    x1_o[...] = x1
    _, ge, _ = _router_logits(_rms(x1, gffn_ref[...]), whi_ref, wlo_ref)
    lane = lax.broadcasted_iota(jnp.int32, ge.shape, 1)
    ge = jnp.where(lane < n_exp, ge, -jnp.inf)
    grp = (lane // group_size).astype(F32)
    gmax = jnp.max(ge, axis=1, keepdims=True)
    g_idx = jnp.min(jnp.where(ge == gmax, grp, float(n_exp)), axis=1, keepdims=True)
    gid_o[...] = jnp.broadcast_to(g_idx, ge.shape).astype(jnp.int32)


def _merge(x, ro, gr, mo, gm, wts, group_size, n_exp, tm):
    n, d_model = x.shape
    tok = lambda w: pl.BlockSpec((tm, w), lambda i: (i, 0))
    w_args = [wts[k] for k in ("wro", "wmo", "wout", "gffn", "wr_hi", "wr_lo")]
    in_specs = [tok(d_model), tok(ro.shape[1]), tok(d_model), tok(mo.shape[1]), tok(d_model)] + [
        _const_spec(a.shape) for a in w_args]
    out_shape = (jax.ShapeDtypeStruct((n, d_model), F32), jax.ShapeDtypeStruct((n, LANES), jnp.int32))
    kern = functools.partial(_merge_kernel, group_size=group_size, n_exp=n_exp)
    return pl.pallas_call(kern, out_shape=out_shape, grid=(n // tm,), in_specs=in_specs,
                          out_specs=(tok(d_model), tok(LANES)),
                          compiler_params=_cparams(1), name="merge_router")(x, ro, gr, mo, gm, *w_args)


def _group_tiles(gid, n_grp, tmr):
    n = gid.shape[0]
    token = jnp.arange(n, dtype=jnp.int32)
    counts = jnp.sum((gid[:, None] == jnp.arange(n_grp, dtype=jnp.int32)[None, :]).astype(jnp.int32), axis=0)
    by_group = jnp.sort(gid * n + token) % n
    dense_start = jnp.cumsum(counts) - counts
    padded = ((counts + tmr - 1) // tmr) * tmr
    ends = jnp.cumsum(padded)
    offs = ends - padded
    n_tiles = n // tmr + n_grp
    tile_start = jnp.arange(n_tiles, dtype=jnp.int32) * tmr
    tile_group = jnp.minimum(jnp.sum((tile_start[:, None] >= ends[None, :]).astype(jnp.int32), axis=1),
                             n_grp - 1)
    tile_valid = jnp.clip(counts[tile_group] - (tile_start - offs[tile_group]), 0, tmr)
    row_group = jnp.repeat(tile_group, tmr)
    in_group = jnp.arange(n_tiles * tmr, dtype=jnp.int32) - offs[row_group]
    src = jnp.clip(dense_start[row_group] + in_group, 0, n - 1)
    row_token = jnp.where(in_group < counts[row_group], by_group[src], 0)
    return row_token, tile_group, tile_valid.astype(jnp.int32)


def _moe_kernel(rt_ref, tg_ref, tv_ref, x1_hbm, gffn_ref, gfin_ref, whi_ref, wlo_ref, wg_ref, wu_ref, wd_ref,
                y_hbm, xbuf, obuf, trash, gsem, ssem, *, tmr, group_size, n_exp):
    t = pl.program_id(0)
    last = pl.num_programs(0) - 1
    slot = t % 2

    def gather_copy(tok, r, slot_):
        return pltpu.make_async_copy(x1_hbm.at[pl.ds(tok, 1)], xbuf.at[slot_, pl.ds(r, 1)], gsem.at[slot_])

    def start_gather(tile, slot_):
        for r in range(tmr):
            gather_copy(rt_ref[tile * tmr + r], r, slot_).start()

    def wait_gather(slot_):
        for r in range(tmr):
            gather_copy(0, r, slot_).wait()

    def scatter_copy(r, dst_row_ref, slot_):
        return pltpu.make_async_copy(obuf.at[slot_, pl.ds(r, 1)], dst_row_ref, ssem.at[slot_])

    def wait_scatter(slot_):
        for r in range(tmr):
            scatter_copy(r, trash.at[slot_, pl.ds(r, 1)], slot_).wait()

    @pl.when(t == 0)
    def _():
        start_gather(t, slot)

    wait_gather(slot)

    @pl.when(jnp.logical_and(t >= 2, tv_ref[jnp.maximum(t - 2, 0)] > 0))
    def _():
        wait_scatter(slot)

    start_gather(jnp.minimum(t + 1, last), 1 - slot)

    n_real = tv_ref[t]

    @pl.when(n_real > 0)
    def _():
        g = tg_ref[t]
        x = xbuf[slot]
        hb, ge, el = _router_logits(_rms(x, gffn_ref[...]), whi_ref, wlo_ref)
        lane_i = lax.broadcasted_iota(jnp.int32, ge.shape, 1)
        lane = lane_i.astype(F32)
        is_exp = lane_i < n_exp
        in_grp = (lane_i // group_size) == g
        gmax = jnp.max(jnp.where(is_exp, ge, -jnp.inf), axis=1, keepdims=True)
        gsum = jnp.sum(jnp.where(is_exp, jnp.exp(ge - gmax), 0.0), axis=1, keepdims=True) / group_size
        ge_g = jnp.max(jnp.where(in_grp, ge, -jnp.inf), axis=1, keepdims=True)
        g_w = jnp.exp(ge_g - gmax) / gsum
        e_in = jnp.where(in_grp, el, -jnp.inf)
        top1 = jnp.max(e_in, axis=1, keepdims=True)
        idx1 = jnp.min(jnp.where(e_in == top1, lane, float(LANES)), axis=1, keepdims=True)
        e_rest = jnp.where(lane == idx1, -jnp.inf, e_in)
        top2 = jnp.max(e_rest, axis=1, keepdims=True)
        idx2 = jnp.min(jnp.where(e_rest == top2, lane, float(LANES)), axis=1, keepdims=True)
        e2 = jnp.exp(top2 - top1)
        comb = (jnp.where(lane == idx1, g_w / (1.0 + e2), 0.0)
                + jnp.where(lane == idx2, g_w * e2 / (1.0 + e2), 0.0))

        first = (g * group_size).astype(F32)
        parts = []
        for e in range(group_size):
            ce = jnp.sum(jnp.where(lane == first + e, comb, 0.0), axis=1, keepdims=True)
            hg = _dot(hb, wg_ref[e])
            parts.append((hg * _sigmoid(hg) * _dot(hb, wu_ref[e]) * ce).astype(BF16))
        y = x + _dot(jnp.concatenate(parts, axis=1), wd_ref[0])
        obuf[slot] = _rms(y, gfin_ref[...])

    def scatter_real(r, carry):
        tok = rt_ref[t * tmr + r]
        scatter_copy(r, y_hbm.at[pl.ds(tok, 1)], slot).start()
        return carry

    def scatter_pad(r, carry):
        scatter_copy(r, trash.at[slot, pl.ds(r, 1)], slot).start()
        return carry

    @pl.when(n_real == tmr)
    def _():
        for r in range(tmr):
            scatter_real(r, 0)

    @pl.when(jnp.logical_and(n_real > 0, n_real < tmr))
    def _():
        lax.fori_loop(0, n_real, scatter_real, 0)
        lax.fori_loop(n_real, tmr, scatter_pad, 0)

    @pl.when(t == last)
    def _():
        wait_gather(1 - slot)

        @pl.when(n_real > 0)
        def _():
            wait_scatter(slot)

        @pl.when(jnp.logical_and(t >= 1, tv_ref[jnp.maximum(t - 1, 0)] > 0))
        def _():
            wait_scatter(1 - slot)


def _moe(x1, gid, wts, wg, wu, wd, gfin, group_size, n_exp, tmr):
    n, d_model = x1.shape
    n_grp = wd.shape[0]
    row_token, tile_group, tile_valid = _group_tiles(gid, n_grp, tmr)
    n_tiles = tile_group.shape[0]
    hbm = pl.BlockSpec(memory_space=pl.ANY)
    const = lambda a: pl.BlockSpec(a.shape, lambda t, rt, tg, tv: (0,) * a.ndim, pipeline_mode=pl.Buffered(1))
    grp_w = lambda a: pl.BlockSpec((a.shape[0] // n_grp,) + a.shape[1:], lambda t, rt, tg, tv: (tg[t], 0, 0))
    w_args = [wts["gffn"], gfin, wts["wr_hi"], wts["wr_lo"]]
    grid_spec = pltpu.PrefetchScalarGridSpec(
        num_scalar_prefetch=3, grid=(n_tiles,),
        in_specs=[hbm] + [const(a) for a in w_args] + [grp_w(wg), grp_w(wu), grp_w(wd)],
        out_specs=hbm,
        scratch_shapes=[pltpu.VMEM((2, tmr, d_model), F32), pltpu.VMEM((2, tmr, d_model), F32),
                        pltpu.VMEM((2, tmr, d_model), F32),
                        pltpu.SemaphoreType.DMA((2,)), pltpu.SemaphoreType.DMA((2,))])
    kern = functools.partial(_moe_kernel, tmr=tmr, group_size=group_size, n_exp=n_exp)
    return pl.pallas_call(kern, out_shape=jax.ShapeDtypeStruct((n, d_model), F32), grid_spec=grid_spec,
                          compiler_params=_cparams(1), name="moe")(
        row_token, tile_group, tile_valid, x1, *w_args, wg, wu, wd)


def _rope_tables(start, count, repeat, dk, rope_dim, nope):
    n_hi = -(-count // LANES)

    def cs(half):
        inv = ROPE_BASE ** (-jnp.arange(half, dtype=F32) / half)
        a = (start + LANES * jnp.arange(n_hi, dtype=F32))[:, None, None] * inv
        b = jnp.arange(LANES, dtype=F32)[None, :, None] * inv
        ca, sa, cb, sb = jnp.cos(a), jnp.sin(a), jnp.cos(b), jnp.sin(b)
        c = (ca * cb - sa * sb).reshape(n_hi * LANES, half)[:count]
        s = (sa * cb + ca * sb).reshape(n_hi * LANES, half)[:count]
        return jnp.repeat(c, repeat, axis=0), jnp.repeat(s, repeat, axis=0)

    c, s = cs(dk // 2)
    cosr = jnp.concatenate([c, c], axis=1)
    sinr = jnp.concatenate([-s, s], axis=1)
    c, s = cs(rope_dim // 2)
    n = count * repeat
    tail = jnp.zeros((n, HEAD_PAD - nope - rope_dim), F32)
    cp = jnp.concatenate([jnp.ones((n, nope), F32), c, c, tail], axis=1)
    sp = jnp.concatenate([jnp.zeros((n, nope), F32), -s, s, tail], axis=1)
    return cosr, sinr, cp, sp


def _swap_halves(w):
    half = w.shape[-1] // 2
    return jnp.concatenate([w[..., half:], w[..., :half]], axis=-1)


def _layout_weights(norm_mix_g, w_in, q_norm_g, kv_norm_g, w_uq, w_ukv, w_ret_o, w_mla_o, w_out,
                    norm_ffn_g, w_router_group, w_router_expert, dims):
    n_rh, dk, dv, q_rank, kv_rank, rope_dim, n_mh, nope, dv_m = dims
    d_model = w_in.shape[0]
    qk_w, v_w = n_rh * dk, n_rh * dv
    o_cq = 2 * qk_w + 2 * v_w
    o_ckv = o_cq + q_rank
    o_kr = o_ckv + kv_rank
    o_gate = o_kr + rope_dim
    pad_tail = HEAD_PAD - nope - rope_dim
    zeros = lambda *shape: jnp.zeros(shape, F32)
    w_kr = w_in[:, o_kr:o_gate]
    place = lambda w: jnp.concatenate([zeros(d_model, nope), w, zeros(d_model, pad_tail)], axis=1)
    wsmall = jnp.concatenate([w_in[:, o_cq:o_kr], place(w_kr), place(_swap_halves(w_kr))], axis=1)
    wmain = jnp.concatenate([w_in[:, :o_cq], w_in[:, o_gate:]], axis=1)

    uq = w_uq.reshape(q_rank, n_mh, nope + rope_dim)
    uq_rope = uq[..., nope:]
    wq = jnp.concatenate([uq, zeros(q_rank, n_mh, pad_tail)], axis=-1)
    wqsw = jnp.concatenate([zeros(q_rank, n_mh, nope), _swap_halves(uq_rope),
                            zeros(q_rank, n_mh, pad_tail)], axis=-1)
    uk = w_ukv[..., :nope]
    uv = w_ukv[..., nope:]
    wk = jnp.concatenate([uk, zeros(kv_rank, n_mh, HEAD_PAD - nope)], axis=-1)
    wv = jnp.concatenate([uv, zeros(kv_rank, n_mh, HEAD_PAD - dv_m)], axis=-1)
    flat = lambda w: w.reshape(w.shape[0], n_mh * HEAD_PAD).astype(BF16)
    wmo = w_mla_o
    wabs = jnp.concatenate([jnp.transpose(uk, (1, 2, 0)), zeros(n_mh, HEAD_PAD - nope, kv_rank)], axis=1)
    wvh = jnp.transpose(uv, (1, 0, 2))
    n_grp = w_router_group.shape[1]
    n_exp = w_router_expert.shape[1]
    group_size = n_exp // n_grp
    assert n_exp <= LANES
    w_r = jnp.concatenate([jnp.repeat(w_router_group, group_size, axis=1), zeros(d_model, LANES - n_exp),
                           w_router_expert, zeros(d_model, LANES - n_exp)], axis=1)
    wr_hi = w_r.astype(BF16)
    wr_lo = (w_r - wr_hi.astype(F32)).astype(BF16)
    return dict(
        gmix=norm_mix_g.reshape(1, -1), wmain=wmain.astype(BF16), wsmall=wsmall.astype(BF16),
        gq=q_norm_g.reshape(1, -1), gkv=kv_norm_g.reshape(1, -1),
        wq=flat(wq), wqsw=flat(wqsw), wk=flat(wk), wv=flat(wv),
        wro=w_ret_o.astype(BF16), wmo=wmo.astype(BF16), wout=w_out.astype(BF16),
        gffn=norm_ffn_g.reshape(1, -1), wr_hi=wr_hi, wr_lo=wr_lo,
        wabs=wabs.astype(BF16), wvh=wvh.astype(BF16)), group_size, n_exp


def kernel(x_prompt, x_sample, cache_kv_latent, cache_k_rope, state_retention, page_table, norm_mix_g,
           w_in, q_norm_g, kv_norm_g, w_uq, w_ukv, ret_gn_g, w_ret_o, w_mla_o, w_out, norm_ffn_g,
           w_router_group, w_router_expert, w_gate, w_up, w_down, norm_final_g):
    depth = w_in.shape[0]
    assert depth == 1, "single-layer step only"
    b, t, d_model = x_prompt.shape
    bd, tn, _ = x_sample.shape
    assert tn == 1, "one new token per sample sequence"
    _, _, n_rh, dk, dv = state_retention.shape
    kv_rank = cache_kv_latent.shape[-1]
    rope_dim = cache_k_rope.shape[-1]
    page = cache_kv_latent.shape[2]
    n_mh = w_ukv.shape[2]
    q_rank = w_uq.shape[1]
    nope = w_uq.shape[2] // n_mh - rope_dim
    dv_m = w_ukv.shape[3] - nope
    assert nope + rope_dim <= HEAD_PAD and dv_m < HEAD_PAD
    dims = (n_rh, dk, dv, q_rank, kv_rank, rope_dim, n_mh, nope, dv_m)
    past_len = page_table.shape[1] * page

    wts, group_size, n_exp = _layout_weights(norm_mix_g[0], w_in[0], q_norm_g[0], kv_norm_g[0], w_uq[0],
                                             w_ukv[0], w_ret_o[0], w_mla_o[0], w_out[0], norm_ffn_g[0],
                                             w_router_group[0], w_router_expert[0], dims)
    gn = ret_gn_g[0]
    n_grp = n_exp // group_size
    d_exp = w_gate.shape[-1]
    wg, wu = w_gate[0].astype(BF16), w_up[0].astype(BF16)
    wd = w_down[0].astype(BF16).reshape(n_grp, group_size * d_exp, d_model)
    gfin = norm_final_g.reshape(1, -1)

    tm_p = min(INPROJ_ROW_TILE, t)
    xp = x_prompt.reshape(b * t, d_model)
    tabs_p = _rope_tables(0.0, t, 1, dk, rope_dim, nope)
    (ret_o, st_p, gret, gmla, lat_p, rope_p, q_p, k_p, v_p) = _inproj(xp, tabs_p, wts, dims, tm_p,
                                                                      ret_gn=gn, n_seq=b)
    mla_o = _attn_prompt(q_p, k_p, v_p, b, t, min(ATTN_TQ, t), min(ATTN_HEADS_PER_STEP, n_mh), dv_m)
    x1, gid = _merge(xp, ret_o, gret, mla_o, gmla, wts, group_size, n_exp,
                     min(MERGE_ROW_TILE, t))
    y_prompt = _moe(x1, gid[:, 0], wts, wg, wu, wd, gfin, group_size, n_exp,
                    min(MOE_ROW_TILE, b * t)).reshape(b, t, d_model)

    xs = x_sample.reshape(bd, d_model)
    tabs_s = _rope_tables(float(past_len), tn, bd, dk, rope_dim, nope)
    (rq, rk, rv, rgs, gret, gmla, lat_s, rope_s, q_s, _, _) = _inproj(xs, tabs_s, wts, dims, bd)
    ret_o_s, st_s = _ret_sample(rq, rk, rv, rgs, state_retention[0], gn, n_rh, dk, dv)
    q_lat = jnp.transpose(_headmm(q_s, wts["wabs"], F32), (1, 0, 2))
    q_pe = jnp.transpose(q_s[:, :, nope:nope + rope_dim].astype(F32), (1, 0, 2))
    o_lat = _attn_sample(page_table, q_lat, q_pe, lat_s.reshape(bd, 1, kv_rank),
                         rope_s.reshape(bd, 1, rope_dim), cache_kv_latent.reshape(-1, page, kv_rank),
                         jnp.transpose(cache_k_rope.reshape(-1, page, rope_dim), (0, 2, 1)),
                         min(DECODE_PAGES, page_table.shape[1]))
    mla_o_s = _headmm(jnp.transpose(o_lat, (1, 0, 2)), wts["wvh"], BF16)
    mla_o_s = jnp.transpose(mla_o_s, (1, 0, 2)).reshape(bd, n_mh * dv_m)
    x1, gid = _merge(xs, ret_o_s, gret, mla_o_s, gmla, wts, group_size, n_exp, bd)
    y_sample = _moe(x1, gid[:, 0], wts, wg, wu, wd, gfin, group_size, n_exp,
                    min(MOE_ROW_TILE, bd)).reshape(bd, tn, d_model)

    return (y_prompt, y_sample,
            lat_p.reshape(depth, b, t // page, page, kv_rank),
            rope_p.reshape(depth, b, t // page, page, rope_dim),
            st_p.reshape(depth, b, n_rh, dk, dv),
            lat_s.reshape(depth, bd, tn, kv_rank),
            rope_s.reshape(depth, bd, tn, rope_dim),
            st_s.reshape(depth, bd, n_rh, dk, dv))
```

```python
import functools
import math

import numpy as np
import jax
import jax.numpy as jnp
from jax import lax
from jax.experimental import pallas as pl
from jax.experimental.pallas import tpu as pltpu

F32 = jnp.float32
BF16 = jnp.bfloat16

ROPE_BASE = 10000.0
EPS = 1e-6
RET_CHUNK = 128
LANES = 128
HEAD_PAD = LANES
INPROJ_ROW_TILE = 512
MERGE_ROW_TILE = 512
MOE_ROW_TILE = 512
RET_SAMPLE_SEQS = 8
DECODE_PAGES = 64
DECODE_SLOTS = 3
ATTN_TQ = 512
ATTN_HEADS_PER_STEP = 8
VMEM_LIMIT = 48 * 1024 * 1024


def _cparams(n_axes):
    return pltpu.CompilerParams(dimension_semantics=("arbitrary",) * n_axes,
                                vmem_limit_bytes=VMEM_LIMIT)


def _const_spec(shape):
    nd = len(shape)
    return pl.BlockSpec(shape, lambda *_: (0,) * nd, pipeline_mode=pl.Buffered(1))


def _rms(x, g):
    return x * lax.rsqrt(jnp.mean(x * x, axis=-1, keepdims=True) + EPS) * g


def _sigmoid(x):
    return 1.0 / (1.0 + jnp.exp(-x))


def _dot(a, b):
    return jnp.dot(a, b, preferred_element_type=F32)


def _dot_nt(a, b):
    return lax.dot_general(a, b, (((1,), (1,)), ((), ())), preferred_element_type=F32)


def _retention_chunk(q, k, v, gate, state, i, mask_ref, qd_ref, kd_ref, gn_ref, chunk_decay):
    s = _dot_nt(q, k) * mask_ref[i]
    o = _dot(s.astype(BF16), v) + _dot(q, state.astype(BF16)) * qd_ref[i]
    kd = (k.astype(F32) * kd_ref[i]).T.astype(BF16)
    new_state = state * chunk_decay[i] + _dot(kd, v)
    return (_rms(o, gn_ref[i]) * gate.astype(F32)).astype(BF16), new_state


def _inproj_kernel(x_ref, gmix_ref, wmain_ref, wsmall_ref, gq_ref, gkv_ref, wq_ref, wqsw_ref,
                   wk_ref, wv_ref, cosr_ref, sinr_ref, cp_ref, sp_ref, *rest,
                   n_rh, dk, dv, q_rank, kv_rank, rope_dim, n_mh, nope, dv_m, k_scale, q_scale,
                   ret_chunk, blocks_per_seq, chunk_decay):
    if ret_chunk:
        (mask_ref, qd_ref, kd_ref, gn_ref, reto_o, st_o,
         gret_o, gmla_o, lat_o, rope_o, q_o, k_o, v_o, s_scr) = rest
    else:
        rq_o, rk_o, rv_o, rgs_o, gret_o, gmla_o, lat_o, rope_o, q_o, k_o, v_o = rest
    if ret_chunk:
        blk = pl.program_id(0) % blocks_per_seq

        @pl.when(blk == 0)
        def _():
            s_scr[...] = jnp.zeros_like(s_scr)
    x = x_ref[...]
    h = _rms(x, gmix_ref[...]).astype(BF16)
    cosr = cosr_ref[...]
    sinr = sinr_ref[...]
    qk_w = n_rh * dk
    v_w = n_rh * dv

    def rope_heads(z, scale):
        outs = []
        for i in range(n_rh):
            seg = z[:, i * dk:(i + 1) * dk]
            rot = seg * cosr + pltpu.roll(seg, dk // 2, 1) * sinr
            outs.append(rot if scale is None else rot * scale)
        return jnp.concatenate(outs, axis=1)

    off = 0
    rq = rope_heads(_dot(h, wmain_ref[:, off:off + qk_w]), None).astype(BF16)
    off += qk_w
    rk = rope_heads(_dot(h, wmain_ref[:, off:off + qk_w]), k_scale).astype(BF16)
    off += qk_w
    rv = _dot(h, wmain_ref[:, off:off + v_w]).astype(BF16)
    off += v_w
    rg = _dot(h, wmain_ref[:, off:off + v_w])
    rgs = (rg * _sigmoid(rg)).astype(BF16)
    off += v_w
    if ret_chunk:
        for c in range(x.shape[0] // ret_chunk):
            rows = slice(c * ret_chunk, (c + 1) * ret_chunk)
            for i in range(n_rh):
                qs, vs = slice(i * dk, (i + 1) * dk), slice(i * dv, (i + 1) * dv)
                o, s_scr[i] = _retention_chunk(rq[rows, qs], rk[rows, qs], rv[rows, vs], rgs[rows, vs],
                                               s_scr[i], i, mask_ref, qd_ref, kd_ref, gn_ref, chunk_decay)
                reto_o[rows, vs] = o
    else:
        rq_o[...] = rq
        rk_o[...] = rk
        rv_o[...] = rv
        rgs_o[...] = rgs
    d_model = x.shape[1]
    gret_o[...] = _sigmoid(_dot(h, wmain_ref[:, off:off + d_model])).astype(BF16)
    off += d_model
    gmla_o[...] = _sigmoid(_dot(h, wmain_ref[:, off:off + d_model])).astype(BF16)

    zs = _dot(h, wsmall_ref[...])
    cq = zs[:, :q_rank]
    ckv = zs[:, q_rank:q_rank + kv_rank]
    krp = zs[:, q_rank + kv_rank:q_rank + kv_rank + HEAD_PAD]
    krsw = zs[:, q_rank + kv_rank + HEAD_PAD:]
    cp = cp_ref[...]
    sp = sp_ref[...]
    cqn = _rms(cq, gq_ref[...]).astype(BF16)
    ckvn = _rms(ckv, gkv_ref[...])
    lat_o[...] = ckvn
    ckvb = ckvn.astype(BF16)
    kr_rot = krp * cp + krsw * sp
    rope_o[...] = pltpu.roll(kr_rot, HEAD_PAD - nope, 1)[:, :rope_dim]

    qh = _dot(cqn, wq_ref[...])
    qs = _dot(cqn, wqsw_ref[...])
    kh = _dot(ckvb, wk_ref[...])
    vh = _dot(ckvb, wv_ref[...])
    sum_lane = lax.broadcasted_iota(jnp.int32, cp.shape, 1) == dv_m
    for i in range(n_mh):
        sl = slice(i * HEAD_PAD, (i + 1) * HEAD_PAD)
        q_o[i] = ((qh[:, sl] * cp + qs[:, sl] * sp) * q_scale).astype(BF16)
        k_o[i] = (kh[:, sl] + kr_rot).astype(BF16)
        v_o[i] = jnp.where(sum_lane, 1.0, vh[:, sl]).astype(BF16)

    if ret_chunk:
        @pl.when(blk == blocks_per_seq - 1)
        def _():
            st_o[0] = s_scr[...]


def _inproj(x, tables, wts, dims, tm, ret_gn=None, n_seq=1):
    n, d_model = x.shape
    cosr, sinr, cp, sp = tables
    tab_blocks = cosr.shape[0] // tm
    n_rh, dk, dv, q_rank, kv_rank, rope_dim, n_mh, nope, dv_m = dims
    grid = (n // tm,)
    tok = lambda w: pl.BlockSpec((tm, w), lambda i: (i, 0))
    tab = lambda w: pl.BlockSpec((tm, w), lambda i: (i % tab_blocks, 0))
    head = pl.BlockSpec((n_mh, tm, HEAD_PAD), lambda i: (0, i, 0))
    w_names = ("gmix", "wmain", "wsmall", "gq", "gkv", "wq", "wqsw", "wk", "wv")
    args = [x] + [wts[k] for k in w_names] + [cosr, sinr, cp, sp]
    in_specs = ([tok(d_model)] + [_const_spec(wts[k].shape) for k in w_names]
                + [tab(dk), tab(dk), tab(HEAD_PAD), tab(HEAD_PAD)])
    qk_w, v_w = n_rh * dk, n_rh * dv
    sds = jax.ShapeDtypeStruct
    common_shape = (sds((n, d_model), BF16), sds((n, d_model), BF16), sds((n, kv_rank), F32),
                    sds((n, rope_dim), F32), sds((n_mh, n, HEAD_PAD), BF16), sds((n_mh, n, HEAD_PAD), BF16),
                    sds((n_mh, n, HEAD_PAD), BF16))
    common_specs = (tok(d_model), tok(d_model), tok(kv_rank), tok(rope_dim), head, head, head)
    scratch = []
    if ret_gn is None:
        ret_chunk, blocks_per_seq, cd = 0, 1, None
        out_shape = (sds((n, qk_w), BF16), sds((n, qk_w), BF16), sds((n, v_w), BF16),
                     sds((n, v_w), BF16)) + common_shape
        out_specs = (tok(qk_w), tok(qk_w), tok(v_w), tok(v_w)) + common_specs
    else:
        ret_chunk = min(RET_CHUNK, tm)
        blocks_per_seq = n // n_seq // tm
        mask, qd, kd, cd, _ = _ret_tables(n_rh, ret_chunk, dk, dv)
        tabs = [mask, qd, kd, ret_gn.reshape(n_rh, 1, dv)]
        args += tabs
        in_specs += [_const_spec(a.shape) for a in tabs]
        out_shape = (sds((n, v_w), BF16), sds((n_seq, n_rh, dk, dv), F32)) + common_shape
        out_specs = (tok(v_w), pl.BlockSpec((1, n_rh, dk, dv), lambda i: (i // blocks_per_seq, 0, 0, 0))
                     ) + common_specs
        scratch = [pltpu.VMEM((n_rh, dk, dv), F32)]
    kern = functools.partial(_inproj_kernel, n_rh=n_rh, dk=dk, dv=dv, q_rank=q_rank, kv_rank=kv_rank,
                             rope_dim=rope_dim, n_mh=n_mh, nope=nope, dv_m=dv_m, k_scale=dk ** -0.5,
                             q_scale=(nope + rope_dim) ** -0.5 * math.log2(math.e),
                             ret_chunk=ret_chunk, blocks_per_seq=blocks_per_seq, chunk_decay=cd)
    return pl.pallas_call(kern, out_shape=out_shape, grid=grid, in_specs=in_specs, out_specs=out_specs,
                          scratch_shapes=scratch, compiler_params=_cparams(1), name="inproj")(*args)


def _ret_tables(n_rh, chunk, dk, dv):
    log_g = np.log1p(-np.exp2(-5.0 - np.arange(n_rh, dtype=np.float64)))
    idx = np.arange(chunk, dtype=np.float64)
    diff = idx[:, None] - idx[None, :]
    mask = np.where(diff >= 0, np.exp(log_g[:, None, None] * np.maximum(diff, 0.0)), 0.0)
    qd = np.exp(log_g[:, None] * (idx + 1.0))
    kd = np.exp(log_g[:, None] * (chunk - 1.0 - idx))
    cd = np.exp(log_g * chunk)
    qd = np.broadcast_to(qd[:, :, None], (n_rh, chunk, dv))
    kd = np.broadcast_to(kd[:, :, None], (n_rh, chunk, dk))
    return (jnp.asarray(mask, F32), jnp.asarray(qd, F32), jnp.asarray(kd, F32),
            tuple(float(np.float32(v)) for v in cd), tuple(float(np.float32(v)) for v in np.exp(log_g)))


def _ret_sample_kernel(qt_ref, kt_ref, v_ref, gate_ref, st_ref, gn_ref, o_ref, ns_ref,
                       *, n_rh, dk, dv, gamma):
    seqs = v_ref.shape[0]
    lane = lax.broadcasted_iota(jnp.int32, qt_ref.shape, 1)
    for j in range(seqs):
        sel = lane == pl.program_id(0) * seqs + j
        qcol = jnp.sum(jnp.where(sel, qt_ref[...], 0.0), axis=1, keepdims=True)
        kcol = jnp.sum(jnp.where(sel, kt_ref[...], 0.0), axis=1, keepdims=True)
        v = v_ref[j]
        gate = gate_ref[j]
        for i in range(n_rh):
            vi = v[:, i * dv:(i + 1) * dv]
            new = st_ref[j, i] * gamma[i] + kcol[i * dk:(i + 1) * dk] * vi
            ns_ref[j, i] = new
            o = jnp.sum(qcol[i * dk:(i + 1) * dk] * new, axis=0, keepdims=True)
            o_ref[j, :, i * dv:(i + 1) * dv] = _rms(o, gn_ref[i]) * gate[:, i * dv:(i + 1) * dv]


def _ret_sample(rq, rk, rv, rgs, state, gn, n_rh, dk, dv):
    nb = rq.shape[0]
    _, _, _, _, gamma = _ret_tables(n_rh, 1, dk, dv)
    qt = rq.astype(F32).T
    kt = rk.astype(F32).T
    seqs = math.gcd(nb, RET_SAMPLE_SEQS)
    row = lambda w: pl.BlockSpec((seqs, 1, w), lambda b: (b, 0, 0))
    st_spec = pl.BlockSpec((seqs, n_rh, dk, dv), lambda b: (b, 0, 0, 0))
    in_specs = [_const_spec(qt.shape), _const_spec(kt.shape), row(n_rh * dv), row(n_rh * dv), st_spec,
                _const_spec((n_rh, 1, dv))]
    out_shape = (jax.ShapeDtypeStruct((nb, 1, n_rh * dv), F32),
                 jax.ShapeDtypeStruct((nb, n_rh, dk, dv), F32))
    kern = functools.partial(_ret_sample_kernel, n_rh=n_rh, dk=dk, dv=dv, gamma=gamma)
    o, ns = pl.pallas_call(kern, out_shape=out_shape, grid=(nb // seqs,), in_specs=in_specs,
                           out_specs=(row(n_rh * dv), st_spec), compiler_params=_cparams(1),
                           name="retention_sample")(
        qt, kt, rv.astype(F32).reshape(nb, 1, -1), rgs.astype(F32).reshape(nb, 1, -1), state,
        gn.reshape(n_rh, 1, dv))
    return o.reshape(nb, n_rh * dv).astype(BF16), ns


def _attn_prompt_kernel(q_ref, k_ref, v_ref, o_ref, m_scr, acc_scr, *, tq, hpg, sum_lane):
    i = pl.program_id(2)
    m_scr[...] = jnp.full(m_scr.shape, -jnp.inf, F32)
    acc_scr[...] = jnp.zeros(acc_scr.shape, F32)

    def tile(j, masked):
        start = pl.multiple_of(j * tq, tq)
        for h in range(hpg):
            s = _dot_nt(q_ref[h], k_ref[h, pl.ds(start, tq), :])
            if masked:
                row = lax.broadcasted_iota(jnp.int32, s.shape, 0)
                col = lax.broadcasted_iota(jnp.int32, s.shape, 1)
                s = jnp.where(col <= row, s, -jnp.inf)
            m = m_scr[h]
            m_new = jnp.maximum(m, jnp.max(s, axis=1, keepdims=True))
            m_wide = jnp.concatenate([m_new] * (tq // HEAD_PAD), axis=1)
            p = jnp.exp2((s - m_wide).astype(BF16))
            acc_scr[h] = jnp.exp2(m - m_new) * acc_scr[h] + _dot(p, v_ref[h, pl.ds(start, tq), :])
            m_scr[h] = m_new

    def body(j, carry):
        tile(j, False)
        return carry

    lax.fori_loop(0, i, body, 0)
    tile(i, True)
    for h in range(hpg):
        acc = acc_scr[h]
        o = acc / acc[:, sum_lane:sum_lane + 1]
        o_ref[:, h * sum_lane:(h + 1) * sum_lane] = o[:, :sum_lane].astype(BF16)


def _attn_prompt(q, k, v, b, t, tq, hpg, sum_lane):
    n_mh, n, _ = q.shape
    nq = t // tq
    kv_spec = pl.BlockSpec((hpg, t, HEAD_PAD), lambda bi, g, i: (g, bi, 0), pipeline_mode=pl.Buffered(1))
    in_specs = [pl.BlockSpec((hpg, tq, HEAD_PAD), lambda bi, g, i: (g, bi * nq + i, 0)), kv_spec, kv_spec]
    out_specs = pl.BlockSpec((tq, hpg * sum_lane), lambda bi, g, i: (bi * nq + i, g))
    return pl.pallas_call(functools.partial(_attn_prompt_kernel, tq=tq, hpg=hpg, sum_lane=sum_lane),
                          out_shape=jax.ShapeDtypeStruct((n, n_mh * sum_lane), BF16),
                          grid=(b, n_mh // hpg, nq), in_specs=in_specs, out_specs=out_specs,
                          scratch_shapes=[pltpu.VMEM((hpg, tq, HEAD_PAD), F32),
                                          pltpu.VMEM((hpg, tq, HEAD_PAD), F32)],
                          compiler_params=_cparams(3), name="attn_prompt")(q, k, v)


def _attn_sample_kernel(pt_ref, ql_ref, qp_ref, cn_ref, rn_ref, lat_hbm, rope_hbm, o_ref,
                        lat_slab, rope_slab, sems, m_scr, l_scr, acc_scr, *, npg, n_chunks, page):
    step = pl.program_id(0)
    n_steps = pl.num_programs(0)
    n_slots = lat_slab.shape[0]
    ahead = n_slots - 1
    slot = step % n_slots
    chunk = step % n_chunks

    def page_copies(page_ids, slot_):
        out = []
        for i in range(npg):
            rows = pl.ds(i * page, page)
            out.append(pltpu.make_async_copy(lat_hbm.at[page_ids(i)], lat_slab.at[slot_, rows, :],
                                             sems.at[0, slot_]))
            out.append(pltpu.make_async_copy(rope_hbm.at[page_ids(i)], rope_slab.at[slot_, :, rows],
                                             sems.at[1, slot_]))
        return out

    def start_step(step_, slot_):
        seq = step_ // n_chunks
        first = (step_ % n_chunks) * npg
        for cp in page_copies(lambda i: pt_ref[seq, first + i], slot_):
            cp.start()

    for k in range(ahead):
        @pl.when(jnp.logical_and(step == 0, k < n_steps))
        def _():
            start_step(k, k)

    @pl.when(step + ahead < n_steps)
    def _():
        start_step(step + ahead, (step + ahead) % n_slots)

    for cp in page_copies(lambda i: 0, slot):
        cp.wait()

    ql = ql_ref[0]
    qp = qp_ref[0]

    @pl.when(chunk == 0)
    def _():
        cn = cn_ref[0]
        s0 = (jnp.sum(ql * cn, axis=1, keepdims=True)
              + jnp.sum(qp * rn_ref[0], axis=1, keepdims=True))
        m_scr[...] = s0
        l_scr[...] = jnp.ones_like(l_scr)
        acc_scr[...] = jnp.broadcast_to(cn, acc_scr.shape)

    lat = lat_slab[slot].astype(BF16)
    s = _dot_nt(ql.astype(BF16), lat) + _dot(qp, rope_slab[slot])
    m_prev = m_scr[...]
    m_new = jnp.maximum(m_prev, jnp.max(s, axis=1, keepdims=True))
    alpha = jnp.exp2(m_prev - m_new)
    p = jnp.exp2(s - m_new)
    m_scr[...] = m_new
    l_scr[...] = alpha * l_scr[...] + jnp.sum(p, axis=1, keepdims=True)
    acc_scr[...] = alpha * acc_scr[...] + _dot(p.astype(BF16), lat)

    @pl.when(chunk == n_chunks - 1)
    def _():
        o_ref[0] = acc_scr[...] / l_scr[...]


def _attn_sample(page_table, q_lat, q_pe, c_new, r_new, cache_lat, cache_rope_t, npg):
    nb, n_mh, kv_rank = q_lat.shape
    rope_dim = q_pe.shape[-1]
    n_pages = page_table.shape[1]
    page = cache_lat.shape[1]
    n_chunks = n_pages // npg
    per_seq = lambda shp: pl.BlockSpec((1,) + shp, lambda s, pt: (s // n_chunks, 0, 0))
    hbm = pl.BlockSpec(memory_space=pl.ANY)
    in_specs = [per_seq((n_mh, kv_rank)), per_seq((n_mh, rope_dim)), per_seq((1, kv_rank)),
                per_seq((1, rope_dim)), hbm, hbm]
    grid_spec = pltpu.PrefetchScalarGridSpec(
        num_scalar_prefetch=1, grid=(nb * n_chunks,), in_specs=in_specs,
        out_specs=per_seq((n_mh, kv_rank)),
        scratch_shapes=[pltpu.VMEM((DECODE_SLOTS, npg * page, kv_rank), F32),
                        pltpu.VMEM((DECODE_SLOTS, rope_dim, npg * page), F32),
                        pltpu.SemaphoreType.DMA((2, DECODE_SLOTS)),
                        pltpu.VMEM((n_mh, 1), F32), pltpu.VMEM((n_mh, 1), F32),
                        pltpu.VMEM((n_mh, kv_rank), F32)])
    kern = functools.partial(_attn_sample_kernel, npg=npg, n_chunks=n_chunks, page=page)
    return pl.pallas_call(kern, out_shape=jax.ShapeDtypeStruct((nb, n_mh, kv_rank), F32),
                          grid_spec=grid_spec, compiler_params=_cparams(1), name="attn_sample")(
        page_table, q_lat, q_pe, c_new, r_new, cache_lat, cache_rope_t)


def _headmm_kernel(a_ref, w_ref, o_ref):
    o_ref[0] = _dot(a_ref[0].astype(BF16), w_ref[0]).astype(o_ref.dtype)


def _headmm(a, w, out_dtype):
    nh, m, kk = a.shape
    nn = w.shape[-1]
    spec = lambda r, c: pl.BlockSpec((1, r, c), lambda h: (h, 0, 0))
    return pl.pallas_call(_headmm_kernel, out_shape=jax.ShapeDtypeStruct((nh, m, nn), out_dtype),
                          grid=(nh,), in_specs=[spec(m, kk), spec(kk, nn)], out_specs=spec(m, nn),
                          compiler_params=_cparams(1), name="head_matmul")(a, w)


def _router_logits(h2, whi_ref, wlo_ref):
    hi = h2.astype(BF16)
    lo = (h2 - hi.astype(F32)).astype(BF16)
    lg = _dot(hi, whi_ref[...]) + (_dot(hi, wlo_ref[...]) + _dot(lo, whi_ref[...]))
    return hi, lg[:, :LANES], lg[:, LANES:]


def _merge_kernel(x_ref, ro_ref, gr_ref, mo_ref, gm_ref, wro_ref, wmo_ref, wout_ref, gffn_ref,
                  whi_ref, wlo_ref, x1_o, gid_o, *, group_size, n_exp):
    mixed = (_dot(ro_ref[...], wro_ref[...]) * gr_ref[...].astype(F32)
             + _dot(mo_ref[...], wmo_ref[...]) * gm_ref[...].astype(F32))
    x1 = x_ref[...] + _dot(mixed.astype(BF16), wout_ref[...])
    x1_o[...] = x1
    _, ge, _ = _router_logits(_rms(x1, gffn_ref[...]), whi_ref, wlo_ref)
    lane = lax.broadcasted_iota(jnp.int32, ge.shape, 1)
    ge = jnp.where(lane < n_exp, ge, -jnp.inf)
    grp = (lane // group_size).astype(F32)
    gmax = jnp.max(ge, axis=1, keepdims=True)
    g_idx = jnp.min(jnp.where(ge == gmax, grp, float(n_exp)), axis=1, keepdims=True)
    gid_o[...] = jnp.broadcast_to(g_idx, ge.shape).astype(jnp.int32)


def _merge(x, ro, gr, mo, gm, wts, group_size, n_exp, tm):
    n, d_model = x.shape
    tok = lambda w: pl.BlockSpec((tm, w), lambda i: (i, 0))
    w_args = [wts[k] for k in ("wro", "wmo", "wout", "gffn", "wr_hi", "wr_lo")]
    in_specs = [tok(d_model), tok(ro.shape[1]), tok(d_model), tok(mo.shape[1]), tok(d_model)] + [
        _const_spec(a.shape) for a in w_args]
    out_shape = (jax.ShapeDtypeStruct((n, d_model), F32), jax.ShapeDtypeStruct((n, LANES), jnp.int32))
    kern = functools.partial(_merge_kernel, group_size=group_size, n_exp=n_exp)
    return pl.pallas_call(kern, out_shape=out_shape, grid=(n // tm,), in_specs=in_specs,
                          out_specs=(tok(d_model), tok(LANES)),
                          compiler_params=_cparams(1), name="merge_router")(x, ro, gr, mo, gm, *w_args)


def _group_tiles(gid, n_grp, tmr):
    n = gid.shape[0]
    token = jnp.arange(n, dtype=jnp.int32)
    counts = jnp.sum((gid[:, None] == jnp.arange(n_grp, dtype=jnp.int32)[None, :]).astype(jnp.int32), axis=0)
    by_group = jnp.sort(gid * n + token) % n
    dense_start = jnp.cumsum(counts) - counts
    padded = ((counts + tmr - 1) // tmr) * tmr
    ends = jnp.cumsum(padded)
    offs = ends - padded
    n_tiles = n // tmr + n_grp
    tile_start = jnp.arange(n_tiles, dtype=jnp.int32) * tmr
    tile_group = jnp.minimum(jnp.sum((tile_start[:, None] >= ends[None, :]).astype(jnp.int32), axis=1),
                             n_grp - 1)
    tile_valid = jnp.clip(counts[tile_group] - (tile_start - offs[tile_group]), 0, tmr)
    row_group = jnp.repeat(tile_group, tmr)
    in_group = jnp.arange(n_tiles * tmr, dtype=jnp.int32) - offs[row_group]
    src = jnp.clip(dense_start[row_group] + in_group, 0, n - 1)
    row_token = jnp.where(in_group < counts[row_group], by_group[src], 0)
    return row_token, tile_group, tile_valid.astype(jnp.int32)


def _moe_kernel(rt_ref, tg_ref, tv_ref, x1_hbm, gffn_ref, gfin_ref, whi_ref, wlo_ref, wg_ref, wu_ref, wd_ref,
                y_hbm, xbuf, obuf, trash, gsem, ssem, *, tmr, group_size, n_exp):
    t = pl.program_id(0)
    last = pl.num_programs(0) - 1
    slot = t % 2

    def gather_copy(tok, r, slot_):
        return pltpu.make_async_copy(x1_hbm.at[pl.ds(tok, 1)], xbuf.at[slot_, pl.ds(r, 1)], gsem.at[slot_])

    def start_gather(tile, slot_):
        for r in range(tmr):
            gather_copy(rt_ref[tile * tmr + r], r, slot_).start()

    def wait_gather(slot_):
        for r in range(tmr):
            gather_copy(0, r, slot_).wait()

    def scatter_copy(r, dst_row_ref, slot_):
        return pltpu.make_async_copy(obuf.at[slot_, pl.ds(r, 1)], dst_row_ref, ssem.at[slot_])

    def wait_scatter(slot_):
        for r in range(tmr):
            scatter_copy(r, trash.at[slot_, pl.ds(r, 1)], slot_).wait()

    @pl.when(t == 0)
    def _():
        start_gather(t, slot)

    wait_gather(slot)

    @pl.when(jnp.logical_and(t >= 2, tv_ref[jnp.maximum(t - 2, 0)] > 0))
    def _():
        wait_scatter(slot)

    start_gather(jnp.minimum(t + 1, last), 1 - slot)

    n_real = tv_ref[t]

    @pl.when(n_real > 0)
    def _():
        g = tg_ref[t]
        x = xbuf[slot]
        hb, ge, el = _router_logits(_rms(x, gffn_ref[...]), whi_ref, wlo_ref)
        lane_i = lax.broadcasted_iota(jnp.int32, ge.shape, 1)
        lane = lane_i.astype(F32)
        is_exp = lane_i < n_exp
        in_grp = (lane_i // group_size) == g
        gmax = jnp.max(jnp.where(is_exp, ge, -jnp.inf), axis=1, keepdims=True)
        gsum = jnp.sum(jnp.where(is_exp, jnp.exp(ge - gmax), 0.0), axis=1, keepdims=True) / group_size
        ge_g = jnp.max(jnp.where(in_grp, ge, -jnp.inf), axis=1, keepdims=True)
        g_w = jnp.exp(ge_g - gmax) / gsum
        e_in = jnp.where(in_grp, el, -jnp.inf)
        top1 = jnp.max(e_in, axis=1, keepdims=True)
        idx1 = jnp.min(jnp.where(e_in == top1, lane, float(LANES)), axis=1, keepdims=True)
        e_rest = jnp.where(lane == idx1, -jnp.inf, e_in)
        top2 = jnp.max(e_rest, axis=1, keepdims=True)
        idx2 = jnp.min(jnp.where(e_rest == top2, lane, float(LANES)), axis=1, keepdims=True)
        e2 = jnp.exp(top2 - top1)
        comb = (jnp.where(lane == idx1, g_w / (1.0 + e2), 0.0)
                + jnp.where(lane == idx2, g_w * e2 / (1.0 + e2), 0.0))

        first = (g * group_size).astype(F32)
        parts = []
        for e in range(group_size):
            ce = jnp.sum(jnp.where(lane == first + e, comb, 0.0), axis=1, keepdims=True)
            hg = _dot(hb, wg_ref[e])
            parts.append((hg * _sigmoid(hg) * _dot(hb, wu_ref[e]) * ce).astype(BF16))
        y = x + _dot(jnp.concatenate(parts, axis=1), wd_ref[0])
        obuf[slot] = _rms(y, gfin_ref[...])

    def scatter_real(r, carry):
        tok = rt_ref[t * tmr + r]
        scatter_copy(r, y_hbm.at[pl.ds(tok, 1)], slot).start()
        return carry

    def scatter_pad(r, carry):
        scatter_copy(r, trash.at[slot, pl.ds(r, 1)], slot).start()
        return carry

    @pl.when(n_real == tmr)
    def _():
        for r in range(tmr):
            scatter_real(r, 0)

    @pl.when(jnp.logical_and(n_real > 0, n_real < tmr))
    def _():
        lax.fori_loop(0, n_real, scatter_real, 0)
        lax.fori_loop(n_real, tmr, scatter_pad, 0)

    @pl.when(t == last)
    def _():
        wait_gather(1 - slot)

        @pl.when(n_real > 0)
        def _():
            wait_scatter(slot)

        @pl.when(jnp.logical_and(t >= 1, tv_ref[jnp.maximum(t - 1, 0)] > 0))
        def _():
            wait_scatter(1 - slot)


def _moe(x1, gid, wts, wg, wu, wd, gfin, group_size, n_exp, tmr):
    n, d_model = x1.shape
    n_grp = wd.shape[0]
    row_token, tile_group, tile_valid = _group_tiles(gid, n_grp, tmr)
    n_tiles = tile_group.shape[0]
    hbm = pl.BlockSpec(memory_space=pl.ANY)
    const = lambda a: pl.BlockSpec(a.shape, lambda t, rt, tg, tv: (0,) * a.ndim, pipeline_mode=pl.Buffered(1))
    grp_w = lambda a: pl.BlockSpec((a.shape[0] // n_grp,) + a.shape[1:], lambda t, rt, tg, tv: (tg[t], 0, 0))
    w_args = [wts["gffn"], gfin, wts["wr_hi"], wts["wr_lo"]]
    grid_spec = pltpu.PrefetchScalarGridSpec(
        num_scalar_prefetch=3, grid=(n_tiles,),
        in_specs=[hbm] + [const(a) for a in w_args] + [grp_w(wg), grp_w(wu), grp_w(wd)],
        out_specs=hbm,
        scratch_shapes=[pltpu.VMEM((2, tmr, d_model), F32), pltpu.VMEM((2, tmr, d_model), F32),
                        pltpu.VMEM((2, tmr, d_model), F32),
                        pltpu.SemaphoreType.DMA((2,)), pltpu.SemaphoreType.DMA((2,))])
    kern = functools.partial(_moe_kernel, tmr=tmr, group_size=group_size, n_exp=n_exp)
    return pl.pallas_call(kern, out_shape=jax.ShapeDtypeStruct((n, d_model), F32), grid_spec=grid_spec,
                          compiler_params=_cparams(1), name="moe")(
        row_token, tile_group, tile_valid, x1, *w_args, wg, wu, wd)


def _rope_tables(start, count, repeat, dk, rope_dim, nope):
    n_hi = -(-count // LANES)

    def cs(half):
        inv = ROPE_BASE ** (-jnp.arange(half, dtype=F32) / half)
        a = (start + LANES * jnp.arange(n_hi, dtype=F32))[:, None, None] * inv
        b = jnp.arange(LANES, dtype=F32)[None, :, None] * inv
        ca, sa, cb, sb = jnp.cos(a), jnp.sin(a), jnp.cos(b), jnp.sin(b)
        c = (ca * cb - sa * sb).reshape(n_hi * LANES, half)[:count]
        s = (sa * cb + ca * sb).reshape(n_hi * LANES, half)[:count]
        return jnp.repeat(c, repeat, axis=0), jnp.repeat(s, repeat, axis=0)

    c, s = cs(dk // 2)
    cosr = jnp.concatenate([c, c], axis=1)
    sinr = jnp.concatenate([-s, s], axis=1)
    c, s = cs(rope_dim // 2)
    n = count * repeat
    tail = jnp.zeros((n, HEAD_PAD - nope - rope_dim), F32)
    cp = jnp.concatenate([jnp.ones((n, nope), F32), c, c, tail], axis=1)
    sp = jnp.concatenate([jnp.zeros((n, nope), F32), -s, s, tail], axis=1)
    return cosr, sinr, cp, sp


def _swap_halves(w):
    half = w.shape[-1] // 2
    return jnp.concatenate([w[..., half:], w[..., :half]], axis=-1)


def _layout_weights(norm_mix_g, w_in, q_norm_g, kv_norm_g, w_uq, w_ukv, w_ret_o, w_mla_o, w_out,
                    norm_ffn_g, w_router_group, w_router_expert, dims):
    n_rh, dk, dv, q_rank, kv_rank, rope_dim, n_mh, nope, dv_m = dims
    d_model = w_in.shape[0]
    qk_w, v_w = n_rh * dk, n_rh * dv
    o_cq = 2 * qk_w + 2 * v_w
    o_ckv = o_cq + q_rank
    o_kr = o_ckv + kv_rank
    o_gate = o_kr + rope_dim
    pad_tail = HEAD_PAD - nope - rope_dim
    zeros = lambda *shape: jnp.zeros(shape, F32)
    w_kr = w_in[:, o_kr:o_gate]
    place = lambda w: jnp.concatenate([zeros(d_model, nope), w, zeros(d_model, pad_tail)], axis=1)
    wsmall = jnp.concatenate([w_in[:, o_cq:o_kr], place(w_kr), place(_swap_halves(w_kr))], axis=1)
    wmain = jnp.concatenate([w_in[:, :o_cq], w_in[:, o_gate:]], axis=1)

    uq = w_uq.reshape(q_rank, n_mh, nope + rope_dim)
    uq_rope = uq[..., nope:]
    wq = jnp.concatenate([uq, zeros(q_rank, n_mh, pad_tail)], axis=-1)
    wqsw = jnp.concatenate([zeros(q_rank, n_mh, nope), _swap_halves(uq_rope),
                            zeros(q_rank, n_mh, pad_tail)], axis=-1)
    uk = w_ukv[..., :nope]
    uv = w_ukv[..., nope:]
    wk = jnp.concatenate([uk, zeros(kv_rank, n_mh, HEAD_PAD - nope)], axis=-1)
    wv = jnp.concatenate([uv, zeros(kv_rank, n_mh, HEAD_PAD - dv_m)], axis=-1)
    flat = lambda w: w.reshape(w.shape[0], n_mh * HEAD_PAD).astype(BF16)
    wmo = w_mla_o
    wabs = jnp.concatenate([jnp.transpose(uk, (1, 2, 0)), zeros(n_mh, HEAD_PAD - nope, kv_rank)], axis=1)
    wvh = jnp.transpose(uv, (1, 0, 2))
    n_grp = w_router_group.shape[1]
    n_exp = w_router_expert.shape[1]
    group_size = n_exp // n_grp
    assert n_exp <= LANES
    w_r = jnp.concatenate([jnp.repeat(w_router_group, group_size, axis=1), zeros(d_model, LANES - n_exp),
                           w_router_expert, zeros(d_model, LANES - n_exp)], axis=1)
    wr_hi = w_r.astype(BF16)
    wr_lo = (w_r - wr_hi.astype(F32)).astype(BF16)
    return dict(
        gmix=norm_mix_g.reshape(1, -1), wmain=wmain.astype(BF16), wsmall=wsmall.astype(BF16),
        gq=q_norm_g.reshape(1, -1), gkv=kv_norm_g.reshape(1, -1),
        wq=flat(wq), wqsw=flat(wqsw), wk=flat(wk), wv=flat(wv),
        wro=w_ret_o.astype(BF16), wmo=wmo.astype(BF16), wout=w_out.astype(BF16),
        gffn=norm_ffn_g.reshape(1, -1), wr_hi=wr_hi, wr_lo=wr_lo,
        wabs=wabs.astype(BF16), wvh=wvh.astype(BF16)), group_size, n_exp


def kernel(x_prompt, x_sample, cache_kv_latent, cache_k_rope, state_retention, page_table, norm_mix_g,
           w_in, q_norm_g, kv_norm_g, w_uq, w_ukv, ret_gn_g, w_ret_o, w_mla_o, w_out, norm_ffn_g,
           w_router_group, w_router_expert, w_gate, w_up, w_down, norm_final_g):
    depth = w_in.shape[0]
    assert depth == 1, "single-layer step only"
    b, t, d_model = x_prompt.shape
    bd, tn, _ = x_sample.shape
    assert tn == 1, "one new token per sample sequence"
    _, _, n_rh, dk, dv = state_retention.shape
    kv_rank = cache_kv_latent.shape[-1]
    rope_dim = cache_k_rope.shape[-1]
    page = cache_kv_latent.shape[2]
    n_mh = w_ukv.shape[2]
    q_rank = w_uq.shape[1]
    nope = w_uq.shape[2] // n_mh - rope_dim
    dv_m = w_ukv.shape[3] - nope
    assert nope + rope_dim <= HEAD_PAD and dv_m < HEAD_PAD
    dims = (n_rh, dk, dv, q_rank, kv_rank, rope_dim, n_mh, nope, dv_m)
    past_len = page_table.shape[1] * page

    wts, group_size, n_exp = _layout_weights(norm_mix_g[0], w_in[0], q_norm_g[0], kv_norm_g[0], w_uq[0],
                                             w_ukv[0], w_ret_o[0], w_mla_o[0], w_out[0], norm_ffn_g[0],
                                             w_router_group[0], w_router_expert[0], dims)
    gn = ret_gn_g[0]
    n_grp = n_exp // group_size
    d_exp = w_gate.shape[-1]
    wg, wu = w_gate[0].astype(BF16), w_up[0].astype(BF16)
    wd = w_down[0].astype(BF16).reshape(n_grp, group_size * d_exp, d_model)
    gfin = norm_final_g.reshape(1, -1)

    tm_p = min(INPROJ_ROW_TILE, t)
    xp = x_prompt.reshape(b * t, d_model)
    tabs_p = _rope_tables(0.0, t, 1, dk, rope_dim, nope)
    (ret_o, st_p, gret, gmla, lat_p, rope_p, q_p, k_p, v_p) = _inproj(xp, tabs_p, wts, dims, tm_p,
                                                                      ret_gn=gn, n_seq=b)
    mla_o = _attn_prompt(q_p, k_p, v_p, b, t, min(ATTN_TQ, t), min(ATTN_HEADS_PER_STEP, n_mh), dv_m)
    x1, gid = _merge(xp, ret_o, gret, mla_o, gmla, wts, group_size, n_exp,
                     min(MERGE_ROW_TILE, t))
    y_prompt = _moe(x1, gid[:, 0], wts, wg, wu, wd, gfin, group_size, n_exp,
                    min(MOE_ROW_TILE, b * t)).reshape(b, t, d_model)

    xs = x_sample.reshape(bd, d_model)
    tabs_s = _rope_tables(float(past_len), tn, bd, dk, rope_dim, nope)
    (rq, rk, rv, rgs, gret, gmla, lat_s, rope_s, q_s, _, _) = _inproj(xs, tabs_s, wts, dims, bd)
    ret_o_s, st_s = _ret_sample(rq, rk, rv, rgs, state_retention[0], gn, n_rh, dk, dv)
    q_lat = jnp.transpose(_headmm(q_s, wts["wabs"], F32), (1, 0, 2))
    q_pe = jnp.transpose(q_s[:, :, nope:nope + rope_dim].astype(F32), (1, 0, 2))
    o_lat = _attn_sample(page_table, q_lat, q_pe, lat_s.reshape(bd, 1, kv_rank),
                         rope_s.reshape(bd, 1, rope_dim), cache_kv_latent.reshape(-1, page, kv_rank),
                         jnp.transpose(cache_k_rope.reshape(-1, page, rope_dim), (0, 2, 1)),
                         min(DECODE_PAGES, page_table.shape[1]))
    mla_o_s = _headmm(jnp.transpose(o_lat, (1, 0, 2)), wts["wvh"], BF16)
    mla_o_s = jnp.transpose(mla_o_s, (1, 0, 2)).reshape(bd, n_mh * dv_m)
    x1, gid = _merge(xs, ret_o_s, gret, mla_o_s, gmla, wts, group_size, n_exp, bd)
    y_sample = _moe(x1, gid[:, 0], wts, wg, wu, wd, gfin, group_size, n_exp,
                    min(MOE_ROW_TILE, bd)).reshape(bd, tn, d_model)

    return (y_prompt, y_sample,
            lat_p.reshape(depth, b, t // page, page, kv_rank),
            rope_p.reshape(depth, b, t // page, page, rope_dim),
            st_p.reshape(depth, b, n_rh, dk, dv),
            lat_s.reshape(depth, bd, tn, kv_rank),
            rope_s.reshape(depth, bd, tn, rope_dim),
            st_s.reshape(depth, bd, n_rh, dk, dv))
```

```python
import functools
import math

import numpy as np
import jax
import jax.numpy as jnp
from jax import lax
from jax.experimental import pallas as pl
from jax.experimental.pallas import tpu as pltpu

F32 = jnp.float32
BF16 = jnp.bfloat16

ROPE_BASE = 10000.0
EPS = 1e-6
RET_CHUNK = 128
LANES = 128
HEAD_PAD = LANES
INPROJ_ROW_TILE = 512
MERGE_ROW_TILE = 512
MOE_ROW_TILE = 256
RET_SAMPLE_SEQS = 8
DECODE_PAGES = 64
DECODE_SLOTS = 3
ATTN_TQ = 512
ATTN_HEADS_PER_STEP = 8
VMEM_LIMIT = 48 * 1024 * 1024


def _cparams(n_axes):
    return pltpu.CompilerParams(dimension_semantics=("arbitrary",) * n_axes,
                                vmem_limit_bytes=VMEM_LIMIT)


def _const_spec(shape):
    nd = len(shape)
    return pl.BlockSpec(shape, lambda *_: (0,) * nd, pipeline_mode=pl.Buffered(1))


def _rms(x, g):
    return x * lax.rsqrt(jnp.mean(x * x, axis=-1, keepdims=True) + EPS) * g


def _sigmoid(x):
    return 1.0 / (1.0 + jnp.exp(-x))


def _dot(a, b):
    return jnp.dot(a, b, preferred_element_type=F32)


def _dot_nt(a, b):
    return lax.dot_general(a, b, (((1,), (1,)), ((), ())), preferred_element_type=F32)


def _retention_chunk(q, k, v, gate, state, i, mask_ref, qd_ref, kd_ref, gn_ref, chunk_decay):
    s = _dot_nt(q, k) * mask_ref[i]
    o = _dot(s.astype(BF16), v) + _dot(q, state.astype(BF16)) * qd_ref[i]
    kd = (k.astype(F32) * kd_ref[i]).T.astype(BF16)
    new_state = state * chunk_decay[i] + _dot(kd, v)
    return (_rms(o, gn_ref[i]) * gate.astype(F32)).astype(BF16), new_state


def _inproj_kernel(x_ref, gmix_ref, wmain_ref, wsmall_ref, gq_ref, gkv_ref, wq_ref, wqsw_ref,
                   wk_ref, wv_ref, cosr_ref, sinr_ref, cp_ref, sp_ref, *rest,
                   n_rh, dk, dv, q_rank, kv_rank, rope_dim, n_mh, nope, dv_m, k_scale, q_scale,
                   ret_chunk, blocks_per_seq, chunk_decay):
    if ret_chunk:
        (mask_ref, qd_ref, kd_ref, gn_ref, reto_o, st_o,
         gret_o, gmla_o, lat_o, rope_o, q_o, k_o, v_o, s_scr) = rest
    else:
        rq_o, rk_o, rv_o, rgs_o, gret_o, gmla_o, lat_o, rope_o, q_o, k_o, v_o = rest
    if ret_chunk:
        blk = pl.program_id(0) % blocks_per_seq

        @pl.when(blk == 0)
        def _():
            s_scr[...] = jnp.zeros_like(s_scr)
    x = x_ref[...]
    h = _rms(x, gmix_ref[...]).astype(BF16)
    cosr = cosr_ref[...]
    sinr = sinr_ref[...]
    qk_w = n_rh * dk
    v_w = n_rh * dv

    def rope_heads(z, scale):
        outs = []
        for i in range(n_rh):
            seg = z[:, i * dk:(i + 1) * dk]
            rot = seg * cosr + pltpu.roll(seg, dk // 2, 1) * sinr
            outs.append(rot if scale is None else rot * scale)
        return jnp.concatenate(outs, axis=1)

    off = 0
    rq = rope_heads(_dot(h, wmain_ref[:, off:off + qk_w]), None).astype(BF16)
    off += qk_w
    rk = rope_heads(_dot(h, wmain_ref[:, off:off + qk_w]), k_scale).astype(BF16)
    off += qk_w
    rv = _dot(h, wmain_ref[:, off:off + v_w]).astype(BF16)
    off += v_w
    rg = _dot(h, wmain_ref[:, off:off + v_w])
    rgs = (rg * _sigmoid(rg)).astype(BF16)
    off += v_w
    if ret_chunk:
        for c in range(x.shape[0] // ret_chunk):
            rows = slice(c * ret_chunk, (c + 1) * ret_chunk)
            for i in range(n_rh):
                qs, vs = slice(i * dk, (i + 1) * dk), slice(i * dv, (i + 1) * dv)
                o, s_scr[i] = _retention_chunk(rq[rows, qs], rk[rows, qs], rv[rows, vs], rgs[rows, vs],
                                               s_scr[i], i, mask_ref, qd_ref, kd_ref, gn_ref, chunk_decay)
                reto_o[rows, vs] = o
    else:
        rq_o[...] = rq
        rk_o[...] = rk
        rv_o[...] = rv
        rgs_o[...] = rgs
    d_model = x.shape[1]
    gret_o[...] = _sigmoid(_dot(h, wmain_ref[:, off:off + d_model])).astype(BF16)
    off += d_model
    gmla_o[...] = _sigmoid(_dot(h, wmain_ref[:, off:off + d_model])).astype(BF16)

    zs = _dot(h, wsmall_ref[...])
    cq = zs[:, :q_rank]
    ckv = zs[:, q_rank:q_rank + kv_rank]
    krp = zs[:, q_rank + kv_rank:q_rank + kv_rank + HEAD_PAD]
    krsw = zs[:, q_rank + kv_rank + HEAD_PAD:]
    cp = cp_ref[...]
    sp = sp_ref[...]
    cqn = _rms(cq, gq_ref[...]).astype(BF16)
    ckvn = _rms(ckv, gkv_ref[...])
    lat_o[...] = ckvn
    ckvb = ckvn.astype(BF16)
    kr_rot = krp * cp + krsw * sp
    rope_o[...] = pltpu.roll(kr_rot, HEAD_PAD - nope, 1)[:, :rope_dim]

    qh = _dot(cqn, wq_ref[...])
    qs = _dot(cqn, wqsw_ref[...])
    kh = _dot(ckvb, wk_ref[...])
    vh = _dot(ckvb, wv_ref[...])
    sum_lane = lax.broadcasted_iota(jnp.int32, cp.shape, 1) == dv_m
    for i in range(n_mh):
        sl = slice(i * HEAD_PAD, (i + 1) * HEAD_PAD)
        q_o[i] = ((qh[:, sl] * cp + qs[:, sl] * sp) * q_scale).astype(BF16)
        k_o[i] = (kh[:, sl] + kr_rot).astype(BF16)
        v_o[i] = jnp.where(sum_lane, 1.0, vh[:, sl]).astype(BF16)

    if ret_chunk:
        @pl.when(blk == blocks_per_seq - 1)
        def _():
            st_o[0] = s_scr[...]


def _inproj(x, tables, wts, dims, tm, ret_gn=None, n_seq=1):
    n, d_model = x.shape
    cosr, sinr, cp, sp = tables
    tab_blocks = cosr.shape[0] // tm
    n_rh, dk, dv, q_rank, kv_rank, rope_dim, n_mh, nope, dv_m = dims
    grid = (n // tm,)
    tok = lambda w: pl.BlockSpec((tm, w), lambda i: (i, 0))
    tab = lambda w: pl.BlockSpec((tm, w), lambda i: (i % tab_blocks, 0))
    head = pl.BlockSpec((n_mh, tm, HEAD_PAD), lambda i: (0, i, 0))
    w_names = ("gmix", "wmain", "wsmall", "gq", "gkv", "wq", "wqsw", "wk", "wv")
    args = [x] + [wts[k] for k in w_names] + [cosr, sinr, cp, sp]
    in_specs = ([tok(d_model)] + [_const_spec(wts[k].shape) for k in w_names]
                + [tab(dk), tab(dk), tab(HEAD_PAD), tab(HEAD_PAD)])
    qk_w, v_w = n_rh * dk, n_rh * dv
    sds = jax.ShapeDtypeStruct
    common_shape = (sds((n, d_model), BF16), sds((n, d_model), BF16), sds((n, kv_rank), F32),
                    sds((n, rope_dim), F32), sds((n_mh, n, HEAD_PAD), BF16), sds((n_mh, n, HEAD_PAD), BF16),
                    sds((n_mh, n, HEAD_PAD), BF16))
    common_specs = (tok(d_model), tok(d_model), tok(kv_rank), tok(rope_dim), head, head, head)
    scratch = []
    if ret_gn is None:
        ret_chunk, blocks_per_seq, cd = 0, 1, None
        out_shape = (sds((n, qk_w), BF16), sds((n, qk_w), BF16), sds((n, v_w), BF16),
                     sds((n, v_w), BF16)) + common_shape
        out_specs = (tok(qk_w), tok(qk_w), tok(v_w), tok(v_w)) + common_specs
    else:
        ret_chunk = min(RET_CHUNK, tm)
        blocks_per_seq = n // n_seq // tm
        mask, qd, kd, cd, _ = _ret_tables(n_rh, ret_chunk, dk, dv)
        tabs = [mask, qd, kd, ret_gn.reshape(n_rh, 1, dv)]
        args += tabs
        in_specs += [_const_spec(a.shape) for a in tabs]
        out_shape = (sds((n, v_w), BF16), sds((n_seq, n_rh, dk, dv), F32)) + common_shape
        out_specs = (tok(v_w), pl.BlockSpec((1, n_rh, dk, dv), lambda i: (i // blocks_per_seq, 0, 0, 0))
                     ) + common_specs
        scratch = [pltpu.VMEM((n_rh, dk, dv), F32)]
    kern = functools.partial(_inproj_kernel, n_rh=n_rh, dk=dk, dv=dv, q_rank=q_rank, kv_rank=kv_rank,
                             rope_dim=rope_dim, n_mh=n_mh, nope=nope, dv_m=dv_m, k_scale=dk ** -0.5,
                             q_scale=(nope + rope_dim) ** -0.5 * math.log2(math.e),
                             ret_chunk=ret_chunk, blocks_per_seq=blocks_per_seq, chunk_decay=cd)
    return pl.pallas_call(kern, out_shape=out_shape, grid=grid, in_specs=in_specs, out_specs=out_specs,
                          scratch_shapes=scratch, compiler_params=_cparams(1), name="inproj")(*args)


def _ret_tables(n_rh, chunk, dk, dv):
    log_g = np.log1p(-np.exp2(-5.0 - np.arange(n_rh, dtype=np.float64)))
    idx = np.arange(chunk, dtype=np.float64)
    diff = idx[:, None] - idx[None, :]
    mask = np.where(diff >= 0, np.exp(log_g[:, None, None] * np.maximum(diff, 0.0)), 0.0)
    qd = np.exp(log_g[:, None] * (idx + 1.0))
    kd = np.exp(log_g[:, None] * (chunk - 1.0 - idx))
    cd = np.exp(log_g * chunk)
    qd = np.broadcast_to(qd[:, :, None], (n_rh, chunk, dv))
    kd = np.broadcast_to(kd[:, :, None], (n_rh, chunk, dk))
    return (jnp.asarray(mask, F32), jnp.asarray(qd, F32), jnp.asarray(kd, F32),
            tuple(float(np.float32(v)) for v in cd), tuple(float(np.float32(v)) for v in np.exp(log_g)))


def _ret_sample_kernel(qt_ref, kt_ref, v_ref, gate_ref, st_ref, gn_ref, o_ref, ns_ref,
                       *, n_rh, dk, dv, gamma):
    seqs = v_ref.shape[0]
    lane = lax.broadcasted_iota(jnp.int32, qt_ref.shape, 1)
    for j in range(seqs):
        sel = lane == pl.program_id(0) * seqs + j
        qcol = jnp.sum(jnp.where(sel, qt_ref[...], 0.0), axis=1, keepdims=True)
        kcol = jnp.sum(jnp.where(sel, kt_ref[...], 0.0), axis=1, keepdims=True)
        v = v_ref[j]
        gate = gate_ref[j]
        for i in range(n_rh):
            vi = v[:, i * dv:(i + 1) * dv]
            new = st_ref[j, i] * gamma[i] + kcol[i * dk:(i + 1) * dk] * vi
            ns_ref[j, i] = new
            o = jnp.sum(qcol[i * dk:(i + 1) * dk] * new, axis=0, keepdims=True)
            o_ref[j, :, i * dv:(i + 1) * dv] = _rms(o, gn_ref[i]) * gate[:, i * dv:(i + 1) * dv]


def _ret_sample(rq, rk, rv, rgs, state, gn, n_rh, dk, dv):
    nb = rq.shape[0]
    _, _, _, _, gamma = _ret_tables(n_rh, 1, dk, dv)
    qt = rq.astype(F32).T
    kt = rk.astype(F32).T
    seqs = math.gcd(nb, RET_SAMPLE_SEQS)
    row = lambda w: pl.BlockSpec((seqs, 1, w), lambda b: (b, 0, 0))
    st_spec = pl.BlockSpec((seqs, n_rh, dk, dv), lambda b: (b, 0, 0, 0))
    in_specs = [_const_spec(qt.shape), _const_spec(kt.shape), row(n_rh * dv), row(n_rh * dv), st_spec,
                _const_spec((n_rh, 1, dv))]
    out_shape = (jax.ShapeDtypeStruct((nb, 1, n_rh * dv), F32),
                 jax.ShapeDtypeStruct((nb, n_rh, dk, dv), F32))
    kern = functools.partial(_ret_sample_kernel, n_rh=n_rh, dk=dk, dv=dv, gamma=gamma)
    o, ns = pl.pallas_call(kern, out_shape=out_shape, grid=(nb // seqs,), in_specs=in_specs,
                           out_specs=(row(n_rh * dv), st_spec), compiler_params=_cparams(1),
                           name="retention_sample")(
        qt, kt, rv.astype(F32).reshape(nb, 1, -1), rgs.astype(F32).reshape(nb, 1, -1), state,
        gn.reshape(n_rh, 1, dv))
    return o.reshape(nb, n_rh * dv).astype(BF16), ns


def _attn_prompt_kernel(q_ref, k_ref, v_ref, o_ref, m_scr, acc_scr, *, tq, hpg, sum_lane):
    i = pl.program_id(2)
    m_scr[...] = jnp.full(m_scr.shape, -jnp.inf, F32)
    acc_scr[...] = jnp.zeros(acc_scr.shape, F32)

    def tile(j, masked):
        start = pl.multiple_of(j * tq, tq)
        for h in range(hpg):
            s = _dot_nt(q_ref[h], k_ref[h, pl.ds(start, tq), :])
            if masked:
                row = lax.broadcasted_iota(jnp.int32, s.shape, 0)
                col = lax.broadcasted_iota(jnp.int32, s.shape, 1)
                s = jnp.where(col <= row, s, -jnp.inf)
            m = m_scr[h]
            m_new = jnp.maximum(m, jnp.max(s, axis=1, keepdims=True))
            m_wide = jnp.concatenate([m_new] * (tq // HEAD_PAD), axis=1)
            p = jnp.exp2((s - m_wide).astype(BF16))
            acc_scr[h] = jnp.exp2(m - m_new) * acc_scr[h] + _dot(p, v_ref[h, pl.ds(start, tq), :])
            m_scr[h] = m_new

    def body(j, carry):
        tile(j, False)
        return carry

    lax.fori_loop(0, i, body, 0)
    tile(i, True)
    for h in range(hpg):
        acc = acc_scr[h]
        o = acc / acc[:, sum_lane:sum_lane + 1]
        o_ref[:, h * sum_lane:(h + 1) * sum_lane] = o[:, :sum_lane].astype(BF16)


def _attn_prompt(q, k, v, b, t, tq, hpg, sum_lane):
    n_mh, n, _ = q.shape
    nq = t // tq
    kv_spec = pl.BlockSpec((hpg, t, HEAD_PAD), lambda bi, g, i: (g, bi, 0), pipeline_mode=pl.Buffered(1))
    in_specs = [pl.BlockSpec((hpg, tq, HEAD_PAD), lambda bi, g, i: (g, bi * nq + i, 0)), kv_spec, kv_spec]
    out_specs = pl.BlockSpec((tq, hpg * sum_lane), lambda bi, g, i: (bi * nq + i, g))
    return pl.pallas_call(functools.partial(_attn_prompt_kernel, tq=tq, hpg=hpg, sum_lane=sum_lane),
                          out_shape=jax.ShapeDtypeStruct((n, n_mh * sum_lane), BF16),
                          grid=(b, n_mh // hpg, nq), in_specs=in_specs, out_specs=out_specs,
                          scratch_shapes=[pltpu.VMEM((hpg, tq, HEAD_PAD), F32),
                                          pltpu.VMEM((hpg, tq, HEAD_PAD), F32)],
                          compiler_params=_cparams(3), name="attn_prompt")(q, k, v)


def _attn_sample_kernel(pt_ref, ql_ref, qp_ref, cn_ref, rn_ref, lat_hbm, rope_hbm, o_ref,
                        lat_slab, rope_slab, sems, m_scr, l_scr, acc_scr, *, npg, n_chunks, page):
    step = pl.program_id(0)
    n_steps = pl.num_programs(0)
    n_slots = lat_slab.shape[0]
    ahead = n_slots - 1
    slot = step % n_slots
    chunk = step % n_chunks

    def page_copies(page_ids, slot_):
        out = []
        for i in range(npg):
            rows = pl.ds(i * page, page)
            out.append(pltpu.make_async_copy(lat_hbm.at[page_ids(i)], lat_slab.at[slot_, rows, :],
                                             sems.at[0, slot_]))
            out.append(pltpu.make_async_copy(rope_hbm.at[page_ids(i)], rope_slab.at[slot_, :, rows],
                                             sems.at[1, slot_]))
        return out

    def start_step(step_, slot_):
        seq = step_ // n_chunks
        first = (step_ % n_chunks) * npg
        for cp in page_copies(lambda i: pt_ref[seq, first + i], slot_):
            cp.start()

    for k in range(ahead):
        @pl.when(jnp.logical_and(step == 0, k < n_steps))
        def _():
            start_step(k, k)

    @pl.when(step + ahead < n_steps)
    def _():
        start_step(step + ahead, (step + ahead) % n_slots)

    for cp in page_copies(lambda i: 0, slot):
        cp.wait()

    ql = ql_ref[0]
    qp = qp_ref[0]

    @pl.when(chunk == 0)
    def _():
        cn = cn_ref[0]
        s0 = (jnp.sum(ql * cn, axis=1, keepdims=True)
              + jnp.sum(qp * rn_ref[0], axis=1, keepdims=True))
        m_scr[...] = s0
        l_scr[...] = jnp.ones_like(l_scr)
        acc_scr[...] = jnp.broadcast_to(cn, acc_scr.shape)

    lat = lat_slab[slot].astype(BF16)
    s = _dot_nt(ql.astype(BF16), lat) + _dot(qp, rope_slab[slot])
    m_prev = m_scr[...]
    m_new = jnp.maximum(m_prev, jnp.max(s, axis=1, keepdims=True))
    alpha = jnp.exp2(m_prev - m_new)
    p = jnp.exp2(s - m_new)
    m_scr[...] = m_new
    l_scr[...] = alpha * l_scr[...] + jnp.sum(p, axis=1, keepdims=True)
    acc_scr[...] = alpha * acc_scr[...] + _dot(p.astype(BF16), lat)

    @pl.when(chunk == n_chunks - 1)
    def _():
        o_ref[0] = acc_scr[...] / l_scr[...]


def _attn_sample(page_table, q_lat, q_pe, c_new, r_new, cache_lat, cache_rope_t, npg):
    nb, n_mh, kv_rank = q_lat.shape
    rope_dim = q_pe.shape[-1]
    n_pages = page_table.shape[1]
    page = cache_lat.shape[1]
    n_chunks = n_pages // npg
    per_seq = lambda shp: pl.BlockSpec((1,) + shp, lambda s, pt: (s // n_chunks, 0, 0))
    hbm = pl.BlockSpec(memory_space=pl.ANY)
    in_specs = [per_seq((n_mh, kv_rank)), per_seq((n_mh, rope_dim)), per_seq((1, kv_rank)),
                per_seq((1, rope_dim)), hbm, hbm]
    grid_spec = pltpu.PrefetchScalarGridSpec(
        num_scalar_prefetch=1, grid=(nb * n_chunks,), in_specs=in_specs,
        out_specs=per_seq((n_mh, kv_rank)),
        scratch_shapes=[pltpu.VMEM((DECODE_SLOTS, npg * page, kv_rank), F32),
                        pltpu.VMEM((DECODE_SLOTS, rope_dim, npg * page), F32),
                        pltpu.SemaphoreType.DMA((2, DECODE_SLOTS)),
                        pltpu.VMEM((n_mh, 1), F32), pltpu.VMEM((n_mh, 1), F32),
                        pltpu.VMEM((n_mh, kv_rank), F32)])
    kern = functools.partial(_attn_sample_kernel, npg=npg, n_chunks=n_chunks, page=page)
    return pl.pallas_call(kern, out_shape=jax.ShapeDtypeStruct((nb, n_mh, kv_rank), F32),
                          grid_spec=grid_spec, compiler_params=_cparams(1), name="attn_sample")(
        page_table, q_lat, q_pe, c_new, r_new, cache_lat, cache_rope_t)


def _headmm_kernel(a_ref, w_ref, o_ref):
    o_ref[0] = _dot(a_ref[0].astype(BF16), w_ref[0]).astype(o_ref.dtype)


def _headmm(a, w, out_dtype):
    nh, m, kk = a.shape
    nn = w.shape[-1]
    spec = lambda r, c: pl.BlockSpec((1, r, c), lambda h: (h, 0, 0))
    return pl.pallas_call(_headmm_kernel, out_shape=jax.ShapeDtypeStruct((nh, m, nn), out_dtype),
                          grid=(nh,), in_specs=[spec(m, kk), spec(kk, nn)], out_specs=spec(m, nn),
                          compiler_params=_cparams(1), name="head_matmul")(a, w)


def _router_logits(h2, whi_ref, wlo_ref):
    hi = h2.astype(BF16)
    lo = (h2 - hi.astype(F32)).astype(BF16)
    lg = _dot(hi, whi_ref[...]) + (_dot(hi, wlo_ref[...]) + _dot(lo, whi_ref[...]))
    return hi, lg[:, :LANES], lg[:, LANES:]


def _merge_kernel(x_ref, ro_ref, gr_ref, mo_ref, gm_ref, wro_ref, wmo_ref, wout_ref, gffn_ref,
                  whi_ref, wlo_ref, x1_o, gid_o, *, group_size, n_exp):
    mixed = (_dot(ro_ref[...], wro_ref[...]) * gr_ref[...].astype(F32)
             + _dot(mo_ref[...], wmo_ref[...]) * gm_ref[...].astype(F32))
    x1 = x_ref[...] + _dot(mixed.astype(BF16), wout_ref[...])
    x1_o[...] = x1
    _, ge, _ = _router_logits(_rms(x1, gffn_ref[...]), whi_ref, wlo_ref)
    lane = lax.broadcasted_iota(jnp.int32, ge.shape, 1)
    ge = jnp.where(lane < n_exp, ge, -jnp.inf)
    grp = (lane // group_size).astype(F32)
    gmax = jnp.max(ge, axis=1, keepdims=True)
    g_idx = jnp.min(jnp.where(ge == gmax, grp, float(n_exp)), axis=1, keepdims=True)
    gid_o[...] = jnp.broadcast_to(g_idx, ge.shape).astype(jnp.int32)


def _merge(x, ro, gr, mo, gm, wts, group_size, n_exp, tm):
    n, d_model = x.shape
    tok = lambda w: pl.BlockSpec((tm, w), lambda i: (i, 0))
    w_args = [wts[k] for k in ("wro", "wmo", "wout", "gffn", "wr_hi", "wr_lo")]
    in_specs = [tok(d_model), tok(ro.shape[1]), tok(d_model), tok(mo.shape[1]), tok(d_model)] + [
        _const_spec(a.shape) for a in w_args]
    out_shape = (jax.ShapeDtypeStruct((n, d_model), F32), jax.ShapeDtypeStruct((n, LANES), jnp.int32))
    kern = functools.partial(_merge_kernel, group_size=group_size, n_exp=n_exp)
    return pl.pallas_call(kern, out_shape=out_shape, grid=(n // tm,), in_specs=in_specs,
                          out_specs=(tok(d_model), tok(LANES)),
                          compiler_params=_cparams(1), name="merge_router")(x, ro, gr, mo, gm, *w_args)


def _group_tiles(gid, n_grp, tmr):
    n = gid.shape[0]
    token = jnp.arange(n, dtype=jnp.int32)
    counts = jnp.sum((gid[:, None] == jnp.arange(n_grp, dtype=jnp.int32)[None, :]).astype(jnp.int32), axis=0)
    by_group = jnp.sort(gid * n + token) % n
    dense_start = jnp.cumsum(counts) - counts
    padded = ((counts + tmr - 1) // tmr) * tmr
    ends = jnp.cumsum(padded)
    offs = ends - padded
    n_tiles = n // tmr + n_grp
    tile_start = jnp.arange(n_tiles, dtype=jnp.int32) * tmr
    tile_group = jnp.minimum(jnp.sum((tile_start[:, None] >= ends[None, :]).astype(jnp.int32), axis=1),
                             n_grp - 1)
    tile_valid = jnp.clip(counts[tile_group] - (tile_start - offs[tile_group]), 0, tmr)
    row_group = jnp.repeat(tile_group, tmr)
    in_group = jnp.arange(n_tiles * tmr, dtype=jnp.int32) - offs[row_group]
    src = jnp.clip(dense_start[row_group] + in_group, 0, n - 1)
    row_token = jnp.where(in_group < counts[row_group], by_group[src], 0)
    return row_token, tile_group, tile_valid.astype(jnp.int32)


def _moe_kernel(rt_ref, tg_ref, tv_ref, x1_hbm, gffn_ref, gfin_ref, whi_ref, wlo_ref, wg_ref, wu_ref, wd_ref,
                y_hbm, xbuf, obuf, trash, gsem, ssem, *, tmr, group_size, n_exp):
    t = pl.program_id(0)
    last = pl.num_programs(0) - 1
    slot = t % 2

    def gather_copy(tok, r, slot_):
        return pltpu.make_async_copy(x1_hbm.at[pl.ds(tok, 1)], xbuf.at[slot_, pl.ds(r, 1)], gsem.at[slot_])

    def start_gather(tile, slot_):
        for r in range(tmr):
            gather_copy(rt_ref[tile * tmr + r], r, slot_).start()

    def wait_gather(slot_):
        for r in range(tmr):
            gather_copy(0, r, slot_).wait()

    def scatter_copy(r, dst_row_ref, slot_):
        return pltpu.make_async_copy(obuf.at[slot_, pl.ds(r, 1)], dst_row_ref, ssem.at[slot_])

    def wait_scatter(slot_):
        for r in range(tmr):
            scatter_copy(r, trash.at[slot_, pl.ds(r, 1)], slot_).wait()

    @pl.when(t == 0)
    def _():
        start_gather(t, slot)

    wait_gather(slot)

    @pl.when(jnp.logical_and(t >= 2, tv_ref[jnp.maximum(t - 2, 0)] > 0))
    def _():
        wait_scatter(slot)

    start_gather(jnp.minimum(t + 1, last), 1 - slot)

    n_real = tv_ref[t]

    @pl.when(n_real > 0)
    def _():
        g = tg_ref[t]
        x = xbuf[slot]
        hb, ge, el = _router_logits(_rms(x, gffn_ref[...]), whi_ref, wlo_ref)
        lane_i = lax.broadcasted_iota(jnp.int32, ge.shape, 1)
        lane = lane_i.astype(F32)
        is_exp = lane_i < n_exp
        in_grp = (lane_i // group_size) == g
        gmax = jnp.max(jnp.where(is_exp, ge, -jnp.inf), axis=1, keepdims=True)
        gsum = jnp.sum(jnp.where(is_exp, jnp.exp(ge - gmax), 0.0), axis=1, keepdims=True) / group_size
        ge_g = jnp.max(jnp.where(in_grp, ge, -jnp.inf), axis=1, keepdims=True)
        g_w = jnp.exp(ge_g - gmax) / gsum
        e_in = jnp.where(in_grp, el, -jnp.inf)
        top1 = jnp.max(e_in, axis=1, keepdims=True)
        idx1 = jnp.min(jnp.where(e_in == top1, lane, float(LANES)), axis=1, keepdims=True)
        e_rest = jnp.where(lane == idx1, -jnp.inf, e_in)
        top2 = jnp.max(e_rest, axis=1, keepdims=True)
        idx2 = jnp.min(jnp.where(e_rest == top2, lane, float(LANES)), axis=1, keepdims=True)
        e2 = jnp.exp(top2 - top1)
        comb = (jnp.where(lane == idx1, g_w / (1.0 + e2), 0.0)
                + jnp.where(lane == idx2, g_w * e2 / (1.0 + e2), 0.0))

        first = (g * group_size).astype(F32)
        parts = []
        for e in range(group_size):
            ce = jnp.sum(jnp.where(lane == first + e, comb, 0.0), axis=1, keepdims=True)
            hg = _dot(hb, wg_ref[e])
            parts.append((hg * _sigmoid(hg) * _dot(hb, wu_ref[e]) * ce).astype(BF16))
        y = x + _dot(jnp.concatenate(parts, axis=1), wd_ref[0])
        obuf[slot] = _rms(y, gfin_ref[...])

    def scatter_real(r, carry):
        tok = rt_ref[t * tmr + r]
        scatter_copy(r, y_hbm.at[pl.ds(tok, 1)], slot).start()
        return carry

    def scatter_pad(r, carry):
        scatter_copy(r, trash.at[slot, pl.ds(r, 1)], slot).start()
        return carry

    @pl.when(n_real == tmr)
    def _():
        for r in range(tmr):
            scatter_real(r, 0)

    @pl.when(jnp.logical_and(n_real > 0, n_real < tmr))
    def _():
        lax.fori_loop(0, n_real, scatter_real, 0)
        lax.fori_loop(n_real, tmr, scatter_pad, 0)

    @pl.when(t == last)
    def _():
        wait_gather(1 - slot)

        @pl.when(n_real > 0)
        def _():
            wait_scatter(slot)

        @pl.when(jnp.logical_and(t >= 1, tv_ref[jnp.maximum(t - 1, 0)] > 0))
        def _():
            wait_scatter(1 - slot)


def _moe(x1, gid, wts, wg, wu, wd, gfin, group_size, n_exp, tmr):
    n, d_model = x1.shape
    n_grp = wd.shape[0]
    row_token, tile_group, tile_valid = _group_tiles(gid, n_grp, tmr)
    n_tiles = tile_group.shape[0]
    hbm = pl.BlockSpec(memory_space=pl.ANY)
    const = lambda a: pl.BlockSpec(a.shape, lambda t, rt, tg, tv: (0,) * a.ndim, pipeline_mode=pl.Buffered(1))
    grp_w = lambda a: pl.BlockSpec((a.shape[0] // n_grp,) + a.shape[1:], lambda t, rt, tg, tv: (tg[t], 0, 0))
    w_args = [wts["gffn"], gfin, wts["wr_hi"], wts["wr_lo"]]
    grid_spec = pltpu.PrefetchScalarGridSpec(
        num_scalar_prefetch=3, grid=(n_tiles,),
        in_specs=[hbm] + [const(a) for a in w_args] + [grp_w(wg), grp_w(wu), grp_w(wd)],
        out_specs=hbm,
        scratch_shapes=[pltpu.VMEM((2, tmr, d_model), F32), pltpu.VMEM((2, tmr, d_model), F32),
                        pltpu.VMEM((2, tmr, d_model), F32),
                        pltpu.SemaphoreType.DMA((2,)), pltpu.SemaphoreType.DMA((2,))])
    kern = functools.partial(_moe_kernel, tmr=tmr, group_size=group_size, n_exp=n_exp)
    return pl.pallas_call(kern, out_shape=jax.ShapeDtypeStruct((n, d_model), F32), grid_spec=grid_spec,
                          compiler_params=_cparams(1), name="moe")(
        row_token, tile_group, tile_valid, x1, *w_args, wg, wu, wd)


def _rope_tables(start, count, repeat, dk, rope_dim, nope):
    n_hi = -(-count // LANES)

    def cs(half):
        inv = ROPE_BASE ** (-jnp.arange(half, dtype=F32) / half)
        a = (start + LANES * jnp.arange(n_hi, dtype=F32))[:, None, None] * inv
        b = jnp.arange(LANES, dtype=F32)[None, :, None] * inv
        ca, sa, cb, sb = jnp.cos(a), jnp.sin(a), jnp.cos(b), jnp.sin(b)
        c = (ca * cb - sa * sb).reshape(n_hi * LANES, half)[:count]
        s = (sa * cb + ca * sb).reshape(n_hi * LANES, half)[:count]
        return jnp.repeat(c, repeat, axis=0), jnp.repeat(s, repeat, axis=0)

    c, s = cs(dk // 2)
    cosr = jnp.concatenate([c, c], axis=1)
    sinr = jnp.concatenate([-s, s], axis=1)
    c, s = cs(rope_dim // 2)
    n = count * repeat
    tail = jnp.zeros((n, HEAD_PAD - nope - rope_dim), F32)
    cp = jnp.concatenate([jnp.ones((n, nope), F32), c, c, tail], axis=1)
    sp = jnp.concatenate([jnp.zeros((n, nope), F32), -s, s, tail], axis=1)
    return cosr, sinr, cp, sp


def _swap_halves(w):
    half = w.shape[-1] // 2
    return jnp.concatenate([w[..., half:], w[..., :half]], axis=-1)


def _layout_weights(norm_mix_g, w_in, q_norm_g, kv_norm_g, w_uq, w_ukv, w_ret_o, w_mla_o, w_out,
                    norm_ffn_g, w_router_group, w_router_expert, dims):
    n_rh, dk, dv, q_rank, kv_rank, rope_dim, n_mh, nope, dv_m = dims
    d_model = w_in.shape[0]
    qk_w, v_w = n_rh * dk, n_rh * dv
    o_cq = 2 * qk_w + 2 * v_w
    o_ckv = o_cq + q_rank
    o_kr = o_ckv + kv_rank
    o_gate = o_kr + rope_dim
    pad_tail = HEAD_PAD - nope - rope_dim
    zeros = lambda *shape: jnp.zeros(shape, F32)
    w_kr = w_in[:, o_kr:o_gate]
    place = lambda w: jnp.concatenate([zeros(d_model, nope), w, zeros(d_model, pad_tail)], axis=1)
    wsmall = jnp.concatenate([w_in[:, o_cq:o_kr], place(w_kr), place(_swap_halves(w_kr))], axis=1)
    wmain = jnp.concatenate([w_in[:, :o_cq], w_in[:, o_gate:]], axis=1)

    uq = w_uq.reshape(q_rank, n_mh, nope + rope_dim)
    uq_rope = uq[..., nope:]
    wq = jnp.concatenate([uq, zeros(q_rank, n_mh, pad_tail)], axis=-1)
    wqsw = jnp.concatenate([zeros(q_rank, n_mh, nope), _swap_halves(uq_rope),
                            zeros(q_rank, n_mh, pad_tail)], axis=-1)
    uk = w_ukv[..., :nope]
    uv = w_ukv[..., nope:]
    wk = jnp.concatenate([uk, zeros(kv_rank, n_mh, HEAD_PAD - nope)], axis=-1)
    wv = jnp.concatenate([uv, zeros(kv_rank, n_mh, HEAD_PAD - dv_m)], axis=-1)
    flat = lambda w: w.reshape(w.shape[0], n_mh * HEAD_PAD).astype(BF16)
    wmo = w_mla_o
    wabs = jnp.concatenate([jnp.transpose(uk, (1, 2, 0)), zeros(n_mh, HEAD_PAD - nope, kv_rank)], axis=1)
    wvh = jnp.transpose(uv, (1, 0, 2))
    n_grp = w_router_group.shape[1]
    n_exp = w_router_expert.shape[1]
    group_size = n_exp // n_grp
    assert n_exp <= LANES
    w_r = jnp.concatenate([jnp.repeat(w_router_group, group_size, axis=1), zeros(d_model, LANES - n_exp),
                           w_router_expert, zeros(d_model, LANES - n_exp)], axis=1)
    wr_hi = w_r.astype(BF16)
    wr_lo = (w_r - wr_hi.astype(F32)).astype(BF16)
    return dict(
        gmix=norm_mix_g.reshape(1, -1), wmain=wmain.astype(BF16), wsmall=wsmall.astype(BF16),
        gq=q_norm_g.reshape(1, -1), gkv=kv_norm_g.reshape(1, -1),
        wq=flat(wq), wqsw=flat(wqsw), wk=flat(wk), wv=flat(wv),
        wro=w_ret_o.astype(BF16), wmo=wmo.astype(BF16), wout=w_out.astype(BF16),
        gffn=norm_ffn_g.reshape(1, -1), wr_hi=wr_hi, wr_lo=wr_lo,
        wabs=wabs.astype(BF16), wvh=wvh.astype(BF16)), group_size, n_exp


def kernel(x_prompt, x_sample, cache_kv_latent, cache_k_rope, state_retention, page_table, norm_mix_g,
           w_in, q_norm_g, kv_norm_g, w_uq, w_ukv, ret_gn_g, w_ret_o, w_mla_o, w_out, norm_ffn_g,
           w_router_group, w_router_expert, w_gate, w_up, w_down, norm_final_g):
    depth = w_in.shape[0]
    assert depth == 1, "single-layer step only"
    b, t, d_model = x_prompt.shape
    bd, tn, _ = x_sample.shape
    assert tn == 1, "one new token per sample sequence"
    _, _, n_rh, dk, dv = state_retention.shape
    kv_rank = cache_kv_latent.shape[-1]
    rope_dim = cache_k_rope.shape[-1]
    page = cache_kv_latent.shape[2]
    n_mh = w_ukv.shape[2]
    q_rank = w_uq.shape[1]
    nope = w_uq.shape[2] // n_mh - rope_dim
    dv_m = w_ukv.shape[3] - nope
    assert nope + rope_dim <= HEAD_PAD and dv_m < HEAD_PAD
    dims = (n_rh, dk, dv, q_rank, kv_rank, rope_dim, n_mh, nope, dv_m)
    past_len = page_table.shape[1] * page

    wts, group_size, n_exp = _layout_weights(norm_mix_g[0], w_in[0], q_norm_g[0], kv_norm_g[0], w_uq[0],
                                             w_ukv[0], w_ret_o[0], w_mla_o[0], w_out[0], norm_ffn_g[0],
                                             w_router_group[0], w_router_expert[0], dims)
    gn = ret_gn_g[0]
    n_grp = n_exp // group_size
    d_exp = w_gate.shape[-1]
    wg, wu = w_gate[0].astype(BF16), w_up[0].astype(BF16)
    wd = w_down[0].astype(BF16).reshape(n_grp, group_size * d_exp, d_model)
    gfin = norm_final_g.reshape(1, -1)

    tm_p = min(INPROJ_ROW_TILE, t)
    xp = x_prompt.reshape(b * t, d_model)
    tabs_p = _rope_tables(0.0, t, 1, dk, rope_dim, nope)
    (ret_o, st_p, gret, gmla, lat_p, rope_p, q_p, k_p, v_p) = _inproj(xp, tabs_p, wts, dims, tm_p,
                                                                      ret_gn=gn, n_seq=b)
    mla_o = _attn_prompt(q_p, k_p, v_p, b, t, min(ATTN_TQ, t), min(ATTN_HEADS_PER_STEP, n_mh), dv_m)
    x1, gid = _merge(xp, ret_o, gret, mla_o, gmla, wts, group_size, n_exp,
                     min(MERGE_ROW_TILE, t))
    y_prompt = _moe(x1, gid[:, 0], wts, wg, wu, wd, gfin, group_size, n_exp,
                    min(MOE_ROW_TILE, b * t)).reshape(b, t, d_model)

    xs = x_sample.reshape(bd, d_model)
    tabs_s = _rope_tables(float(past_len), tn, bd, dk, rope_dim, nope)
    (rq, rk, rv, rgs, gret, gmla, lat_s, rope_s, q_s, _, _) = _inproj(xs, tabs_s, wts, dims, bd)
    ret_o_s, st_s = _ret_sample(rq, rk, rv, rgs, state_retention[0], gn, n_rh, dk, dv)
    q_lat = jnp.transpose(_headmm(q_s, wts["wabs"], F32), (1, 0, 2))
    q_pe = jnp.transpose(q_s[:, :, nope:nope + rope_dim].astype(F32), (1, 0, 2))
    o_lat = _attn_sample(page_table, q_lat, q_pe, lat_s.reshape(bd, 1, kv_rank),
                         rope_s.reshape(bd, 1, rope_dim), cache_kv_latent.reshape(-1, page, kv_rank),
                         jnp.transpose(cache_k_rope.reshape(-1, page, rope_dim), (0, 2, 1)),
                         min(DECODE_PAGES, page_table.shape[1]))
    mla_o_s = _headmm(jnp.transpose(o_lat, (1, 0, 2)), wts["wvh"], BF16)
    mla_o_s = jnp.transpose(mla_o_s, (1, 0, 2)).reshape(bd, n_mh * dv_m)
    x1, gid = _merge(xs, ret_o_s, gret, mla_o_s, gmla, wts, group_size, n_exp, bd)
    y_sample = _moe(x1, gid[:, 0], wts, wg, wu, wd, gfin, group_size, n_exp,
                    min(MOE_ROW_TILE, bd)).reshape(bd, tn, d_model)

    return (y_prompt, y_sample,
            lat_p.reshape(depth, b, t // page, page, kv_rank),
            rope_p.reshape(depth, b, t // page, page, rope_dim),
            st_p.reshape(depth, b, n_rh, dk, dv),
            lat_s.reshape(depth, bd, tn, kv_rank),
            rope_s.reshape(depth, bd, tn, rope_dim),
            st_s.reshape(depth, bd, n_rh, dk, dv))
```
